```python
import math
import jax, jax.numpy as jnp
from jax import lax
import numpy as np

D_MODEL = 2048
BATCH = 4
SEQ = 2048
DEPTH = 2

CHUNK = 64
Q_BLOCK = 128
A_WIDTH = D_MODEL // 4
A_HEAD_DIM = 128
A_HALF = A_HEAD_DIM // 2
A_HEADS = A_WIDTH // A_HEAD_DIM
B_WIDTH = 3 * D_MODEL // 8
B_HEAD_DIM = 128
B_HEADS = B_WIDTH // B_HEAD_DIM
B_PAST_CHUNKS = 8
B_BAND = B_PAST_CHUNKS + 1
REL_CLIP = 256
C_WIDTH = D_MODEL - A_WIDTH - B_WIDTH
C_HEADS = 4
C_HEAD_DIM = C_WIDTH // C_HEADS
CONV_K = 4
MIX_WIDTH = A_WIDTH + B_WIDTH + C_WIDTH
D_FF = (8 * D_MODEL + 3 * 256 - 1) // (3 * 256) * 256
SPLITS = [A_WIDTH] * 3 + [B_WIDTH] * 3 + [C_WIDTH] * 4 + [C_HEADS] * 2
IN_WIDTH = sum(SPLITS)
SPLIT_POINTS = [int(v) for v in np.cumsum(SPLITS)[:-1]]
DEEPNORM_ALPHA = (2.0 * DEPTH) ** 0.25
DEEPNORM_BETA = (8.0 * DEPTH) ** -0.25
ALIBI_SLOPES = tuple(2.0 ** (-8.0 * (h + 1) / A_HEADS) for h in range(A_HEADS))
NEG = -1e30

kernel_name = "hybrid_diffattn_chunkband_mlstm_deepnorm"


def layer_norm(x, g, b, eps=1e-5):
    xf = x.astype(jnp.float32)
    mu = xf.mean(-1, keepdims=True)
    var = jnp.mean(jnp.square(xf - mu), -1, keepdims=True)
    return ((xf - mu) * lax.rsqrt(var + eps) * g.astype(jnp.float32) + b.astype(jnp.float32)).astype(x.dtype)


def rms_norm(x, g, eps=1e-6):
    xf = x.astype(jnp.float32)
    ms = jnp.mean(jnp.square(xf), -1, keepdims=True)
    return (xf * lax.rsqrt(ms + eps) * g.astype(jnp.float32)).astype(x.dtype)


def causal_depthwise_conv(x, w):
    return lax.conv_general_dilated(
        x, w[:, None, :], window_strides=(1,), padding=[(CONV_K - 1, 0)],
        dimension_numbers=("NWC", "WIO", "NWC"), feature_group_count=x.shape[-1])


def diff_attention(q, k, v, lam, subln, layer_idx):
    B, S, _ = q.shape
    nb = S // Q_BLOCK
    q = q.reshape(B, S, A_HEADS, 2, A_HALF)
    k = k.reshape(B, S, A_HEADS, 2, A_HALF)
    vh = v.reshape(B, S, A_HEADS, A_HEAD_DIM).transpose(0, 2, 1, 3)
    qb = q.reshape(B, nb, Q_BLOCK, A_HEADS, 2, A_HALF).transpose(1, 0, 3, 4, 2, 5)
    kt = k.transpose(0, 2, 3, 1, 4)
    lam_init = 0.8 - 0.6 * math.exp(-0.3 * layer_idx)
    lf = lam.astype(jnp.float32)
    lam_full = jnp.exp(jnp.sum(lf[0] * lf[1])) - jnp.exp(jnp.sum(lf[2] * lf[3])) + lam_init
    slopes = jnp.asarray(ALIBI_SLOPES, dtype=jnp.float32)
    key_pos = jnp.arange(S)
    key_chunk = key_pos // CHUNK
    scale = A_HALF ** -0.5

    def block(args):
        bi, qblk = args
        qpos = bi * Q_BLOCK + jnp.arange(Q_BLOCK)
        s = jnp.einsum("bhmqd,bhmkd->bhmqk", qblk, kt).astype(jnp.float32) * scale
        dist = jnp.abs(qpos[:, None] - key_pos[None, :]).astype(jnp.float32)
        allowed = key_chunk[None, :] <= (qpos // CHUNK)[:, None]
        bias = jnp.where(allowed[None], -slopes[:, None, None] * dist[None], NEG)
        p = jax.nn.softmax(s + bias[None, :, None], axis=-1)
        w = p[:, :, 0] - lam_full * p[:, :, 1]
        return jnp.einsum("bhqk,bhkd->bhqd", w.astype(vh.dtype), vh)

    out = lax.map(block, (jnp.arange(nb), qb))
    out = out.transpose(1, 0, 3, 2, 4).reshape(B, S, A_HEADS, A_HEAD_DIM)
    out = rms_norm(out, subln) * (1.0 - lam_init)
    return out.reshape(B, S, A_WIDTH)


def chunk_band_attention(q, k, v, rel_bias):
    B, S, _ = q.shape
    nc = S // CHUNK

    def heads(t):
        return t.reshape(B, nc, CHUNK, B_HEADS, B_HEAD_DIM).transpose(0, 3, 1, 2, 4)

    qh, kh, vh = heads(q), heads(k), heads(v)
    pad = ((0, 0), (0, 0), (B_PAST_CHUNKS, 0), (0, 0), (0, 0))
    idx = jnp.arange(nc)[:, None] + jnp.arange(B_BAND)[None, :]

    def band(t):
        return jnp.pad(t, pad)[:, :, idx].reshape(B, B_HEADS, nc, B_BAND * CHUNK, B_HEAD_DIM)

    kb, vb = band(kh), band(vh)
    s = jnp.einsum("bhcqd,bhckd->bhcqk", qh, kb).astype(jnp.float32) * B_HEAD_DIM ** -0.5
    rel = (B_PAST_CHUNKS * CHUNK + jnp.arange(CHUNK))[:, None] - jnp.arange(B_BAND * CHUNK)[None, :]
    rel_idx = jnp.clip(rel, -REL_CLIP, REL_CLIP) + REL_CLIP
    bias = rel_bias.astype(jnp.float32)[:, rel_idx]
    valid = jnp.repeat(idx - B_PAST_CHUNKS >= 0, CHUNK, axis=1)
    s = jnp.where(valid[None, None, :, None, :], s + bias[None, :, None], NEG)
    p = jax.nn.softmax(s, axis=-1)
    out = jnp.einsum("bhcqk,bhckd->bhcqd", p.astype(vb.dtype), vb)
    return out.transpose(0, 2, 3, 1, 4).reshape(B, S, B_WIDTH)


def mlstm(q, k, v, o_pre, i_pre, f_pre, conv_w, norm_g):
    B, S, _ = q.shape
    nc = S // CHUNK
    out_dtype = v.dtype
    qk = jax.nn.silu(causal_depthwise_conv(jnp.concatenate([q, k], axis=-1), conv_w))
    q, k = jnp.split(qk, 2, axis=-1)

    def heads(t):
        return t.astype(jnp.float32).reshape(B, nc, CHUNK, C_HEADS, C_HEAD_DIM).transpose(1, 0, 3, 2, 4)

    def gates(t):
        return t.astype(jnp.float32).reshape(B, nc, CHUNK, C_HEADS).transpose(1, 0, 3, 2)

    qh, kh, vh = heads(q), heads(k) * C_HEAD_DIM ** -0.5, heads(v)
    ig = gates(i_pre)
    lfg = jax.nn.log_sigmoid(gates(f_pre))
    tri = jnp.tril(jnp.ones((CHUNK, CHUNK), dtype=bool))

    def step(carry, xs):
        C, n, m = carry
        qc, kc, vc, ic, fc = xs
        b = jnp.cumsum(fc, axis=-1)
        D = jnp.where(tri, b[..., :, None] - b[..., None, :] + ic[..., None, :], NEG)
        inter = b + m[..., None]
        m_t = jnp.maximum(inter, D.max(-1))
        Sw = jnp.einsum("bhtd,bhsd->bhts", qc, kc) * jnp.exp(D - m_t[..., None])
        w_inter = jnp.exp(inter - m_t)
        num = jnp.einsum("bhts,bhsd->bhtd", Sw, vc) + w_inter[..., None] * jnp.einsum("bhvk,bhtk->bhtv", C, qc)
        den = Sw.sum(-1) + w_inter * jnp.einsum("bhk,bhtk->bht", n, qc)
        h = num / jnp.maximum(jnp.abs(den), jnp.exp(-m_t))[..., None]
        bL = b[..., -1]
        g = bL[..., None] - b + ic
        m_new = jnp.maximum(bL + m, g.max(-1))
        wk = jnp.exp(g - m_new[..., None])
        decay = jnp.exp(bL + m - m_new)
        C = decay[..., None, None] * C + jnp.einsum("bhs,bhsv,bhsk->bhvk", wk, vc, kc)
        n = decay[..., None] * n + jnp.einsum("bhs,bhsk->bhk", wk, kc)
        return (C, n, m_new), h

    init = (jnp.zeros((B, C_HEADS, C_HEAD_DIM, C_HEAD_DIM), jnp.float32),
            jnp.zeros((B, C_HEADS, C_HEAD_DIM), jnp.float32),
            jnp.zeros((B, C_HEADS), jnp.float32))
    _, h = lax.scan(step, init, (qh, kh, vh, ig, lfg))
    h = h.transpose(1, 0, 3, 2, 4).reshape(B, S, C_HEADS, C_HEAD_DIM)
    h = rms_norm(h, norm_g).reshape(B, S, C_WIDTH)
    return (jax.nn.sigmoid(o_pre.astype(jnp.float32)) * h).astype(out_dtype)


def setup_inputs(seed: int = 0) -> dict:
    key = jax.random.key(seed)
    ks = jax.random.split(key, 20)
    x = jax.random.normal(ks[0], (BATCH, SEQ, D_MODEL), jnp.float32)
    col_scale = np.ones((IN_WIDTH,), np.float32)
    col_scale[2 * A_WIDTH:3 * A_WIDTH] = DEEPNORM_BETA
    off_b = 3 * A_WIDTH
    col_scale[off_b + 2 * B_WIDTH:off_b + 3 * B_WIDTH] = DEEPNORM_BETA
    off_c = off_b + 3 * B_WIDTH
    col_scale[off_c + 2 * C_WIDTH:off_c + 3 * C_WIDTH] = DEEPNORM_BETA
    w_in = jax.random.normal(ks[1], (DEPTH, D_MODEL, IN_WIDTH), jnp.float32) * (D_MODEL ** -0.5) * jnp.asarray(col_scale)
    i_bias = 0.1 * jax.random.normal(ks[2], (DEPTH, C_HEADS), jnp.float32)
    f_bias = jnp.linspace(3.0, 6.0, C_HEADS, dtype=jnp.float32)[None] + 0.1 * jax.random.normal(ks[3], (DEPTH, C_HEADS), jnp.float32)
    gate_bias = jnp.concatenate([i_bias, f_bias], axis=-1)
    conv_w = jax.random.normal(ks[4], (DEPTH, CONV_K, 2 * C_WIDTH), jnp.float32) * CONV_K ** -0.5
    lam = 0.1 * jax.random.normal(ks[5], (DEPTH, 4, A_HALF), jnp.float32)
    subln_a = 1.0 + 0.02 * jax.random.normal(ks[6], (DEPTH, A_HEAD_DIM), jnp.float32)
    norm_c = 1.0 + 0.02 * jax.random.normal(ks[7], (DEPTH, C_HEAD_DIM), jnp.float32)
    rel_bias = 0.1 * jax.random.normal(ks[8], (DEPTH, B_HEADS, 2 * REL_CLIP + 1), jnp.float32)
    w_out = jax.random.normal(ks[9], (DEPTH, MIX_WIDTH, D_MODEL), jnp.float32) * (MIX_WIDTH ** -0.5) * DEEPNORM_BETA
    ln1_g = 1.0 + 0.02 * jax.random.normal(ks[10], (DEPTH, D_MODEL), jnp.float32)
    ln1_b = 0.02 * jax.random.normal(ks[11], (DEPTH, D_MODEL), jnp.float32)
    w_gate = jax.random.normal(ks[12], (DEPTH, D_MODEL, D_FF), jnp.float32) * D_MODEL ** -0.5
    w_up = jax.random.normal(ks[13], (DEPTH, D_MODEL, D_FF), jnp.float32) * D_MODEL ** -0.5
    w_down = jax.random.normal(ks[14], (DEPTH, D_FF, D_MODEL), jnp.float32) * (D_FF ** -0.5) * DEEPNORM_BETA
    ln2_g = 1.0 + 0.02 * jax.random.normal(ks[15], (DEPTH, D_MODEL), jnp.float32)
    ln2_b = 0.02 * jax.random.normal(ks[16], (DEPTH, D_MODEL), jnp.float32)
    return {"x": x, "w_in": w_in, "gate_bias": gate_bias, "conv_w": conv_w, "lam": lam,
            "subln_a": subln_a, "norm_c": norm_c, "rel_bias": rel_bias, "w_out": w_out,
            "ln1_g": ln1_g, "ln1_b": ln1_b, "w_gate": w_gate, "w_up": w_up, "w_down": w_down,
            "ln2_g": ln2_g, "ln2_b": ln2_b}


def reference(x, w_in, gate_bias, conv_w, lam, subln_a, norm_c, rel_bias, w_out,
              ln1_g, ln1_b, w_gate, w_up, w_down, ln2_g, ln2_b):
    for li in range(DEPTH):
        proj = x @ w_in[li]
        qa, ka, va, qb, kb, vb, qc, kc, vc, oc, ic, fc = jnp.split(proj, SPLIT_POINTS, axis=-1)
        ic = ic + gate_bias[li, :C_HEADS]
        fc = fc + gate_bias[li, C_HEADS:]
        ya = diff_attention(qa, ka, va, lam[li], subln_a[li], li)
        yb = chunk_band_attention(qb, kb, vb, rel_bias[li])
        yc = mlstm(qc, kc, vc, oc, ic, fc, conv_w[li], norm_c[li])
        mix = jnp.concatenate([ya, yb, yc], axis=-1) @ w_out[li]
        x = layer_norm(DEEPNORM_ALPHA * x + mix, ln1_g[li], ln1_b[li])
        ffn = (jax.nn.silu(x @ w_gate[li]) * (x @ w_up[li])) @ w_down[li]
        x = layer_norm(DEEPNORM_ALPHA * x + ffn, ln2_g[li], ln2_b[li])
    return x
```

```python
import functools
import math

import numpy as np
import jax
import jax.numpy as jnp
from jax import lax
from jax.experimental import pallas as pl
from jax.experimental.pallas import tpu as pltpu

CHUNK = 64
A_HEADS = 4
A_HEAD_DIM = 128
A_HALF = 64
B_HEADS = 6
B_HEAD_DIM = 128
B_PAST_CHUNKS = 8
REL_CLIP = 256
C_HEADS = 4
C_HEAD_DIM = 192
C_PAD = 256
CONV_K = 4
NEG = -1e30
GATE_LANES = 128

A_WIDTH = A_HEADS * A_HEAD_DIM
B_WIDTH = B_HEADS * B_HEAD_DIM
C_WIDTH = C_HEADS * C_HEAD_DIM
CP_WIDTH = C_HEADS * C_PAD
OFF_C = 0
OFF_A = 4 * CP_WIDTH
OFF_B = OFF_A + 3 * A_WIDTH
PROJ_USED = OFF_B + 3 * B_WIDTH
PROJ_WIDTH = 8192

VMEM_LIMIT = 56 * 1024 * 1024

_NT = (((1,), (1,)), ((), ()))


def _params(sem):
    return pltpu.CompilerParams(dimension_semantics=sem, vmem_limit_bytes=VMEM_LIMIT)


def _dot(a, b):
    return jnp.dot(a, b, preferred_element_type=jnp.float32)


def _dot_nt(a, b):
    return lax.dot_general(a, b, _NT, preferred_element_type=jnp.float32)


def _in_proj_kernel(x_ref, w_ref, wg_ref, gb_ref, o_ref, g_ref):
    x = x_ref[...]
    o_ref[...] = _dot(x, w_ref[...]).astype(o_ref.dtype)

    @pl.when(pl.program_id(1) == 0)
    def _():
        g_ref[...] = _dot(x, wg_ref[...]) + gb_ref[...]


def _in_proj(xb, w_main, w_gate, gbias, tm=1024, tn=1024):
    m, k = xb.shape
    n = w_main.shape[1]
    tm, tn = min(tm, m), min(tn, n)
    return pl.pallas_call(
        _in_proj_kernel,
        grid=(m // tm, n // tn),
        in_specs=[
            pl.BlockSpec((tm, k), lambda i, j: (i, 0)),
            pl.BlockSpec((k, tn), lambda i, j: (0, j)),
            pl.BlockSpec((k, GATE_LANES), lambda i, j: (0, 0)),
            pl.BlockSpec((1, GATE_LANES), lambda i, j: (0, 0)),
        ],
        out_specs=[
            pl.BlockSpec((tm, tn), lambda i, j: (i, j)),
            pl.BlockSpec((tm, GATE_LANES), lambda i, j: (i, 0)),
        ],
        out_shape=[
            jax.ShapeDtypeStruct((m, n), jnp.bfloat16),
            jax.ShapeDtypeStruct((m, GATE_LANES), jnp.float32),
        ],
        compiler_params=_params(("parallel", "arbitrary")),
        name="in_proj",
    )(xb, w_main, w_gate, gbias)


def _attn_a_kernel(q_ref, k_ref, v_ref, lam_ref, g_ref, slope_ref, o_ref,
                   acc1, acc2, m1, l1, m2, l2, *, tq, lam_init):
    s_len = q_ref.shape[0]
    nq = s_len // tq
    lane = lax.broadcasted_iota(jnp.int32, (tq, A_HEAD_DIM), 1)
    first_half = lane < A_HALF
    slope = slope_ref[0:1, 0:1]
    lf = lam_ref[...]
    lam_full = (jnp.exp(jnp.sum(lf[0:1] * lf[1:2], axis=1, keepdims=True))
                - jnp.exp(jnp.sum(lf[2:3] * lf[3:4], axis=1, keepdims=True)) + lam_init)
    row = lax.broadcasted_iota(jnp.int32, (tq, tq), 0)
    col = lax.broadcasted_iota(jnp.int32, (tq, tq), 1)
    rc = (row - col).astype(jnp.float32)
    bias_off = -slope * rc
    bias_diag = -slope * jnp.abs(rc)
    diag_ok = (col // CHUNK) <= (row // CHUNK)
    zero = jnp.zeros((), jnp.bfloat16)

    def update(s, v, m_ref, l_ref, acc_ref):
        m_prev = m_ref[...]
        m_new = jnp.maximum(m_prev, jnp.max(s, axis=1, keepdims=True))
        alpha = jnp.exp(m_prev - m_new)
        p = jnp.exp(s - m_new)
        l_ref[...] = alpha * l_ref[...] + jnp.sum(p, axis=1, keepdims=True)
        acc_ref[...] = alpha * acc_ref[...] + _dot(p.astype(jnp.bfloat16), v)
        m_ref[...] = m_new

    def q_block(i, carry):
        qs = pl.multiple_of(i * tq, tq)
        q = q_ref[pl.ds(qs, tq), :] * jnp.asarray(A_HALF ** -0.5, jnp.bfloat16)
        q1 = jnp.where(first_half, q, zero)
        q2 = jnp.where(first_half, zero, q)
        for m_ref in (m1, m2):
            m_ref[...] = jnp.full(m_ref.shape, NEG, jnp.float32)
        for z_ref in (l1, l2, acc1, acc2):
            z_ref[...] = jnp.zeros(z_ref.shape, jnp.float32)

        def kv_block(j, c):
            ks = pl.multiple_of(j * tq, tq)
            k = k_ref[pl.ds(ks, tq), :]
            v = v_ref[pl.ds(ks, tq), :]
            bias = bias_off - slope * ((i - j) * tq).astype(jnp.float32)
            update(_dot_nt(q1, k) + bias, v, m1, l1, acc1)
            update(_dot_nt(q2, k) + bias, v, m2, l2, acc2)
            return c

        lax.fori_loop(0, i, kv_block, 0)
        k = k_ref[pl.ds(qs, tq), :]
        v = v_ref[pl.ds(qs, tq), :]
        update(jnp.where(diag_ok, _dot_nt(q1, k) + bias_diag, NEG), v, m1, l1, acc1)
        update(jnp.where(diag_ok, _dot_nt(q2, k) + bias_diag, NEG), v, m2, l2, acc2)

        o = acc1[...] / l1[...] - lam_full * (acc2[...] / l2[...])
        ms = jnp.mean(o * o, axis=1, keepdims=True)
        o = o * lax.rsqrt(ms + 1e-6) * g_ref[...] * (1.0 - lam_init)
        o_ref[pl.ds(qs, tq), :] = o.astype(o_ref.dtype)
        return carry

    lax.fori_loop(0, nq, q_block, 0)


def _attn_a(proj, lam, subln, li, tq=256):
    b, s, _ = proj.shape
    tq = min(tq, s)
    lam_init = 0.8 - 0.6 * math.exp(-0.3 * li)
    slopes = np.asarray([2.0 ** (-8.0 * (h + 1) / A_HEADS) for h in range(A_HEADS)], np.float32)
    slopes = jnp.asarray(np.broadcast_to(slopes[:, None, None], (A_HEADS, 1, 128)))
    qi = OFF_A // A_HEAD_DIM
    ki = (OFF_A + A_WIDTH) // A_HEAD_DIM
    vi = (OFF_A + 2 * A_WIDTH) // A_HEAD_DIM

    def hspec(base):
        return pl.BlockSpec((None, s, A_HEAD_DIM), lambda bi, h, base=base: (bi, 0, base + h))

    return pl.pallas_call(
        functools.partial(_attn_a_kernel, tq=tq, lam_init=lam_init),
        grid=(b, A_HEADS),
        in_specs=[
            hspec(qi), hspec(ki), hspec(vi),
            pl.BlockSpec((4, A_HALF), lambda bi, h: (0, 0)),
            pl.BlockSpec((1, A_HEAD_DIM), lambda bi, h: (0, 0)),
            pl.BlockSpec((None, 1, 128), lambda bi, h: (h, 0, 0)),
        ],
        out_specs=pl.BlockSpec((None, s, A_HEAD_DIM), lambda bi, h: (bi, 0, h)),
        out_shape=jax.ShapeDtypeStruct((b, s, A_WIDTH), jnp.bfloat16),
        scratch_shapes=[
            pltpu.VMEM((tq, A_HEAD_DIM), jnp.float32),
            pltpu.VMEM((tq, A_HEAD_DIM), jnp.float32),
            pltpu.VMEM((tq, 1), jnp.float32),
            pltpu.VMEM((tq, 1), jnp.float32),
            pltpu.VMEM((tq, 1), jnp.float32),
            pltpu.VMEM((tq, 1), jnp.float32),
        ],
        compiler_params=_params(("parallel", "parallel")),
        name="attn_a",
    )(proj, proj, proj, lam, subln.reshape(1, A_HEAD_DIM), slopes)


def _attn_b_kernel(q_ref, k_ref, v_ref, t_ref, o_ref, kpad, vpad, *, tq, pad):
    s_len = q_ref.shape[0]
    win = pad + tq
    kpad[0:pad, :] = jnp.zeros((pad, B_HEAD_DIM), kpad.dtype)
    vpad[0:pad, :] = jnp.zeros((pad, B_HEAD_DIM), vpad.dtype)
    kpad[pad:pad + s_len, :] = k_ref[...]
    vpad[pad:pad + s_len, :] = v_ref[...]
    table = t_ref[...]
    jcol = lax.broadcasted_iota(jnp.int32, (tq, win), 1)
    scale = B_HEAD_DIM ** -0.5

    def q_block(i, carry):
        qs = pl.multiple_of(i * tq, tq)
        q = q_ref[pl.ds(qs, tq), :]
        kw = kpad[pl.ds(qs, win), :]
        vw = vpad[pl.ds(qs, win), :]
        s = _dot_nt(q, kw) * scale + table
        s = jnp.where(jcol >= pad - qs, s, NEG)
        m = jnp.max(s, axis=1, keepdims=True)
        p = jnp.exp(s - m)
        l = jnp.sum(p, axis=1, keepdims=True)
        o = _dot(p.astype(jnp.bfloat16), vw) / l
        o_ref[pl.ds(qs, tq), :] = o.astype(o_ref.dtype)
        return carry

    lax.fori_loop(0, s_len // tq, q_block, 0)


def _band_bias_table(rel_bias, tq, pad):
    r = np.arange(tq)[:, None]
    j = np.arange(pad + tq)[None, :]
    idx = np.clip(pad + r - j, -REL_CLIP, REL_CLIP) + REL_CLIP
    kc, qc = j // CHUNK, B_PAST_CHUNKS + r // CHUNK
    in_band = (kc <= qc) & (kc >= qc - B_PAST_CHUNKS)
    return jnp.where(jnp.asarray(in_band)[None], rel_bias.astype(jnp.float32)[:, idx], NEG)


def _attn_b(proj, rel_bias, tq=128):
    b, s, _ = proj.shape
    pad = B_PAST_CHUNKS * CHUNK
    table = _band_bias_table(rel_bias, tq, pad)
    qi = OFF_B // B_HEAD_DIM
    ki = (OFF_B + B_WIDTH) // B_HEAD_DIM
    vi = (OFF_B + 2 * B_WIDTH) // B_HEAD_DIM

    def hspec(base):
        return pl.BlockSpec((None, s, B_HEAD_DIM), lambda bi, h, base=base: (bi, 0, base + h))

    return pl.pallas_call(
        functools.partial(_attn_b_kernel, tq=tq, pad=pad),
        grid=(b, B_HEADS),
        in_specs=[
            hspec(qi), hspec(ki), hspec(vi),
            pl.BlockSpec((None, tq, pad + tq), lambda bi, h: (h, 0, 0)),
        ],
        out_specs=pl.BlockSpec((None, s, B_HEAD_DIM), lambda bi, h: (bi, 0, h)),
        out_shape=jax.ShapeDtypeStruct((b, s, B_WIDTH), jnp.bfloat16),
        scratch_shapes=[
            pltpu.VMEM((pad + s, B_HEAD_DIM), jnp.bfloat16),
            pltpu.VMEM((pad + s, B_HEAD_DIM), jnp.bfloat16),
        ],
        compiler_params=_params(("parallel", "parallel")),
        name="attn_b",
    )(proj, proj, proj, table)


def _log_sigmoid(x):
    return jnp.minimum(x, 0.0) - jnp.log(1.0 + jnp.exp(-jnp.abs(x)))


def _sigmoid(x):
    return 1.0 / (1.0 + jnp.exp(-x))


def _cumsum_rows(x):
    n = x.shape[0]
    row = lax.broadcasted_iota(jnp.int32, x.shape, 0)
    sh = 1
    while sh < n:
        x = x + jnp.where(row >= sh, pltpu.roll(x, sh, 0), 0.0)
        sh *= 2
    return x


def _mlstm_kernel(q_ref, k_ref, v_ref, og_ref, gate_ref, cq_ref, ck_ref, ng_ref, o_ref,
                  qs_ref, ks_ref, state_ref, gt_ref, *, chunk, halo):
    s_len = q_ref.shape[0]
    n_chunks = s_len // chunk
    h = pl.program_id(1)
    lane = lax.broadcasted_iota(jnp.int32, (chunk, C_PAD), 1)
    glane = lax.broadcasted_iota(jnp.int32, (chunk, GATE_LANES), 1)
    trow = lax.broadcasted_iota(jnp.int32, (chunk, chunk), 0)
    tcol = lax.broadcasted_iota(jnp.int32, (chunk, chunk), 1)
    causal = tcol <= trow

    def conv_chunk(c, carry):
        st = pl.multiple_of(c * chunk, chunk)
        prev_st = pl.multiple_of(jnp.maximum(st - halo, 0), halo)
        keep = (c > 0).astype(jnp.float32)
        for src, w_ref, dst, post in ((q_ref, cq_ref, qs_ref, 1.0),
                                      (k_ref, ck_ref, ks_ref, C_HEAD_DIM ** -0.5)):
            cur = src[pl.ds(st, chunk), :].astype(jnp.float32)
            prev = src[pl.ds(prev_st, halo), :].astype(jnp.float32) * keep
            xc = jnp.concatenate([prev, cur], axis=0)
            w = w_ref[...]
            y = w[CONV_K - 1:CONV_K, :] * cur
            for back in range(1, CONV_K):
                y = y + w[CONV_K - 1 - back:CONV_K - back, :] * pltpu.roll(xc, back, 0)[halo:, :]
            y = y * _sigmoid(y) * post
            dst[pl.ds(st, chunk), :] = y.astype(dst.dtype)
        return carry

    lax.fori_loop(0, n_chunks, conv_chunk, 0)
    state_ref[...] = jnp.zeros(state_ref.shape, jnp.float32)

    def step(c, m):
        st = pl.multiple_of(c * chunk, chunk)
        q = qs_ref[pl.ds(st, chunk), :]
        k = ks_ref[pl.ds(st, chunk), :]
        v = v_ref[pl.ds(st, chunk), :]
        v_ext = jnp.where(lane == C_HEAD_DIM, jnp.ones((), v.dtype), v)

        g = gate_ref[pl.ds(st, chunk), :]
        bcum = _cumsum_rows(_log_sigmoid(g))
        gb = jnp.where(glane < C_HEADS, g, bcum)
        gt_ref[...] = gb.T
        i_row = gt_ref[pl.ds(h, 1), :]
        b_row = gt_ref[pl.ds(C_HEADS + h, 1), :]
        i_col = jnp.sum(jnp.where(glane == h, gb, 0.0), axis=1, keepdims=True)
        b_col = jnp.sum(jnp.where(glane == C_HEADS + h, gb, 0.0), axis=1, keepdims=True)

        d = jnp.where(causal, b_col + (i_row - b_row), NEG)
        inter = b_col + m
        m_t = jnp.maximum(inter, jnp.max(d, axis=1, keepdims=True))
        sw = _dot_nt(q, k) * jnp.exp(d - m_t)
        w_inter = jnp.exp(inter - m_t)
        nd = _dot(sw.astype(jnp.bfloat16), v_ext) + w_inter * _dot(q, state_ref[...].astype(jnp.bfloat16))
        den = jnp.sum(jnp.where(lane == C_HEAD_DIM, nd, 0.0), axis=1, keepdims=True)
        num = jnp.where(lane < C_HEAD_DIM, nd, 0.0)
        hh = num / jnp.maximum(jnp.abs(den), jnp.exp(-m_t))
        ms = jnp.sum(hh * hh, axis=1, keepdims=True) * (1.0 / C_HEAD_DIM)
        hn = hh * lax.rsqrt(ms + 1e-6) * ng_ref[...]
        og = og_ref[pl.ds(st, chunk), :].astype(jnp.float32)
        o_ref[pl.ds(st, chunk), :] = (_sigmoid(og) * hn).astype(o_ref.dtype)

        b_last = b_col[chunk - 1:chunk, :]
        g_col = b_last - b_col + i_col
        m_new = jnp.maximum(b_last + m, jnp.max(g_col, axis=0, keepdims=True))
        wk = jnp.exp(g_col - m_new)
        decay = jnp.exp(b_last + m - m_new)
        kw_t = (k.astype(jnp.float32) * wk).T.astype(jnp.bfloat16)
        state_ref[...] = decay * state_ref[...] + _dot(kw_t, v_ext)
        return m_new

    lax.fori_loop(0, n_chunks, step, jnp.zeros((1, 1), jnp.float32))


def _mlstm(proj, gates, conv_q, conv_k, norm_g, chunk=256, halo=16):
    b, s, _ = proj.shape
    chunk = min(chunk, s)

    def hspec(base):
        return pl.BlockSpec((None, s, C_PAD), lambda bi, h, base=base: (bi, 0, base + h))

    return pl.pallas_call(
        functools.partial(_mlstm_kernel, chunk=chunk, halo=halo),
        grid=(b, C_HEADS),
        in_specs=[
            hspec(0), hspec(C_HEADS), hspec(2 * C_HEADS), hspec(3 * C_HEADS),
            pl.BlockSpec((None, s, GATE_LANES), lambda bi, h: (bi, 0, 0)),
            pl.BlockSpec((CONV_K, C_PAD), lambda bi, h: (0, h)),
            pl.BlockSpec((CONV_K, C_PAD), lambda bi, h: (0, h)),
            pl.BlockSpec((1, C_PAD), lambda bi, h: (0, 0)),
        ],
        out_specs=pl.BlockSpec((None, s, C_PAD), lambda bi, h: (bi, 0, h)),
        out_shape=jax.ShapeDtypeStruct((b, s, CP_WIDTH), jnp.bfloat16),
        scratch_shapes=[
            pltpu.VMEM((s, C_PAD), jnp.bfloat16),
            pltpu.VMEM((s, C_PAD), jnp.bfloat16),
            pltpu.VMEM((C_PAD, C_PAD), jnp.float32),
            pltpu.VMEM((GATE_LANES, chunk), jnp.float32),
        ],
        compiler_params=_params(("parallel", "parallel")),
        name="mlstm",
    )(proj, proj, proj, proj, gates, conv_q, conv_k, norm_g)


def _layer_norm(z, g, b):
    mu = jnp.mean(z, axis=1, keepdims=True)
    zc = z - mu
    var = jnp.mean(zc * zc, axis=1, keepdims=True)
    return zc * lax.rsqrt(var + 1e-5) * g + b


def _out_proj_ln_kernel(ya_ref, yb_ref, yc_ref, wa_ref, wb_ref, wc_ref, x_ref, g_ref, b_ref,
                        o32_ref, o16_ref, *, alpha):
    acc = _dot(ya_ref[...], wa_ref[...]) + _dot(yb_ref[...], wb_ref[...]) + _dot(yc_ref[...], wc_ref[...])
    y = _layer_norm(alpha * x_ref[...] + acc, g_ref[...], b_ref[...])
    o32_ref[...] = y
    o16_ref[...] = y.astype(o16_ref.dtype)


def _out_proj_ln(ya, yb, yc, wa, wb, wc, x, g, b, alpha, tm=512):
    m, d = x.shape
    tm = min(tm, m)
    row = lambda w: pl.BlockSpec((tm, w), lambda i: (i, 0))
    full = lambda a: pl.BlockSpec(a.shape, lambda i: (0, 0))
    return pl.pallas_call(
        functools.partial(_out_proj_ln_kernel, alpha=alpha),
        grid=(m // tm,),
        in_specs=[row(ya.shape[1]), row(yb.shape[1]), row(yc.shape[1]),
                  full(wa), full(wb), full(wc), row(d), full(g), full(b)],
        out_specs=[row(d), row(d)],
        out_shape=[jax.ShapeDtypeStruct((m, d), jnp.float32),
                   jax.ShapeDtypeStruct((m, d), jnp.bfloat16)],
        compiler_params=_params(("parallel",)),
        name="out_proj_ln",
    )(ya, yb, yc, wa, wb, wc, x, g, b)


def _ffn_up_kernel(x_ref, wg_ref, wu_ref, o_ref):
    x = x_ref[...]
    a = _dot(x, wg_ref[...])
    u = _dot(x, wu_ref[...])
    o_ref[...] = (a * _sigmoid(a) * u).astype(o_ref.dtype)


def _ffn_up(xb, wg, wu, tm=1024, tn=512):
    m, k = xb.shape
    n = wg.shape[1]
    tm, tn = min(tm, m), min(tn, n)
    return pl.pallas_call(
        _ffn_up_kernel,
        grid=(m // tm, n // tn),
        in_specs=[
            pl.BlockSpec((tm, k), lambda i, j: (i, 0)),
            pl.BlockSpec((k, tn), lambda i, j: (0, j)),
            pl.BlockSpec((k, tn), lambda i, j: (0, j)),
        ],
        out_specs=pl.BlockSpec((tm, tn), lambda i, j: (i, j)),
        out_shape=jax.ShapeDtypeStruct((m, n), jnp.bfloat16),
        compiler_params=_params(("parallel", "arbitrary")),
        name="ffn_up",
    )(xb, wg, wu)


def _ffn_down_ln_kernel(h_ref, w_ref, x_ref, g_ref, b_ref, o32_ref, o16_ref, acc_ref, *, alpha):
    kk = pl.program_id(1)

    @pl.when(kk == 0)
    def _():
        acc_ref[...] = jnp.zeros(acc_ref.shape, jnp.float32)

    acc_ref[...] += _dot(h_ref[...], w_ref[...])

    @pl.when(kk == pl.num_programs(1) - 1)
    def _():
        y = _layer_norm(alpha * x_ref[...] + acc_ref[...], g_ref[...], b_ref[...])
        o32_ref[...] = y
        o16_ref[...] = y.astype(o16_ref.dtype)


def _ffn_down_ln(hid, wd, x, g, b, alpha, tm=512, tk=1408):
    m, f = hid.shape
    d = wd.shape[1]
    tm, tk = min(tm, m), min(tk, f)
    return pl.pallas_call(
        functools.partial(_ffn_down_ln_kernel, alpha=alpha),
        grid=(m // tm, f // tk),
        in_specs=[
            pl.BlockSpec((tm, tk), lambda i, kk: (i, kk)),
            pl.BlockSpec((tk, d), lambda i, kk: (kk, 0)),
            pl.BlockSpec((tm, d), lambda i, kk: (i, 0)),
            pl.BlockSpec((1, d), lambda i, kk: (0, 0)),
            pl.BlockSpec((1, d), lambda i, kk: (0, 0)),
        ],
        out_specs=[pl.BlockSpec((tm, d), lambda i, kk: (i, 0)),
                   pl.BlockSpec((tm, d), lambda i, kk: (i, 0))],
        out_shape=[jax.ShapeDtypeStruct((m, d), jnp.float32),
                   jax.ShapeDtypeStruct((m, d), jnp.bfloat16)],
        scratch_shapes=[pltpu.VMEM((tm, d), jnp.float32)],
        compiler_params=_params(("parallel", "arbitrary")),
        name="ffn_down_ln",
    )(hid, wd, x, g, b)


def _pad_heads(a, axis):
    shape = a.shape[:axis] + (C_HEADS, C_HEAD_DIM) + a.shape[axis + 1:]
    pad = [(0, 0)] * (len(shape))
    pad[axis + 1] = (0, C_PAD - C_HEAD_DIM)
    out = jnp.pad(a.reshape(shape), pad)
    return out.reshape(a.shape[:axis] + (CP_WIDTH,) + a.shape[axis + 1:])


def _layout_w_in(w):
    a0, b0 = 0, 3 * A_WIDTH
    c0 = b0 + 3 * B_WIDTH
    g0 = c0 + 4 * C_WIDTH
    c_parts = [_pad_heads(w[:, c0 + t * C_WIDTH:c0 + (t + 1) * C_WIDTH], 1) for t in range(4)]
    main = jnp.concatenate(c_parts + [w[:, a0:b0], w[:, b0:c0]], axis=1)
    main = jnp.pad(main, ((0, 0), (0, PROJ_WIDTH - PROJ_USED)))
    gate = jnp.pad(w[:, g0:g0 + 2 * C_HEADS], ((0, 0), (0, GATE_LANES - 2 * C_HEADS)))
    return main.astype(jnp.bfloat16), gate.astype(jnp.bfloat16)


def kernel(x, w_in, gate_bias, conv_w, lam, subln_a, norm_c, rel_bias, w_out, ln1_g, ln1_b,
           w_gate, w_up, w_down, ln2_g, ln2_b):
    bsz, s_len, d = x.shape
    depth = w_in.shape[0]
    alpha = (2.0 * depth) ** 0.25
    m = bsz * s_len
    x32 = x.reshape(m, d)
    x16 = x32.astype(jnp.bfloat16)
    for li in range(depth):
        w_main, w_g = _layout_w_in(w_in[li])
        gbias = jnp.pad(gate_bias[li], (0, GATE_LANES - 2 * C_HEADS)).reshape(1, GATE_LANES)
        proj, gates = _in_proj(x16, w_main, w_g, gbias)
        proj = proj.reshape(bsz, s_len, PROJ_WIDTH)
        gates = gates.reshape(bsz, s_len, GATE_LANES)

        ya = _attn_a(proj, lam[li], subln_a[li], li)
        yb = _attn_b(proj, rel_bias[li])
        conv_q = _pad_heads(conv_w[li][:, :C_WIDTH], 1)
        conv_k = _pad_heads(conv_w[li][:, C_WIDTH:], 1)
        norm_g = jnp.pad(norm_c[li], (0, C_PAD - C_HEAD_DIM)).reshape(1, C_PAD)
        yc = _mlstm(proj, gates, conv_q, conv_k, norm_g)

        wo = w_out[li]
        wa = wo[:A_WIDTH].astype(jnp.bfloat16)
        wb = wo[A_WIDTH:A_WIDTH + B_WIDTH].astype(jnp.bfloat16)
        wc = _pad_heads(wo[A_WIDTH + B_WIDTH:], 0).astype(jnp.bfloat16)
        x32, x16 = _out_proj_ln(ya.reshape(m, A_WIDTH), yb.reshape(m, B_WIDTH), yc.reshape(m, CP_WIDTH),
                                wa, wb, wc, x32, ln1_g[li].reshape(1, d), ln1_b[li].reshape(1, d), alpha)

        hid = _ffn_up(x16, w_gate[li].astype(jnp.bfloat16), w_up[li].astype(jnp.bfloat16))
        x32, x16 = _ffn_down_ln(hid, w_down[li].astype(jnp.bfloat16), x32,
                                ln2_g[li].reshape(1, d), ln2_b[li].reshape(1, d), alpha)
    return x32.reshape(bsz, s_len, d)
```

```python
import functools
import math

import numpy as np
import jax
import jax.numpy as jnp
from jax import lax
from jax.experimental import pallas as pl
from jax.experimental.pallas import tpu as pltpu

CHUNK = 64
A_HEADS = 4
A_HEAD_DIM = 128
A_HALF = 64
B_HEADS = 6
B_HEAD_DIM = 128
B_PAST_CHUNKS = 8
REL_CLIP = 256
C_HEADS = 4
C_HEAD_DIM = 192
C_PAD = 256
CONV_K = 4
NEG = -1e30
GATE_LANES = 128

A_WIDTH = A_HEADS * A_HEAD_DIM
B_WIDTH = B_HEADS * B_HEAD_DIM
C_WIDTH = C_HEADS * C_HEAD_DIM
CP_WIDTH = C_HEADS * C_PAD
OFF_C = 0
OFF_A = 4 * CP_WIDTH
OFF_B = OFF_A + 3 * A_WIDTH
PROJ_USED = OFF_B + 3 * B_WIDTH
PROJ_WIDTH = 8192

VMEM_LIMIT = 56 * 1024 * 1024

_NT = (((1,), (1,)), ((), ()))


def _params(sem):
    return pltpu.CompilerParams(dimension_semantics=sem, vmem_limit_bytes=VMEM_LIMIT)


def _dot(a, b):
    return jnp.dot(a, b, preferred_element_type=jnp.float32)


def _dot_nt(a, b):
    return lax.dot_general(a, b, _NT, preferred_element_type=jnp.float32)


def _in_proj_kernel(x_ref, w_ref, wg_ref, gb_ref, o_ref, g_ref):
    x = x_ref[...]
    o_ref[...] = _dot(x, w_ref[...]).astype(o_ref.dtype)

    @pl.when(pl.program_id(1) == 0)
    def _():
        g_ref[...] = _dot(x, wg_ref[...]) + gb_ref[...]


def _in_proj(xb, w_main, w_gate, gbias, tm=1024, tn=1024):
    m, k = xb.shape
    n = w_main.shape[1]
    tm, tn = min(tm, m), min(tn, n)
    return pl.pallas_call(
        _in_proj_kernel,
        grid=(m // tm, n // tn),
        in_specs=[
            pl.BlockSpec((tm, k), lambda i, j: (i, 0)),
            pl.BlockSpec((k, tn), lambda i, j: (0, j)),
            pl.BlockSpec((k, GATE_LANES), lambda i, j: (0, 0)),
            pl.BlockSpec((1, GATE_LANES), lambda i, j: (0, 0)),
        ],
        out_specs=[
            pl.BlockSpec((tm, tn), lambda i, j: (i, j)),
            pl.BlockSpec((tm, GATE_LANES), lambda i, j: (i, 0)),
        ],
        out_shape=[
            jax.ShapeDtypeStruct((m, n), jnp.bfloat16),
            jax.ShapeDtypeStruct((m, GATE_LANES), jnp.float32),
        ],
        compiler_params=_params(("parallel", "arbitrary")),
        name="in_proj",
    )(xb, w_main, w_gate, gbias)


def _attn_a_kernel(q_ref, k_ref, v_ref, lam_ref, g_ref, slope_ref, o_ref, qt1_ref, qt2_ref, vt_ref,
                   *, tq, lam_init):
    s_len = q_ref.shape[0]
    nq = s_len // tq
    slope = slope_ref[0:1, 0:1]
    lf = lam_ref[...]
    lam_full = (jnp.exp(jnp.sum(lf[0:1] * lf[1:2], axis=1, keepdims=True))
                - jnp.exp(jnp.sum(lf[2:3] * lf[3:4], axis=1, keepdims=True)) + lam_init)
    feat = lax.broadcasted_iota(jnp.int32, (A_HEAD_DIM, tq), 0)
    first_half = feat < A_HALF
    for i in range(nq):
        rows = slice(i * tq, (i + 1) * tq)
        qt = (q_ref[rows, :].astype(jnp.float32) * (A_HALF ** -0.5)).T
        qt1_ref[:, rows] = jnp.where(first_half, qt, 0.0).astype(jnp.bfloat16)
        qt2_ref[:, rows] = jnp.where(first_half, 0.0, qt).astype(jnp.bfloat16)
        vt_ref[:, rows] = v_ref[rows, :].astype(jnp.float32).T.astype(jnp.bfloat16)

    key = lax.broadcasted_iota(jnp.int32, (tq, tq), 0)
    qry = lax.broadcasted_iota(jnp.int32, (tq, tq), 1)
    rel = (qry - key).astype(jnp.float32)
    bias_off = -slope * rel
    bias_diag = -slope * jnp.abs(rel)
    diag_ok = (key // CHUNK) <= (qry // CHUNK)

    def update(t, shift_c, vt, m_prev, l_prev, acc_prev):
        m_new = jnp.maximum(m_prev, jnp.max(t, axis=0, keepdims=True) + shift_c)
        alpha = jnp.exp(m_prev - m_new)
        p = jnp.exp(t - (m_new - shift_c))
        l_new = alpha * l_prev + jnp.sum(p, axis=0, keepdims=True)
        acc_new = alpha * acc_prev + _dot(vt, p.astype(jnp.bfloat16))
        return m_new, l_new, acc_new

    for i in range(nq):
        cols = slice(i * tq, (i + 1) * tq)
        qt1 = qt1_ref[:, cols]
        qt2 = qt2_ref[:, cols]
        st1 = (jnp.full((1, tq), NEG, jnp.float32), jnp.zeros((1, tq), jnp.float32),
               jnp.zeros((A_HEAD_DIM, tq), jnp.float32))
        st2 = st1
        for j in range(i + 1):
            krows = slice(j * tq, (j + 1) * tq)
            k = k_ref[krows, :]
            vt = vt_ref[:, krows]
            if j < i:
                shift_c = -slope * float((i - j) * tq)
                t1 = _dot(k, qt1) + bias_off
                t2 = _dot(k, qt2) + bias_off
            else:
                shift_c = jnp.zeros((1, 1), jnp.float32)
                t1 = jnp.where(diag_ok, _dot(k, qt1) + bias_diag, NEG)
                t2 = jnp.where(diag_ok, _dot(k, qt2) + bias_diag, NEG)
            st1 = update(t1, shift_c, vt, *st1)
            st2 = update(t2, shift_c, vt, *st2)
        ot = st1[2] * (1.0 / st1[1]) - (lam_full / st2[1]) * st2[2]
        o = ot.T
        ms = jnp.mean(o * o, axis=1, keepdims=True)
        o = o * lax.rsqrt(ms + 1e-6) * g_ref[...] * (1.0 - lam_init)
        o_ref[cols, :] = o.astype(o_ref.dtype)


def _attn_a(proj, lam, subln, li, tq=256):
    b, s, _ = proj.shape
    tq = min(tq, s)
    lam_init = 0.8 - 0.6 * math.exp(-0.3 * li)
    slopes = np.asarray([2.0 ** (-8.0 * (h + 1) / A_HEADS) for h in range(A_HEADS)], np.float32)
    slopes = jnp.asarray(np.broadcast_to(slopes[:, None, None], (A_HEADS, 1, 128)))
    qi = OFF_A // A_HEAD_DIM
    ki = (OFF_A + A_WIDTH) // A_HEAD_DIM
    vi = (OFF_A + 2 * A_WIDTH) // A_HEAD_DIM

    def hspec(base):
        return pl.BlockSpec((None, s, A_HEAD_DIM), lambda bi, h, base=base: (bi, 0, base + h))

    return pl.pallas_call(
        functools.partial(_attn_a_kernel, tq=tq, lam_init=lam_init),
        grid=(b, A_HEADS),
        in_specs=[
            hspec(qi), hspec(ki), hspec(vi),
            pl.BlockSpec((4, A_HALF), lambda bi, h: (0, 0)),
            pl.BlockSpec((1, A_HEAD_DIM), lambda bi, h: (0, 0)),
            pl.BlockSpec((None, 1, 128), lambda bi, h: (h, 0, 0)),
        ],
        out_specs=pl.BlockSpec((None, s, A_HEAD_DIM), lambda bi, h: (bi, 0, h)),
        out_shape=jax.ShapeDtypeStruct((b, s, A_WIDTH), jnp.bfloat16),
        scratch_shapes=[
            pltpu.VMEM((A_HEAD_DIM, s), jnp.bfloat16),
            pltpu.VMEM((A_HEAD_DIM, s), jnp.bfloat16),
            pltpu.VMEM((A_HEAD_DIM, s), jnp.bfloat16),
        ],
        compiler_params=_params(("parallel", "parallel")),
        name="attn_a",
    )(proj, proj, proj, lam, subln.reshape(1, A_HEAD_DIM), slopes)


def _attn_b_kernel(q_ref, k_ref, v_ref, t_ref, o_ref, kpad, vpad, *, tq, pad):
    s_len = q_ref.shape[0]
    win = pad + tq
    kpad[0:pad, :] = jnp.zeros((pad, B_HEAD_DIM), kpad.dtype)
    vpad[0:pad, :] = jnp.zeros((pad, B_HEAD_DIM), vpad.dtype)
    kpad[pad:pad + s_len, :] = k_ref[...]
    vpad[pad:pad + s_len, :] = v_ref[...]
    table = t_ref[...]
    jcol = lax.broadcasted_iota(jnp.int32, (tq, win), 1)
    scale = B_HEAD_DIM ** -0.5

    def q_block(i, carry):
        qs = pl.multiple_of(i * tq, tq)
        q = q_ref[pl.ds(qs, tq), :]
        kw = kpad[pl.ds(qs, win), :]
        vw = vpad[pl.ds(qs, win), :]
        s = _dot_nt(q, kw) * scale + table
        s = jnp.where(jcol >= pad - qs, s, NEG)
        m = jnp.max(s, axis=1, keepdims=True)
        p = jnp.exp(s - m)
        l = jnp.sum(p, axis=1, keepdims=True)
        o = _dot(p.astype(jnp.bfloat16), vw) / l
        o_ref[pl.ds(qs, tq), :] = o.astype(o_ref.dtype)
        return carry

    lax.fori_loop(0, s_len // tq, q_block, 0, unroll=4)


def _band_bias_table(rel_bias, tq, pad):
    win = pad + tq
    n = win + tq - 1
    n_hi = pad + tq - REL_CLIP
    lo = REL_CLIP - tq + 1
    assert n_hi >= 0 and lo >= 0
    rb = rel_bias.astype(jnp.float32)
    heads = rb.shape[0]
    u = jnp.concatenate([jnp.repeat(rb[:, 2 * REL_CLIP:], n_hi, axis=1), rb[:, lo:2 * REL_CLIP][:, ::-1]], axis=1)
    width = n + 1
    upad = jnp.pad(u, ((0, 0), (0, width + 1 - n)))
    skew = jnp.tile(upad, (1, tq))[:, :tq * width].reshape(heads, tq, width)
    table = skew[:, :, tq - 1:tq - 1 + win]
    r = np.arange(tq)[:, None]
    j = np.arange(win)[None, :]
    kc, qc = j // CHUNK, B_PAST_CHUNKS + r // CHUNK
    in_band = (kc <= qc) & (kc >= qc - B_PAST_CHUNKS)
    return jnp.where(jnp.asarray(in_band)[None], table, NEG)


def _attn_b(proj, rel_bias, tq=128):
    b, s, _ = proj.shape
    pad = B_PAST_CHUNKS * CHUNK
    table = _band_bias_table(rel_bias, tq, pad)
    qi = OFF_B // B_HEAD_DIM
    ki = (OFF_B + B_WIDTH) // B_HEAD_DIM
    vi = (OFF_B + 2 * B_WIDTH) // B_HEAD_DIM

    def hspec(base):
        return pl.BlockSpec((None, s, B_HEAD_DIM), lambda bi, h, base=base: (bi, 0, base + h))

    return pl.pallas_call(
        functools.partial(_attn_b_kernel, tq=tq, pad=pad),
        grid=(b, B_HEADS),
        in_specs=[
            hspec(qi), hspec(ki), hspec(vi),
            pl.BlockSpec((None, tq, pad + tq), lambda bi, h: (h, 0, 0)),
        ],
        out_specs=pl.BlockSpec((None, s, B_HEAD_DIM), lambda bi, h: (bi, 0, h)),
        out_shape=jax.ShapeDtypeStruct((b, s, B_WIDTH), jnp.bfloat16),
        scratch_shapes=[
            pltpu.VMEM((pad + s, B_HEAD_DIM), jnp.bfloat16),
            pltpu.VMEM((pad + s, B_HEAD_DIM), jnp.bfloat16),
        ],
        compiler_params=_params(("parallel", "parallel")),
        name="attn_b",
    )(proj, proj, proj, table)


def _log_sigmoid(x):
    return jnp.minimum(x, 0.0) - jnp.log(1.0 + jnp.exp(-jnp.abs(x)))


def _sigmoid(x):
    return 1.0 / (1.0 + jnp.exp(-x))


def _cumsum_rows(x):
    n = x.shape[0]
    row = lax.broadcasted_iota(jnp.int32, x.shape, 0)
    sh = 1
    while sh < n:
        x = x + jnp.where(row >= sh, pltpu.roll(x, sh, 0), 0.0)
        sh *= 2
    return x


def _mlstm_kernel(q_ref, k_ref, v_ref, og_ref, gate_ref, cq_ref, ck_ref, ng_ref, o_ref,
                  qs_ref, ks_ref, state_ref, gt_ref, *, chunk, halo):
    s_len = q_ref.shape[0]
    n_chunks = s_len // chunk
    h = pl.program_id(1)
    lane = lax.broadcasted_iota(jnp.int32, (chunk, C_PAD), 1)
    glane = lax.broadcasted_iota(jnp.int32, (chunk, GATE_LANES), 1)
    trow = lax.broadcasted_iota(jnp.int32, (chunk, chunk), 0)
    tcol = lax.broadcasted_iota(jnp.int32, (chunk, chunk), 1)
    causal = tcol <= trow

    def conv_chunk(c, carry):
        st = pl.multiple_of(c * chunk, chunk)
        prev_st = pl.multiple_of(jnp.maximum(st - halo, 0), halo)
        keep = (c > 0).astype(jnp.float32)
        for src, w_ref, dst, post in ((q_ref, cq_ref, qs_ref, 1.0),
                                      (k_ref, ck_ref, ks_ref, C_HEAD_DIM ** -0.5)):
            cur = src[pl.ds(st, chunk), :].astype(jnp.float32)
            prev = src[pl.ds(prev_st, halo), :].astype(jnp.float32) * keep
            xc = jnp.concatenate([prev, cur], axis=0)
            w = w_ref[...]
            y = w[CONV_K - 1:CONV_K, :] * cur
            for back in range(1, CONV_K):
                y = y + w[CONV_K - 1 - back:CONV_K - back, :] * pltpu.roll(xc, back, 0)[halo:, :]
            y = y * _sigmoid(y) * post
            dst[pl.ds(st, chunk), :] = y.astype(dst.dtype)
        return carry

    lax.fori_loop(0, n_chunks, conv_chunk, 0)
    state_ref[...] = jnp.zeros(state_ref.shape, jnp.float32)

    def step(c, m):
        st = pl.multiple_of(c * chunk, chunk)
        q = qs_ref[pl.ds(st, chunk), :]
        k = ks_ref[pl.ds(st, chunk), :]
        v = v_ref[pl.ds(st, chunk), :]
        v_ext = jnp.where(lane == C_HEAD_DIM, jnp.ones((), v.dtype), v)

        g = gate_ref[pl.ds(st, chunk), :]
        bcum = _cumsum_rows(_log_sigmoid(g))
        gb = jnp.where(glane < C_HEADS, g, bcum)
        gt_ref[...] = gb.T
        i_row = gt_ref[pl.ds(h, 1), :]
        b_row = gt_ref[pl.ds(C_HEADS + h, 1), :]
        i_col = jnp.sum(jnp.where(glane == h, gb, 0.0), axis=1, keepdims=True)
        b_col = jnp.sum(jnp.where(glane == C_HEADS + h, gb, 0.0), axis=1, keepdims=True)

        d = jnp.where(causal, b_col + (i_row - b_row), NEG)
        inter = b_col + m
        m_t = jnp.maximum(inter, jnp.max(d, axis=1, keepdims=True))
        sw = _dot_nt(q, k) * jnp.exp(d - m_t)
        w_inter = jnp.exp(inter - m_t)
        nd = _dot(sw.astype(jnp.bfloat16), v_ext) + w_inter * _dot(q, state_ref[...].astype(jnp.bfloat16))
        den = jnp.sum(jnp.where(lane == C_HEAD_DIM, nd, 0.0), axis=1, keepdims=True)
        num = jnp.where(lane < C_HEAD_DIM, nd, 0.0)
        hh = num / jnp.maximum(jnp.abs(den), jnp.exp(-m_t))
        ms = jnp.sum(hh * hh, axis=1, keepdims=True) * (1.0 / C_HEAD_DIM)
        hn = hh * lax.rsqrt(ms + 1e-6) * ng_ref[...]
        og = og_ref[pl.ds(st, chunk), :].astype(jnp.float32)
        o_ref[pl.ds(st, chunk), :] = (_sigmoid(og) * hn).astype(o_ref.dtype)

        b_last = b_col[chunk - 1:chunk, :]
        g_col = b_last - b_col + i_col
        m_new = jnp.maximum(b_last + m, jnp.max(g_col, axis=0, keepdims=True))
        wk = jnp.exp(g_col - m_new)
        decay = jnp.exp(b_last + m - m_new)
        kw_t = (k.astype(jnp.float32) * wk).T.astype(jnp.bfloat16)
        state_ref[...] = decay * state_ref[...] + _dot(kw_t, v_ext)
        return m_new

    lax.fori_loop(0, n_chunks, step, jnp.zeros((1, 1), jnp.float32))


def _mlstm(proj, gates, conv_q, conv_k, norm_g, chunk=256, halo=16):
    b, s, _ = proj.shape
    chunk = min(chunk, s)

    def hspec(base):
        return pl.BlockSpec((None, s, C_PAD), lambda bi, h, base=base: (bi, 0, base + h))

    return pl.pallas_call(
        functools.partial(_mlstm_kernel, chunk=chunk, halo=halo),
        grid=(b, C_HEADS),
        in_specs=[
            hspec(0), hspec(C_HEADS), hspec(2 * C_HEADS), hspec(3 * C_HEADS),
            pl.BlockSpec((None, s, GATE_LANES), lambda bi, h: (bi, 0, 0)),
            pl.BlockSpec((CONV_K, C_PAD), lambda bi, h: (0, h)),
            pl.BlockSpec((CONV_K, C_PAD), lambda bi, h: (0, h)),
            pl.BlockSpec((1, C_PAD), lambda bi, h: (0, 0)),
        ],
        out_specs=pl.BlockSpec((None, s, C_PAD), lambda bi, h: (bi, 0, h)),
        out_shape=jax.ShapeDtypeStruct((b, s, CP_WIDTH), jnp.bfloat16),
        scratch_shapes=[
            pltpu.VMEM((s, C_PAD), jnp.bfloat16),
            pltpu.VMEM((s, C_PAD), jnp.bfloat16),
            pltpu.VMEM((C_PAD, C_PAD), jnp.float32),
            pltpu.VMEM((GATE_LANES, chunk), jnp.float32),
        ],
        compiler_params=_params(("parallel", "parallel")),
        name="mlstm",
    )(proj, proj, proj, proj, gates, conv_q, conv_k, norm_g)


def _layer_norm(z, g, b):
    mu = jnp.mean(z, axis=1, keepdims=True)
    zc = z - mu
    var = jnp.mean(zc * zc, axis=1, keepdims=True)
    return zc * lax.rsqrt(var + 1e-5) * g + b


def _out_proj_ln_kernel(ya_ref, yb_ref, yc_ref, wa_ref, wb_ref, wc_ref, x_ref, g_ref, b_ref,
                        o32_ref, o16_ref, *, alpha):
    acc = _dot(ya_ref[...], wa_ref[...]) + _dot(yb_ref[...], wb_ref[...]) + _dot(yc_ref[...], wc_ref[...])
    y = _layer_norm(alpha * x_ref[...] + acc, g_ref[...], b_ref[...])
    o32_ref[...] = y
    o16_ref[...] = y.astype(o16_ref.dtype)


def _out_proj_ln(ya, yb, yc, wa, wb, wc, x, g, b, alpha, tm=512):
    m, d = x.shape
    tm = min(tm, m)
    row = lambda w: pl.BlockSpec((tm, w), lambda i: (i, 0))
    full = lambda a: pl.BlockSpec(a.shape, lambda i: (0, 0))
    return pl.pallas_call(
        functools.partial(_out_proj_ln_kernel, alpha=alpha),
        grid=(m // tm,),
        in_specs=[row(ya.shape[1]), row(yb.shape[1]), row(yc.shape[1]),
                  full(wa), full(wb), full(wc), row(d), full(g), full(b)],
        out_specs=[row(d), row(d)],
        out_shape=[jax.ShapeDtypeStruct((m, d), jnp.float32),
                   jax.ShapeDtypeStruct((m, d), jnp.bfloat16)],
        compiler_params=_params(("parallel",)),
        name="out_proj_ln",
    )(ya, yb, yc, wa, wb, wc, x, g, b)


def _ffn_up_kernel(x_ref, wg_ref, wu_ref, o_ref):
    x = x_ref[...]
    a = _dot(x, wg_ref[...])
    u = _dot(x, wu_ref[...])
    o_ref[...] = (a * _sigmoid(a) * u).astype(o_ref.dtype)


def _ffn_up(xb, wg, wu, tm=1024, tn=512):
    m, k = xb.shape
    n = wg.shape[1]
    tm, tn = min(tm, m), min(tn, n)
    return pl.pallas_call(
        _ffn_up_kernel,
        grid=(m // tm, n // tn),
        in_specs=[
            pl.BlockSpec((tm, k), lambda i, j: (i, 0)),
            pl.BlockSpec((k, tn), lambda i, j: (0, j)),
            pl.BlockSpec((k, tn), lambda i, j: (0, j)),
        ],
        out_specs=pl.BlockSpec((tm, tn), lambda i, j: (i, j)),
        out_shape=jax.ShapeDtypeStruct((m, n), jnp.bfloat16),
        compiler_params=_params(("parallel", "arbitrary")),
        name="ffn_up",
    )(xb, wg, wu)


def _ffn_down_ln_kernel(h_ref, w_ref, x_ref, g_ref, b_ref, o32_ref, o16_ref, acc_ref, *, alpha):
    kk = pl.program_id(1)

    @pl.when(kk == 0)
    def _():
        acc_ref[...] = jnp.zeros(acc_ref.shape, jnp.float32)

    acc_ref[...] += _dot(h_ref[...], w_ref[...])

    @pl.when(kk == pl.num_programs(1) - 1)
    def _():
        y = _layer_norm(alpha * x_ref[...] + acc_ref[...], g_ref[...], b_ref[...])
        o32_ref[...] = y
        o16_ref[...] = y.astype(o16_ref.dtype)


def _ffn_down_ln(hid, wd, x, g, b, alpha, tm=512, tk=1408):
    m, f = hid.shape
    d = wd.shape[1]
    tm, tk = min(tm, m), min(tk, f)
    return pl.pallas_call(
        functools.partial(_ffn_down_ln_kernel, alpha=alpha),
        grid=(m // tm, f // tk),
        in_specs=[
            pl.BlockSpec((tm, tk), lambda i, kk: (i, kk)),
            pl.BlockSpec((tk, d), lambda i, kk: (kk, 0)),
            pl.BlockSpec((tm, d), lambda i, kk: (i, 0)),
            pl.BlockSpec((1, d), lambda i, kk: (0, 0)),
            pl.BlockSpec((1, d), lambda i, kk: (0, 0)),
        ],
        out_specs=[pl.BlockSpec((tm, d), lambda i, kk: (i, 0)),
                   pl.BlockSpec((tm, d), lambda i, kk: (i, 0))],
        out_shape=[jax.ShapeDtypeStruct((m, d), jnp.float32),
                   jax.ShapeDtypeStruct((m, d), jnp.bfloat16)],
        scratch_shapes=[pltpu.VMEM((tm, d), jnp.float32)],
        compiler_params=_params(("parallel", "arbitrary")),
        name="ffn_down_ln",
    )(hid, wd, x, g, b)


def _pad_heads(a, axis):
    shape = a.shape[:axis] + (C_HEADS, C_HEAD_DIM) + a.shape[axis + 1:]
    pad = [(0, 0)] * (len(shape))
    pad[axis + 1] = (0, C_PAD - C_HEAD_DIM)
    out = jnp.pad(a.reshape(shape), pad)
    return out.reshape(a.shape[:axis] + (CP_WIDTH,) + a.shape[axis + 1:])


def _layout_w_in(w):
    a0, b0 = 0, 3 * A_WIDTH
    c0 = b0 + 3 * B_WIDTH
    g0 = c0 + 4 * C_WIDTH
    c_parts = [_pad_heads(w[:, c0 + t * C_WIDTH:c0 + (t + 1) * C_WIDTH], 1) for t in range(4)]
    main = jnp.concatenate(c_parts + [w[:, a0:b0], w[:, b0:c0]], axis=1)
    main = jnp.pad(main, ((0, 0), (0, PROJ_WIDTH - PROJ_USED)))
    gate = jnp.pad(w[:, g0:g0 + 2 * C_HEADS], ((0, 0), (0, GATE_LANES - 2 * C_HEADS)))
    return main.astype(jnp.bfloat16), gate.astype(jnp.bfloat16)


def kernel(x, w_in, gate_bias, conv_w, lam, subln_a, norm_c, rel_bias, w_out, ln1_g, ln1_b,
           w_gate, w_up, w_down, ln2_g, ln2_b):
    bsz, s_len, d = x.shape
    depth = w_in.shape[0]
    alpha = (2.0 * depth) ** 0.25
    m = bsz * s_len
    x32 = x.reshape(m, d)
    x16 = x32.astype(jnp.bfloat16)
    for li in range(depth):
        w_main, w_g = _layout_w_in(w_in[li])
        gbias = jnp.pad(gate_bias[li], (0, GATE_LANES - 2 * C_HEADS)).reshape(1, GATE_LANES)
        proj, gates = _in_proj(x16, w_main, w_g, gbias)
        proj = proj.reshape(bsz, s_len, PROJ_WIDTH)
        gates = gates.reshape(bsz, s_len, GATE_LANES)

        ya = _attn_a(proj, lam[li], subln_a[li], li)
        yb = _attn_b(proj, rel_bias[li])
        conv_q = _pad_heads(conv_w[li][:, :C_WIDTH], 1)
        conv_k = _pad_heads(conv_w[li][:, C_WIDTH:], 1)
        norm_g = jnp.pad(norm_c[li], (0, C_PAD - C_HEAD_DIM)).reshape(1, C_PAD)
        yc = _mlstm(proj, gates, conv_q, conv_k, norm_g)

        wo = w_out[li]
        wa = wo[:A_WIDTH].astype(jnp.bfloat16)
        wb = wo[A_WIDTH:A_WIDTH + B_WIDTH].astype(jnp.bfloat16)
        wc = _pad_heads(wo[A_WIDTH + B_WIDTH:], 0).astype(jnp.bfloat16)
        x32, x16 = _out_proj_ln(ya.reshape(m, A_WIDTH), yb.reshape(m, B_WIDTH), yc.reshape(m, CP_WIDTH),
                                wa, wb, wc, x32, ln1_g[li].reshape(1, d), ln1_b[li].reshape(1, d), alpha)

        hid = _ffn_up(x16, w_gate[li].astype(jnp.bfloat16), w_up[li].astype(jnp.bfloat16))
        x32, x16 = _ffn_down_ln(hid, w_down[li].astype(jnp.bfloat16), x32,
                                ln2_g[li].reshape(1, d), ln2_b[li].reshape(1, d), alpha)
    return x32.reshape(bsz, s_len, d)
```

```python
import functools
import math

import numpy as np
import jax
import jax.numpy as jnp
from jax import lax
from jax.experimental import pallas as pl
from jax.experimental.pallas import tpu as pltpu

CHUNK = 64
A_HEADS = 4
A_HEAD_DIM = 128
A_HALF = 64
B_HEADS = 6
B_HEAD_DIM = 128
B_PAST_CHUNKS = 8
REL_CLIP = 256
C_HEADS = 4
C_HEAD_DIM = 192
C_PAIR = 2 * C_HEAD_DIM
C_WIN = 256
CONV_K = 4
NEG = -1e30
GATE_LANES = 128

A_WIDTH = A_HEADS * A_HEAD_DIM
B_WIDTH = B_HEADS * B_HEAD_DIM
C_WIDTH = C_HEADS * C_HEAD_DIM
OFF_A = 0
OFF_B = 3 * A_WIDTH
OFF_C = OFF_B + 3 * B_WIDTH
OFF_G = OFF_C + 4 * C_WIDTH
PROJ_TILE = 768

VMEM_LIMIT = 56 * 1024 * 1024

_NT = (((1,), (1,)), ((), ()))


def _params(sem):
    return pltpu.CompilerParams(dimension_semantics=sem, vmem_limit_bytes=VMEM_LIMIT)


def _dot(a, b):
    return jnp.dot(a, b, preferred_element_type=jnp.float32)


def _dot_nt(a, b):
    return lax.dot_general(a, b, _NT, preferred_element_type=jnp.float32)


def _log_sigmoid(x):
    return jnp.minimum(x, 0.0) - jnp.log(1.0 + jnp.exp(-jnp.abs(x)))


def _sigmoid(x):
    return 1.0 / (1.0 + jnp.exp(-x))


def _layer_norm(z, g, b):
    mu = jnp.mean(z, axis=1, keepdims=True)
    zc = z - mu
    var = jnp.mean(zc * zc, axis=1, keepdims=True)
    return zc * lax.rsqrt(var + 1e-5) * g + b


def _cast_gate_kernel(x_ref, wg_ref, gb_ref, x16_ref, g_ref):
    x16 = x_ref[...].astype(jnp.bfloat16)
    x16_ref[...] = x16
    g_ref[...] = _dot(x16, wg_ref[...]) + gb_ref[...]


def _cast_gate(x, wg, gb, tm=1024):
    m, d = x.shape
    tm = min(tm, m)
    return pl.pallas_call(
        _cast_gate_kernel,
        grid=(m // tm,),
        in_specs=[
            pl.BlockSpec((tm, d), lambda i: (i, 0)),
            pl.BlockSpec((d, GATE_LANES), lambda i: (0, 0)),
            pl.BlockSpec((1, GATE_LANES), lambda i: (0, 0)),
        ],
        out_specs=[pl.BlockSpec((tm, d), lambda i: (i, 0)),
                   pl.BlockSpec((tm, GATE_LANES), lambda i: (i, 0))],
        out_shape=[jax.ShapeDtypeStruct((m, d), jnp.bfloat16),
                   jax.ShapeDtypeStruct((m, GATE_LANES), jnp.float32)],
        compiler_params=_params(("parallel",)),
        name="cast_gate",
    )(x, wg, gb)


def _in_proj_kernel(x_ref, w_ref, o_ref, w16_ref):
    @pl.when(pl.program_id(1) == 0)
    def _():
        w16_ref[...] = w_ref[...].astype(jnp.bfloat16)

    o_ref[...] = _dot(x_ref[...], w16_ref[...]).astype(o_ref.dtype)


def _in_proj(xb, w_in, li, tm=1024, tn=PROJ_TILE):
    m, k = xb.shape
    tm = min(tm, m)
    n = OFF_G
    return pl.pallas_call(
        _in_proj_kernel,
        grid=(n // tn, m // tm),
        in_specs=[
            pl.BlockSpec((tm, k), lambda j, i: (i, 0)),
            pl.BlockSpec((None, k, tn), lambda j, i: (li, 0, j)),
        ],
        out_specs=pl.BlockSpec((tm, tn), lambda j, i: (i, j)),
        out_shape=jax.ShapeDtypeStruct((m, n), jnp.bfloat16),
        scratch_shapes=[pltpu.VMEM((k, tn), jnp.bfloat16)],
        compiler_params=_params(("arbitrary", "arbitrary")),
        name="in_proj",
    )(xb, w_in)


def _attn_a_kernel(q_ref, k_ref, v_ref, lam_ref, g_ref, slope_ref, o_ref, qt1_ref, qt2_ref, vt_ref,
                   *, tq, lam_init):
    s_len = q_ref.shape[0]
    nq = s_len // tq
    slope = slope_ref[0:1, 0:1]
    lf = lam_ref[...]
    lam_full = (jnp.exp(jnp.sum(lf[0:1] * lf[1:2], axis=1, keepdims=True))
                - jnp.exp(jnp.sum(lf[2:3] * lf[3:4], axis=1, keepdims=True)) + lam_init)
    feat = lax.broadcasted_iota(jnp.int32, (A_HEAD_DIM, tq), 0)
    first_half = feat < A_HALF
    for i in range(nq):
        rows = slice(i * tq, (i + 1) * tq)
        qt = (q_ref[rows, :].astype(jnp.float32) * (A_HALF ** -0.5)).T
        qt1_ref[:, rows] = jnp.where(first_half, qt, 0.0).astype(jnp.bfloat16)
        qt2_ref[:, rows] = jnp.where(first_half, 0.0, qt).astype(jnp.bfloat16)
        vt_ref[:, rows] = v_ref[rows, :].astype(jnp.float32).T.astype(jnp.bfloat16)

    key = lax.broadcasted_iota(jnp.int32, (tq, tq), 0)
    qry = lax.broadcasted_iota(jnp.int32, (tq, tq), 1)
    rel = (qry - key).astype(jnp.float32)
    bias_off = -slope * rel
    bias_diag = -slope * jnp.abs(rel)
    diag_ok = (key // CHUNK) <= (qry // CHUNK)

    def update(t, shift_c, vt, m_prev, l_prev, acc_prev):
        m_new = jnp.maximum(m_prev, jnp.max(t, axis=0, keepdims=True) + shift_c)
        alpha = jnp.exp(m_prev - m_new)
        p = jnp.exp(t - (m_new - shift_c))
        l_new = alpha * l_prev + jnp.sum(p, axis=0, keepdims=True)
        acc_new = alpha * acc_prev + _dot(vt, p.astype(jnp.bfloat16))
        return m_new, l_new, acc_new

    for i in range(nq):
        cols = slice(i * tq, (i + 1) * tq)
        qt1 = qt1_ref[:, cols]
        qt2 = qt2_ref[:, cols]
        st1 = (jnp.full((1, tq), NEG, jnp.float32), jnp.zeros((1, tq), jnp.float32),
               jnp.zeros((A_HEAD_DIM, tq), jnp.float32))
        st2 = st1
        for j in range(i + 1):
            krows = slice(j * tq, (j + 1) * tq)
            k = k_ref[krows, :]
            vt = vt_ref[:, krows]
            if j < i:
                shift_c = -slope * float((i - j) * tq)
                t1 = _dot(k, qt1) + bias_off
                t2 = _dot(k, qt2) + bias_off
            else:
                shift_c = jnp.zeros((1, 1), jnp.float32)
                t1 = jnp.where(diag_ok, _dot(k, qt1) + bias_diag, NEG)
                t2 = jnp.where(diag_ok, _dot(k, qt2) + bias_diag, NEG)
            st1 = update(t1, shift_c, vt, *st1)
            st2 = update(t2, shift_c, vt, *st2)
        ot = st1[2] * (1.0 / st1[1]) - (lam_full / st2[1]) * st2[2]
        o = ot.T
        ms = jnp.mean(o * o, axis=1, keepdims=True)
        o = o * lax.rsqrt(ms + 1e-6) * g_ref[...] * (1.0 - lam_init)
        o_ref[cols, :] = o.astype(o_ref.dtype)


def _attn_a(proj, lam, subln, li, tq=256):
    b, s, _ = proj.shape
    tq = min(tq, s)
    lam_init = 0.8 - 0.6 * math.exp(-0.3 * li)
    slopes = np.asarray([2.0 ** (-8.0 * (h + 1) / A_HEADS) for h in range(A_HEADS)], np.float32)
    slopes = jnp.asarray(np.broadcast_to(slopes[:, None, None], (A_HEADS, 1, 128)))
    qi = OFF_A // A_HEAD_DIM
    ki = (OFF_A + A_WIDTH) // A_HEAD_DIM
    vi = (OFF_A + 2 * A_WIDTH) // A_HEAD_DIM

    def hspec(base):
        return pl.BlockSpec((None, s, A_HEAD_DIM), lambda bi, h, base=base: (bi, 0, base + h))

    return pl.pallas_call(
        functools.partial(_attn_a_kernel, tq=tq, lam_init=lam_init),
        grid=(b, A_HEADS),
        in_specs=[
            hspec(qi), hspec(ki), hspec(vi),
            pl.BlockSpec((None, 4, A_HALF), lambda bi, h: (li, 0, 0)),
            pl.BlockSpec((None, 1, A_HEAD_DIM), lambda bi, h: (li, 0, 0)),
            pl.BlockSpec((None, 1, 128), lambda bi, h: (h, 0, 0)),
        ],
        out_specs=pl.BlockSpec((None, s, A_HEAD_DIM), lambda bi, h: (bi, 0, h)),
        out_shape=jax.ShapeDtypeStruct((b, s, A_WIDTH), jnp.bfloat16),
        scratch_shapes=[
            pltpu.VMEM((A_HEAD_DIM, s), jnp.bfloat16),
            pltpu.VMEM((A_HEAD_DIM, s), jnp.bfloat16),
            pltpu.VMEM((A_HEAD_DIM, s), jnp.bfloat16),
        ],
        compiler_params=_params(("parallel", "parallel")),
        name="attn_a",
    )(proj, proj, proj, lam, subln.reshape(-1, 1, A_HEAD_DIM), slopes)


def _attn_b_kernel(q_ref, k_ref, v_ref, t_ref, o_ref, kpad, vpad, *, tq, pad):
    s_len = q_ref.shape[0]
    win = pad + tq
    kpad[0:pad, :] = jnp.zeros((pad, B_HEAD_DIM), kpad.dtype)
    vpad[0:pad, :] = jnp.zeros((pad, B_HEAD_DIM), vpad.dtype)
    kpad[pad:pad + s_len, :] = k_ref[...]
    vpad[pad:pad + s_len, :] = v_ref[...]
    table = t_ref[...]
    jcol = lax.broadcasted_iota(jnp.int32, (tq, win), 1)
    scale = B_HEAD_DIM ** -0.5

    def q_block(i, carry):
        qs = pl.multiple_of(i * tq, tq)
        q = q_ref[pl.ds(qs, tq), :]
        kw = kpad[pl.ds(qs, win), :]
        vw = vpad[pl.ds(qs, win), :]
        s = _dot_nt(q, kw) * scale + table
        s = jnp.where(jcol >= pad - qs, s, NEG)
        m = jnp.max(s, axis=1, keepdims=True)
        p = jnp.exp(s - m)
        l = jnp.sum(p, axis=1, keepdims=True)
        o = _dot(p.astype(jnp.bfloat16), vw) / l
        o_ref[pl.ds(qs, tq), :] = o.astype(o_ref.dtype)
        return carry

    lax.fori_loop(0, s_len // tq, q_block, 0, unroll=4)


def _band_bias_table(rel_bias, tq, pad):
    win = pad + tq
    n = win + tq - 1
    n_hi = pad + tq - REL_CLIP
    lo = REL_CLIP - tq + 1
    assert n_hi >= 0 and lo >= 0
    rb = rel_bias.astype(jnp.float32)
    heads = rb.shape[0]
    u = jnp.concatenate([jnp.repeat(rb[:, 2 * REL_CLIP:], n_hi, axis=1), rb[:, lo:2 * REL_CLIP][:, ::-1]], axis=1)
    width = n + 1
    upad = jnp.pad(u, ((0, 0), (0, width + 1 - n)))
    skew = jnp.tile(upad, (1, tq))[:, :tq * width].reshape(heads, tq, width)
    table = skew[:, :, tq - 1:tq - 1 + win]
    r = np.arange(tq)[:, None]
    j = np.arange(win)[None, :]
    kc, qc = j // CHUNK, B_PAST_CHUNKS + r // CHUNK
    in_band = (kc <= qc) & (kc >= qc - B_PAST_CHUNKS)
    return jnp.where(jnp.asarray(in_band)[None], table, NEG)


def _attn_b(proj, rel_bias, tq=128):
    b, s, _ = proj.shape
    pad = B_PAST_CHUNKS * CHUNK
    table = _band_bias_table(rel_bias, tq, pad)
    qi = OFF_B // B_HEAD_DIM
    ki = (OFF_B + B_WIDTH) // B_HEAD_DIM
    vi = (OFF_B + 2 * B_WIDTH) // B_HEAD_DIM

    def hspec(base):
        return pl.BlockSpec((None, s, B_HEAD_DIM), lambda bi, h, base=base: (bi, 0, base + h))

    return pl.pallas_call(
        functools.partial(_attn_b_kernel, tq=tq, pad=pad),
        grid=(b, B_HEADS),
        in_specs=[
            hspec(qi), hspec(ki), hspec(vi),
            pl.BlockSpec((None, tq, pad + tq), lambda bi, h: (h, 0, 0)),
        ],
        out_specs=pl.BlockSpec((None, s, B_HEAD_DIM), lambda bi, h: (bi, 0, h)),
        out_shape=jax.ShapeDtypeStruct((b, s, B_WIDTH), jnp.bfloat16),
        scratch_shapes=[
            pltpu.VMEM((pad + s, B_HEAD_DIM), jnp.bfloat16),
            pltpu.VMEM((pad + s, B_HEAD_DIM), jnp.bfloat16),
        ],
        compiler_params=_params(("parallel", "parallel")),
        name="attn_b",
    )(proj, proj, proj, table)


def _cumsum_rows(x):
    n = x.shape[0]
    row = lax.broadcasted_iota(jnp.int32, x.shape, 0)
    sh = 1
    while sh < n:
        x = x + jnp.where(row >= sh, pltpu.roll(x, sh, 0), 0.0)
        sh *= 2
    return x


def _mlstm_kernel(q_ref, k_ref, v_ref, og_ref, gate_ref, cq_ref, ck_ref, ng_ref, o_ref,
                  qs_ref, ks_ref, state_ref, gt_ref, *, chunk, halo):
    s_len = q_ref.shape[0]
    n_chunks = s_len // chunk
    pair = pl.program_id(1)
    lane = lax.broadcasted_iota(jnp.int32, (chunk, C_WIN), 1)
    glane = lax.broadcasted_iota(jnp.int32, (chunk, GATE_LANES), 1)
    trow = lax.broadcasted_iota(jnp.int32, (chunk, chunk), 0)
    tcol = lax.broadcasted_iota(jnp.int32, (chunk, chunk), 1)
    causal = tcol <= trow
    windows = ((0, lane < C_HEAD_DIM, C_HEAD_DIM),
               (C_PAIR - C_WIN, lane >= C_WIN - C_HEAD_DIM, 0))

    def conv_chunk(c, carry):
        st = pl.multiple_of(c * chunk, chunk)
        prev_st = pl.multiple_of(jnp.maximum(st - halo, 0), halo)
        keep = jnp.where(c > 0, 1.0, 0.0)
        for src, w_ref, dst, post in ((q_ref, cq_ref, qs_ref, 1.0),
                                      (k_ref, ck_ref, ks_ref, C_HEAD_DIM ** -0.5)):
            cur = src[pl.ds(st, chunk), :].astype(jnp.float32)
            prev = src[pl.ds(prev_st, halo), :].astype(jnp.float32) * keep
            xc = jnp.concatenate([prev, cur], axis=0)
            w = w_ref[...]
            y = w[CONV_K - 1:CONV_K, :] * cur
            for back in range(1, CONV_K):
                y = y + w[CONV_K - 1 - back:CONV_K - back, :] * pltpu.roll(xc, back, 0)[halo:, :]
            y = y * _sigmoid(y) * post
            dst[pl.ds(st, chunk), :] = y.astype(dst.dtype)
        return carry

    lax.fori_loop(0, n_chunks, conv_chunk, 0)
    state_ref[...] = jnp.zeros(state_ref.shape, jnp.float32)

    def step(c, ms):
        st = pl.multiple_of(c * chunk, chunk)
        g = gate_ref[pl.ds(st, chunk), :]
        bcum = _cumsum_rows(_log_sigmoid(g))
        gb = jnp.where(glane < C_HEADS, g, bcum)
        gt_ref[...] = gb.T
        ng = ng_ref[...]
        hn, m_out = [], []
        for a, (w0, valid, spare) in enumerate(windows):
            h = 2 * pair + a
            m = ms[a]
            wcols = slice(w0, w0 + C_WIN)
            zero = jnp.zeros((), jnp.bfloat16)
            q = jnp.where(valid, qs_ref[pl.ds(st, chunk), wcols], zero)
            k = jnp.where(valid, ks_ref[pl.ds(st, chunk), wcols], zero)
            v = jnp.where(valid, v_ref[pl.ds(st, chunk), wcols], zero)
            v_ext = jnp.where(lane == spare, jnp.ones((), v.dtype), v)

            i_row = gt_ref[pl.ds(h, 1), :]
            b_row = gt_ref[pl.ds(C_HEADS + h, 1), :]
            i_col = jnp.sum(jnp.where(glane == h, gb, 0.0), axis=1, keepdims=True)
            b_col = jnp.sum(jnp.where(glane == C_HEADS + h, gb, 0.0), axis=1, keepdims=True)

            d = jnp.where(causal, b_col + (i_row - b_row), NEG)
            inter = b_col + m
            m_t = jnp.maximum(inter, jnp.max(d, axis=1, keepdims=True))
            sw = _dot_nt(q, k) * jnp.exp(d - m_t)
            w_inter = jnp.exp(inter - m_t)
            nd = (_dot(sw.astype(jnp.bfloat16), v_ext)
                  + w_inter * _dot(q, state_ref[a].astype(jnp.bfloat16)))
            den = jnp.sum(jnp.where(lane == spare, nd, 0.0), axis=1, keepdims=True)
            num = jnp.where(valid, nd, 0.0)
            hh = num / jnp.maximum(jnp.abs(den), jnp.exp(-m_t))
            msq = jnp.sum(hh * hh, axis=1, keepdims=True) * (1.0 / C_HEAD_DIM)
            hn.append(hh * lax.rsqrt(msq + 1e-6) * ng[:, wcols])

            b_last = b_col[chunk - 1:chunk, :]
            g_col = b_last - b_col + i_col
            m_new = jnp.maximum(b_last + m, jnp.max(g_col, axis=0, keepdims=True))
            wk = jnp.exp(g_col - m_new)
            decay = jnp.exp(b_last + m - m_new)
            kw_t = (k.astype(jnp.float32) * wk).T.astype(jnp.bfloat16)
            state_ref[a] = decay * state_ref[a] + _dot(kw_t, v_ext)
            m_out.append(m_new)

        mid = C_PAIR - C_WIN
        hn_pair = jnp.concatenate([hn[0][:, :mid], hn[0][:, mid:] + hn[1][:, :C_WIN - mid], hn[1][:, C_WIN - mid:]],
                                  axis=1)
        og = og_ref[pl.ds(st, chunk), :].astype(jnp.float32)
        o_ref[pl.ds(st, chunk), :] = (_sigmoid(og) * hn_pair).astype(o_ref.dtype)
        return tuple(m_out)

    m0 = jnp.zeros((1, 1), jnp.float32)
    lax.fori_loop(0, n_chunks, step, (m0, m0))


def _mlstm(proj, gates, conv_w, norm_g, li, chunk=256, halo=16):
    b, s, _ = proj.shape
    chunk = min(chunk, s)
    n_pairs = C_HEADS // 2
    base = OFF_C // C_PAIR

    def pspec(t):
        return pl.BlockSpec((None, s, C_PAIR), lambda bi, p, t=t: (bi, 0, base + n_pairs * t + p))

    return pl.pallas_call(
        functools.partial(_mlstm_kernel, chunk=chunk, halo=halo),
        grid=(b, n_pairs),
        in_specs=[
            pspec(0), pspec(1), pspec(2), pspec(3),
            pl.BlockSpec((None, s, GATE_LANES), lambda bi, p: (bi, 0, 0)),
            pl.BlockSpec((None, CONV_K, C_PAIR), lambda bi, p: (li, 0, p)),
            pl.BlockSpec((None, CONV_K, C_PAIR), lambda bi, p: (li, 0, n_pairs + p)),
            pl.BlockSpec((1, C_PAIR), lambda bi, p: (0, 0)),
        ],
        out_specs=pl.BlockSpec((None, s, C_PAIR), lambda bi, p: (bi, 0, p)),
        out_shape=jax.ShapeDtypeStruct((b, s, C_WIDTH), jnp.bfloat16),
        scratch_shapes=[
            pltpu.VMEM((s, C_PAIR), jnp.bfloat16),
            pltpu.VMEM((s, C_PAIR), jnp.bfloat16),
            pltpu.VMEM((2, C_WIN, C_WIN), jnp.float32),
            pltpu.VMEM((GATE_LANES, chunk), jnp.float32),
        ],
        compiler_params=_params(("parallel", "parallel")),
        name="mlstm",
    )(proj, proj, proj, proj, gates, conv_w, conv_w, norm_g)


def _out_proj_ln_kernel(ya_ref, yb_ref, yc_ref, w_ref, x_ref, g_ref, b_ref, o32_ref, o16_ref, w16_ref, *, alpha):
    @pl.when(pl.program_id(0) == 0)
    def _():
        w16_ref[...] = w_ref[...].astype(jnp.bfloat16)

    acc = (_dot(ya_ref[...], w16_ref[0:A_WIDTH, :])
           + _dot(yb_ref[...], w16_ref[A_WIDTH:A_WIDTH + B_WIDTH, :])
           + _dot(yc_ref[...], w16_ref[A_WIDTH + B_WIDTH:, :]))
    y = _layer_norm(alpha * x_ref[...] + acc, g_ref[...], b_ref[...])
    o32_ref[...] = y
    o16_ref[...] = y.astype(o16_ref.dtype)


def _out_proj_ln(ya, yb, yc, w_out, x, g, b, li, alpha, tm=512):
    m, d = x.shape
    tm = min(tm, m)
    row = lambda w: pl.BlockSpec((tm, w), lambda i: (i, 0))
    vec = pl.BlockSpec((None, 1, d), lambda i: (li, 0, 0))
    return pl.pallas_call(
        functools.partial(_out_proj_ln_kernel, alpha=alpha),
        grid=(m // tm,),
        in_specs=[row(ya.shape[1]), row(yb.shape[1]), row(yc.shape[1]),
                  pl.BlockSpec((None, d, d), lambda i: (li, 0, 0), pipeline_mode=pl.Buffered(1)),
                  row(d), vec, vec],
        out_specs=[row(d), row(d)],
        out_shape=[jax.ShapeDtypeStruct((m, d), jnp.float32),
                   jax.ShapeDtypeStruct((m, d), jnp.bfloat16)],
        scratch_shapes=[pltpu.VMEM((d, d), jnp.bfloat16)],
        compiler_params=_params(("arbitrary",)),
        name="out_proj_ln",
    )(ya, yb, yc, w_out, x, g, b)


def _ffn_up_kernel(x_ref, wg_ref, wu_ref, wd_ref, o_ref, wd16_ref, wg16_ref, wu16_ref):
    @pl.when(pl.program_id(1) == 0)
    def _():
        wg16_ref[...] = wg_ref[...].astype(jnp.bfloat16)
        wu16_ref[...] = wu_ref[...].astype(jnp.bfloat16)
        wd16_ref[...] = wd_ref[...].astype(jnp.bfloat16)

    x = x_ref[...]
    a = _dot(x, wg16_ref[...])
    u = _dot(x, wu16_ref[...])
    o_ref[...] = (a * _sigmoid(a) * u).astype(o_ref.dtype)


def _ffn_up(xb, w_gate, w_up, w_down, li, tm=1024, tn=512):
    m, k = xb.shape
    n = w_gate.shape[2]
    d = w_down.shape[2]
    tm, tn = min(tm, m), min(tn, n)
    return pl.pallas_call(
        _ffn_up_kernel,
        grid=(n // tn, m // tm),
        in_specs=[
            pl.BlockSpec((tm, k), lambda j, i: (i, 0)),
            pl.BlockSpec((None, k, tn), lambda j, i: (li, 0, j)),
            pl.BlockSpec((None, k, tn), lambda j, i: (li, 0, j)),
            pl.BlockSpec((None, tn, d), lambda j, i: (li, j, 0)),
        ],
        out_specs=[pl.BlockSpec((tm, tn), lambda j, i: (i, j)),
                   pl.BlockSpec((tn, d), lambda j, i: (j, 0))],
        out_shape=[jax.ShapeDtypeStruct((m, n), jnp.bfloat16),
                   jax.ShapeDtypeStruct((n, d), jnp.bfloat16)],
        scratch_shapes=[pltpu.VMEM((k, tn), jnp.bfloat16), pltpu.VMEM((k, tn), jnp.bfloat16)],
        compiler_params=_params(("arbitrary", "arbitrary")),
        name="ffn_up",
    )(xb, w_gate, w_up, w_down)


def _ffn_down_ln_kernel(h_ref, w_ref, x_ref, g_ref, b_ref, wg_ref, gb_ref, o32_ref, o16_ref, gate_ref, *, alpha):
    kk = pl.program_id(1)

    @pl.when(kk == 0)
    def _():
        o32_ref[...] = jnp.zeros(o32_ref.shape, jnp.float32)

    o32_ref[...] += _dot(h_ref[...], w_ref[...])

    @pl.when(kk == pl.num_programs(1) - 1)
    def _():
        y = _layer_norm(alpha * x_ref[...] + o32_ref[...], g_ref[...], b_ref[...])
        y16 = y.astype(jnp.bfloat16)
        o32_ref[...] = y
        o16_ref[...] = y16
        gate_ref[...] = _dot(y16, wg_ref[...]) + gb_ref[...]


def _ffn_down_ln(hid, wd16, x, g, b, wg_next, gb_next, li, alpha, tm=512, tk=1408):
    m, f = hid.shape
    d = wd16.shape[1]
    tm, tk = min(tm, m), min(tk, f)
    vec = pl.BlockSpec((None, 1, d), lambda i, kk: (li, 0, 0))
    row = pl.BlockSpec((tm, d), lambda i, kk: (i, 0))
    return pl.pallas_call(
        functools.partial(_ffn_down_ln_kernel, alpha=alpha),
        grid=(m // tm, f // tk),
        in_specs=[
            pl.BlockSpec((tm, tk), lambda i, kk: (i, kk)),
            pl.BlockSpec((tk, d), lambda i, kk: (kk, 0)),
            row, vec, vec,
            pl.BlockSpec((d, GATE_LANES), lambda i, kk: (0, 0)),
            pl.BlockSpec((1, GATE_LANES), lambda i, kk: (0, 0)),
        ],
        out_specs=[row, row, pl.BlockSpec((tm, GATE_LANES), lambda i, kk: (i, 0))],
        out_shape=[jax.ShapeDtypeStruct((m, d), jnp.float32),
                   jax.ShapeDtypeStruct((m, d), jnp.bfloat16),
                   jax.ShapeDtypeStruct((m, GATE_LANES), jnp.float32)],
        compiler_params=_params(("parallel", "arbitrary")),
        name="ffn_down_ln",
    )(hid, wd16, x, g, b, wg_next, gb_next)


def _gate_params(w_in, gate_bias, li):
    pad = GATE_LANES - 2 * C_HEADS
    wg = jnp.pad(w_in[li, :, OFF_G:OFF_G + 2 * C_HEADS], ((0, 0), (0, pad))).astype(jnp.bfloat16)
    gb = jnp.pad(gate_bias[li], (0, pad)).reshape(1, GATE_LANES)
    return wg, gb


def kernel(x, w_in, gate_bias, conv_w, lam, subln_a, norm_c, rel_bias, w_out, ln1_g, ln1_b,
           w_gate, w_up, w_down, ln2_g, ln2_b):
    bsz, s_len, d = x.shape
    depth = w_in.shape[0]
    alpha = (2.0 * depth) ** 0.25
    m = bsz * s_len
    x32 = x.reshape(m, d)
    gate_params = [_gate_params(w_in, gate_bias, li) for li in range(depth)]
    ln1_g, ln1_b, ln2_g, ln2_b = (p.reshape(depth, 1, d) for p in (ln1_g, ln1_b, ln2_g, ln2_b))
    x16, gates = _cast_gate(x32, *gate_params[0])
    for li in range(depth):
        proj = _in_proj(x16, w_in, li).reshape(bsz, s_len, OFF_G)
        gates = gates.reshape(bsz, s_len, GATE_LANES)
        ya = _attn_a(proj, lam, subln_a, li)
        yb = _attn_b(proj, rel_bias[li])
        norm_g = jnp.tile(norm_c[li], 2).reshape(1, C_PAIR)
        yc = _mlstm(proj, gates, conv_w, norm_g, li)
        x32, x16 = _out_proj_ln(ya.reshape(m, A_WIDTH), yb.reshape(m, B_WIDTH), yc.reshape(m, C_WIDTH),
                                w_out, x32, ln1_g, ln1_b, li, alpha)
        hid, wd16 = _ffn_up(x16, w_gate, w_up, w_down, li)
        x32, x16, gates = _ffn_down_ln(hid, wd16, x32, ln2_g, ln2_b, *gate_params[(li + 1) % depth], li, alpha)
    return x32.reshape(bsz, s_len, d)
```

```python
import functools
import math

import numpy as np
import jax
import jax.numpy as jnp
from jax import lax
from jax.experimental import pallas as pl
from jax.experimental.pallas import tpu as pltpu

CHUNK = 64
A_HEADS = 4
A_HEAD_DIM = 128
A_HALF = 64
B_HEADS = 6
B_HEAD_DIM = 128
B_PAST_CHUNKS = 8
REL_CLIP = 256
C_HEADS = 4
C_HEAD_DIM = 192
C_PAIR = 2 * C_HEAD_DIM
C_WIN = 256
CONV_K = 4
NEG = -1e30
GATE_LANES = 128

A_WIDTH = A_HEADS * A_HEAD_DIM
B_WIDTH = B_HEADS * B_HEAD_DIM
C_WIDTH = C_HEADS * C_HEAD_DIM
OFF_A = 0
OFF_B = 3 * A_WIDTH
OFF_C = OFF_B + 3 * B_WIDTH
OFF_G = OFF_C + 4 * C_WIDTH
PROJ_TILE = 768

VMEM_LIMIT = 56 * 1024 * 1024

_NT = (((1,), (1,)), ((), ()))


def _params(sem):
    return pltpu.CompilerParams(dimension_semantics=sem, vmem_limit_bytes=VMEM_LIMIT)


def _dot(a, b):
    return jnp.dot(a, b, preferred_element_type=jnp.float32)


def _dot_nt(a, b):
    return lax.dot_general(a, b, _NT, preferred_element_type=jnp.float32)


def _log_sigmoid(x):
    return jnp.minimum(x, 0.0) - jnp.log(1.0 + jnp.exp(-jnp.abs(x)))


def _sigmoid(x):
    return 1.0 / (1.0 + jnp.exp(-x))


def _layer_norm(z, g, b):
    mu = jnp.mean(z, axis=1, keepdims=True)
    zc = z - mu
    var = jnp.mean(zc * zc, axis=1, keepdims=True)
    return zc * lax.rsqrt(var + 1e-5) * g + b


N_GATES = 2 * C_HEADS


def _gate_weight_spec(li, d):
    assert OFF_G % N_GATES == 0
    return pl.BlockSpec((None, N_GATES, d), lambda *_: (li, OFF_G // N_GATES, 0))


def _gate_proj(x16, wg_ref, gb_ref):
    wg = wg_ref[...].astype(jnp.bfloat16)
    wg = jnp.concatenate([wg, jnp.zeros((GATE_LANES - N_GATES, wg.shape[1]), wg.dtype)], axis=0)
    return _dot_nt(x16, wg) + gb_ref[...]


def _cast_gate_kernel(x_ref, wg_ref, gb_ref, x16_ref, g_ref):
    x16 = x_ref[...].astype(jnp.bfloat16)
    x16_ref[...] = x16
    g_ref[...] = _gate_proj(x16, wg_ref, gb_ref)


def _cast_gate(x, w_in_t, gb, li, tm=1024):
    m, d = x.shape
    tm = min(tm, m)
    return pl.pallas_call(
        _cast_gate_kernel,
        grid=(m // tm,),
        in_specs=[
            pl.BlockSpec((tm, d), lambda i: (i, 0)),
            _gate_weight_spec(li, d),
            pl.BlockSpec((1, GATE_LANES), lambda i: (0, 0)),
        ],
        out_specs=[pl.BlockSpec((tm, d), lambda i: (i, 0)),
                   pl.BlockSpec((tm, GATE_LANES), lambda i: (i, 0))],
        out_shape=[jax.ShapeDtypeStruct((m, d), jnp.bfloat16),
                   jax.ShapeDtypeStruct((m, GATE_LANES), jnp.float32)],
        compiler_params=_params(("parallel",)),
        name="cast_gate",
    )(x, w_in_t, gb)


def _in_proj_kernel(x_ref, w_ref, o_ref, w16_ref):
    @pl.when(pl.program_id(1) == 0)
    def _():
        w16_ref[...] = w_ref[...].astype(jnp.bfloat16)

    o_ref[...] = _dot_nt(x_ref[...], w16_ref[...]).astype(o_ref.dtype)


def _in_proj(xb, w_in_t, li, tm=1024, tn=PROJ_TILE):
    m, k = xb.shape
    tm = min(tm, m)
    n = OFF_G
    return pl.pallas_call(
        _in_proj_kernel,
        grid=(n // tn, m // tm),
        in_specs=[
            pl.BlockSpec((tm, k), lambda j, i: (i, 0)),
            pl.BlockSpec((None, tn, k), lambda j, i: (li, j, 0)),
        ],
        out_specs=pl.BlockSpec((tm, tn), lambda j, i: (i, j)),
        out_shape=jax.ShapeDtypeStruct((m, n), jnp.bfloat16),
        scratch_shapes=[pltpu.VMEM((tn, k), jnp.bfloat16)],
        compiler_params=_params(("arbitrary", "arbitrary")),
        name="in_proj",
    )(xb, w_in_t)


def _attn_a_kernel(q_ref, k_ref, v_ref, lam_ref, g_ref, slope_ref, o_ref, qt1_ref, qt2_ref, vt_ref,
                   *, tq, lam_init):
    s_len = q_ref.shape[0]
    nq = s_len // tq
    slope = slope_ref[0:1, 0:1]
    lf = lam_ref[...]
    lam_full = (jnp.exp(jnp.sum(lf[0:1] * lf[1:2], axis=1, keepdims=True))
                - jnp.exp(jnp.sum(lf[2:3] * lf[3:4], axis=1, keepdims=True)) + lam_init)
    feat = lax.broadcasted_iota(jnp.int32, (A_HEAD_DIM, tq), 0)
    first_half = feat < A_HALF
    for i in range(nq):
        rows = slice(i * tq, (i + 1) * tq)
        qt = (q_ref[rows, :].astype(jnp.float32) * (A_HALF ** -0.5)).T
        qt1_ref[:, rows] = jnp.where(first_half, qt, 0.0).astype(jnp.bfloat16)
        qt2_ref[:, rows] = jnp.where(first_half, 0.0, qt).astype(jnp.bfloat16)
        vt_ref[:, rows] = v_ref[rows, :].astype(jnp.float32).T.astype(jnp.bfloat16)

    key = lax.broadcasted_iota(jnp.int32, (tq, tq), 0)
    qry = lax.broadcasted_iota(jnp.int32, (tq, tq), 1)
    rel = (qry - key).astype(jnp.float32)
    bias_off = -slope * rel
    bias_diag = -slope * jnp.abs(rel)
    diag_ok = (key // CHUNK) <= (qry // CHUNK)

    def update(t, shift_c, vt, m_prev, l_prev, acc_prev):
        m_new = jnp.maximum(m_prev, jnp.max(t, axis=0, keepdims=True) + shift_c)
        alpha = jnp.exp(m_prev - m_new)
        p = jnp.exp(t - (m_new - shift_c))
        l_new = alpha * l_prev + jnp.sum(p, axis=0, keepdims=True)
        acc_new = alpha * acc_prev + _dot(vt, p.astype(jnp.bfloat16))
        return m_new, l_new, acc_new

    for i in range(nq):
        cols = slice(i * tq, (i + 1) * tq)
        qt1 = qt1_ref[:, cols]
        qt2 = qt2_ref[:, cols]
        st1 = (jnp.full((1, tq), NEG, jnp.float32), jnp.zeros((1, tq), jnp.float32),
               jnp.zeros((A_HEAD_DIM, tq), jnp.float32))
        st2 = st1
        for j in range(i + 1):
            krows = slice(j * tq, (j + 1) * tq)
            k = k_ref[krows, :]
            vt = vt_ref[:, krows]
            if j < i:
                shift_c = -slope * float((i - j) * tq)
                t1 = _dot(k, qt1) + bias_off
                t2 = _dot(k, qt2) + bias_off
            else:
                shift_c = jnp.zeros((1, 1), jnp.float32)
                t1 = jnp.where(diag_ok, _dot(k, qt1) + bias_diag, NEG)
                t2 = jnp.where(diag_ok, _dot(k, qt2) + bias_diag, NEG)
            st1 = update(t1, shift_c, vt, *st1)
            st2 = update(t2, shift_c, vt, *st2)
        ot = st1[2] * (1.0 / st1[1]) - (lam_full / st2[1]) * st2[2]
        o = ot.T
        ms = jnp.mean(o * o, axis=1, keepdims=True)
        o = o * lax.rsqrt(ms + 1e-6) * g_ref[...] * (1.0 - lam_init)
        o_ref[cols, :] = o.astype(o_ref.dtype)


def _attn_a(proj, lam, subln, li, tq=256):
    b, s, _ = proj.shape
    tq = min(tq, s)
    lam_init = 0.8 - 0.6 * math.exp(-0.3 * li)
    slopes = np.asarray([2.0 ** (-8.0 * (h + 1) / A_HEADS) for h in range(A_HEADS)], np.float32)
    slopes = jnp.asarray(np.broadcast_to(slopes[:, None, None], (A_HEADS, 1, 128)))
    qi = OFF_A // A_HEAD_DIM
    ki = (OFF_A + A_WIDTH) // A_HEAD_DIM
    vi = (OFF_A + 2 * A_WIDTH) // A_HEAD_DIM

    def hspec(base):
        return pl.BlockSpec((None, s, A_HEAD_DIM), lambda bi, h, base=base: (bi, 0, base + h))

    return pl.pallas_call(
        functools.partial(_attn_a_kernel, tq=tq, lam_init=lam_init),
        grid=(b, A_HEADS),
        in_specs=[
            hspec(qi), hspec(ki), hspec(vi),
            pl.BlockSpec((None, 4, A_HALF), lambda bi, h: (li, 0, 0)),
            pl.BlockSpec((None, 1, A_HEAD_DIM), lambda bi, h: (li, 0, 0)),
            pl.BlockSpec((None, 1, 128), lambda bi, h: (h, 0, 0)),
        ],
        out_specs=pl.BlockSpec((None, s, A_HEAD_DIM), lambda bi, h: (bi, 0, h)),
        out_shape=jax.ShapeDtypeStruct((b, s, A_WIDTH), jnp.bfloat16),
        scratch_shapes=[
            pltpu.VMEM((A_HEAD_DIM, s), jnp.bfloat16),
            pltpu.VMEM((A_HEAD_DIM, s), jnp.bfloat16),
            pltpu.VMEM((A_HEAD_DIM, s), jnp.bfloat16),
        ],
        compiler_params=_params(("parallel", "parallel")),
        name="attn_a",
    )(proj, proj, proj, lam, subln.reshape(-1, 1, A_HEAD_DIM), slopes)


def _attn_b_kernel(q_ref, k_ref, v_ref, t_ref, o_ref, kpad, vpad, *, tq, pad):
    s_len = q_ref.shape[0]
    win = pad + tq
    kpad[0:pad, :] = jnp.zeros((pad, B_HEAD_DIM), kpad.dtype)
    vpad[0:pad, :] = jnp.zeros((pad, B_HEAD_DIM), vpad.dtype)
    kpad[pad:pad + s_len, :] = k_ref[...]
    vpad[pad:pad + s_len, :] = v_ref[...]
    table = t_ref[...]
    jcol = lax.broadcasted_iota(jnp.int32, (tq, win), 1)
    scale = B_HEAD_DIM ** -0.5

    def q_block(i, carry):
        qs = pl.multiple_of(i * tq, tq)
        q = q_ref[pl.ds(qs, tq), :]
        kw = kpad[pl.ds(qs, win), :]
        vw = vpad[pl.ds(qs, win), :]
        s = _dot_nt(q, kw) * scale + table
        s = jnp.where(jcol >= pad - qs, s, NEG)
        m = jnp.max(s, axis=1, keepdims=True)
        p = jnp.exp(s - m)
        l = jnp.sum(p, axis=1, keepdims=True)
        o = _dot(p.astype(jnp.bfloat16), vw) / l
        o_ref[pl.ds(qs, tq), :] = o.astype(o_ref.dtype)
        return carry

    lax.fori_loop(0, s_len // tq, q_block, 0, unroll=4)


def _band_bias_table(rel_bias, tq, pad):
    win = pad + tq
    n = win + tq - 1
    n_hi = pad + tq - REL_CLIP
    lo = REL_CLIP - tq + 1
    assert n_hi >= 0 and lo >= 0
    rb = rel_bias.astype(jnp.float32)
    heads = rb.shape[0]
    u = jnp.concatenate([jnp.repeat(rb[:, 2 * REL_CLIP:], n_hi, axis=1), rb[:, lo:2 * REL_CLIP][:, ::-1]], axis=1)
    width = n + 1
    upad = jnp.pad(u, ((0, 0), (0, width + 1 - n)))
    skew = jnp.tile(upad, (1, tq))[:, :tq * width].reshape(heads, tq, width)
    table = skew[:, :, tq - 1:tq - 1 + win]
    r = np.arange(tq)[:, None]
    j = np.arange(win)[None, :]
    kc, qc = j // CHUNK, B_PAST_CHUNKS + r // CHUNK
    in_band = (kc <= qc) & (kc >= qc - B_PAST_CHUNKS)
    return jnp.where(jnp.asarray(in_band)[None], table, NEG)


def _attn_b(proj, rel_bias, tq=128):
    b, s, _ = proj.shape
    pad = B_PAST_CHUNKS * CHUNK
    table = _band_bias_table(rel_bias, tq, pad)
    qi = OFF_B // B_HEAD_DIM
    ki = (OFF_B + B_WIDTH) // B_HEAD_DIM
    vi = (OFF_B + 2 * B_WIDTH) // B_HEAD_DIM

    def hspec(base):
        return pl.BlockSpec((None, s, B_HEAD_DIM), lambda bi, h, base=base: (bi, 0, base + h))

    return pl.pallas_call(
        functools.partial(_attn_b_kernel, tq=tq, pad=pad),
        grid=(b, B_HEADS),
        in_specs=[
            hspec(qi), hspec(ki), hspec(vi),
            pl.BlockSpec((None, tq, pad + tq), lambda bi, h: (h, 0, 0)),
        ],
        out_specs=pl.BlockSpec((None, s, B_HEAD_DIM), lambda bi, h: (bi, 0, h)),
        out_shape=jax.ShapeDtypeStruct((b, s, B_WIDTH), jnp.bfloat16),
        scratch_shapes=[
            pltpu.VMEM((pad + s, B_HEAD_DIM), jnp.bfloat16),
            pltpu.VMEM((pad + s, B_HEAD_DIM), jnp.bfloat16),
        ],
        compiler_params=_params(("parallel", "parallel")),
        name="attn_b",
    )(proj, proj, proj, table)


def _cumsum_rows(x):
    n = x.shape[0]
    row = lax.broadcasted_iota(jnp.int32, x.shape, 0)
    sh = 1
    while sh < n:
        x = x + jnp.where(row >= sh, pltpu.roll(x, sh, 0), 0.0)
        sh *= 2
    return x


def _mlstm_kernel(q_ref, k_ref, v_ref, og_ref, gate_ref, cq_ref, ck_ref, ng_ref, o_ref,
                  qs_ref, ks_ref, state_ref, gt_ref, *, chunk, halo):
    s_len = q_ref.shape[0]
    n_chunks = s_len // chunk
    pair = pl.program_id(1)
    lane = lax.broadcasted_iota(jnp.int32, (chunk, C_WIN), 1)
    glane = lax.broadcasted_iota(jnp.int32, (chunk, GATE_LANES), 1)
    trow = lax.broadcasted_iota(jnp.int32, (chunk, chunk), 0)
    tcol = lax.broadcasted_iota(jnp.int32, (chunk, chunk), 1)
    causal = tcol <= trow
    windows = ((0, lane < C_HEAD_DIM, C_HEAD_DIM),
               (C_PAIR - C_WIN, lane >= C_WIN - C_HEAD_DIM, 0))

    def conv_chunk(c, carry):
        st = pl.multiple_of(c * chunk, chunk)
        prev_st = pl.multiple_of(jnp.maximum(st - halo, 0), halo)
        keep = jnp.where(c > 0, 1.0, 0.0)
        for src, w_ref, dst, post in ((q_ref, cq_ref, qs_ref, 1.0),
                                      (k_ref, ck_ref, ks_ref, C_HEAD_DIM ** -0.5)):
            cur = src[pl.ds(st, chunk), :].astype(jnp.float32)
            prev = src[pl.ds(prev_st, halo), :].astype(jnp.float32) * keep
            xc = jnp.concatenate([prev, cur], axis=0)
            w = w_ref[...]
            y = w[CONV_K - 1:CONV_K, :] * cur
            for back in range(1, CONV_K):
                y = y + w[CONV_K - 1 - back:CONV_K - back, :] * pltpu.roll(xc, back, 0)[halo:, :]
            y = y * _sigmoid(y) * post
            dst[pl.ds(st, chunk), :] = y.astype(dst.dtype)
        return carry

    lax.fori_loop(0, n_chunks, conv_chunk, 0)
    state_ref[...] = jnp.zeros(state_ref.shape, jnp.float32)

    def step(c, ms):
        st = pl.multiple_of(c * chunk, chunk)
        g = gate_ref[pl.ds(st, chunk), :]
        bcum = _cumsum_rows(_log_sigmoid(g))
        gb = jnp.where(glane < C_HEADS, g, bcum)
        gt_ref[...] = gb.T
        ng = ng_ref[...]
        hn, m_out = [], []
        for a, (w0, valid, spare) in enumerate(windows):
            h = 2 * pair + a
            m = ms[a]
            wcols = slice(w0, w0 + C_WIN)
            zero = jnp.zeros((), jnp.bfloat16)
            q = jnp.where(valid, qs_ref[pl.ds(st, chunk), wcols], zero)
            k = jnp.where(valid, ks_ref[pl.ds(st, chunk), wcols], zero)
            v = jnp.where(valid, v_ref[pl.ds(st, chunk), wcols], zero)
            v_ext = jnp.where(lane == spare, jnp.ones((), v.dtype), v)

            i_row = gt_ref[pl.ds(h, 1), :]
            b_row = gt_ref[pl.ds(C_HEADS + h, 1), :]
            i_col = jnp.sum(jnp.where(glane == h, gb, 0.0), axis=1, keepdims=True)
            b_col = jnp.sum(jnp.where(glane == C_HEADS + h, gb, 0.0), axis=1, keepdims=True)

            d = jnp.where(causal, b_col + (i_row - b_row), NEG)
            inter = b_col + m
            m_t = jnp.maximum(inter, jnp.max(d, axis=1, keepdims=True))
            sw = _dot_nt(q, k) * jnp.exp(d - m_t)
            w_inter = jnp.exp(inter - m_t)
            nd = (_dot(sw.astype(jnp.bfloat16), v_ext)
                  + w_inter * _dot(q, state_ref[a].astype(jnp.bfloat16)))
            den = jnp.sum(jnp.where(lane == spare, nd, 0.0), axis=1, keepdims=True)
            num = jnp.where(valid, nd, 0.0)
            hh = num / jnp.maximum(jnp.abs(den), jnp.exp(-m_t))
            msq = jnp.sum(hh * hh, axis=1, keepdims=True) * (1.0 / C_HEAD_DIM)
            hn.append(hh * lax.rsqrt(msq + 1e-6) * ng[:, wcols])

            b_last = b_col[chunk - 1:chunk, :]
            g_col = b_last - b_col + i_col
            m_new = jnp.maximum(b_last + m, jnp.max(g_col, axis=0, keepdims=True))
            wk = jnp.exp(g_col - m_new)
            decay = jnp.exp(b_last + m - m_new)
            kw_t = (k.astype(jnp.float32) * wk).T.astype(jnp.bfloat16)
            state_ref[a] = decay * state_ref[a] + _dot(kw_t, v_ext)
            m_out.append(m_new)

        mid = C_PAIR - C_WIN
        hn_pair = jnp.concatenate([hn[0][:, :mid], hn[0][:, mid:] + hn[1][:, :C_WIN - mid], hn[1][:, C_WIN - mid:]],
                                  axis=1)
        og = og_ref[pl.ds(st, chunk), :].astype(jnp.float32)
        o_ref[pl.ds(st, chunk), :] = (_sigmoid(og) * hn_pair).astype(o_ref.dtype)
        return tuple(m_out)

    m0 = jnp.zeros((1, 1), jnp.float32)
    lax.fori_loop(0, n_chunks, step, (m0, m0))


def _mlstm(proj, gates, conv_w, norm_g, li, chunk=256, halo=16):
    b, s, _ = proj.shape
    chunk = min(chunk, s)
    n_pairs = C_HEADS // 2
    base = OFF_C // C_PAIR

    def pspec(t):
        return pl.BlockSpec((None, s, C_PAIR), lambda bi, p, t=t: (bi, 0, base + n_pairs * t + p))

    return pl.pallas_call(
        functools.partial(_mlstm_kernel, chunk=chunk, halo=halo),
        grid=(b, n_pairs),
        in_specs=[
            pspec(0), pspec(1), pspec(2), pspec(3),
            pl.BlockSpec((None, s, GATE_LANES), lambda bi, p: (bi, 0, 0)),
            pl.BlockSpec((None, CONV_K, C_PAIR), lambda bi, p: (li, 0, p)),
            pl.BlockSpec((None, CONV_K, C_PAIR), lambda bi, p: (li, 0, n_pairs + p)),
            pl.BlockSpec((1, C_PAIR), lambda bi, p: (0, 0)),
        ],
        out_specs=pl.BlockSpec((None, s, C_PAIR), lambda bi, p: (bi, 0, p)),
        out_shape=jax.ShapeDtypeStruct((b, s, C_WIDTH), jnp.bfloat16),
        scratch_shapes=[
            pltpu.VMEM((s, C_PAIR), jnp.bfloat16),
            pltpu.VMEM((s, C_PAIR), jnp.bfloat16),
            pltpu.VMEM((2, C_WIN, C_WIN), jnp.float32),
            pltpu.VMEM((GATE_LANES, chunk), jnp.float32),
        ],
        compiler_params=_params(("parallel", "parallel")),
        name="mlstm",
    )(proj, proj, proj, proj, gates, conv_w, conv_w, norm_g)


def _out_proj_ln_kernel(ya_ref, yb_ref, yc_ref, w_ref, x_ref, g_ref, b_ref, o32_ref, o16_ref, w16_ref, *, alpha):
    @pl.when(pl.program_id(0) == 0)
    def _():
        w16_ref[...] = w_ref[...].astype(jnp.bfloat16)

    acc = (_dot(ya_ref[...], w16_ref[0:A_WIDTH, :])
           + _dot(yb_ref[...], w16_ref[A_WIDTH:A_WIDTH + B_WIDTH, :])
           + _dot(yc_ref[...], w16_ref[A_WIDTH + B_WIDTH:, :]))
    y = _layer_norm(alpha * x_ref[...] + acc, g_ref[...], b_ref[...])
    o32_ref[...] = y
    o16_ref[...] = y.astype(o16_ref.dtype)


def _out_proj_ln(ya, yb, yc, w_out, x, g, b, li, alpha, tm=512):
    m, d = x.shape
    tm = min(tm, m)
    row = lambda w: pl.BlockSpec((tm, w), lambda i: (i, 0))
    vec = pl.BlockSpec((None, 1, d), lambda i: (li, 0, 0))
    return pl.pallas_call(
        functools.partial(_out_proj_ln_kernel, alpha=alpha),
        grid=(m // tm,),
        in_specs=[row(ya.shape[1]), row(yb.shape[1]), row(yc.shape[1]),
                  pl.BlockSpec((None, d, d), lambda i: (li, 0, 0), pipeline_mode=pl.Buffered(1)),
                  row(d), vec, vec],
        out_specs=[row(d), row(d)],
        out_shape=[jax.ShapeDtypeStruct((m, d), jnp.float32),
                   jax.ShapeDtypeStruct((m, d), jnp.bfloat16)],
        scratch_shapes=[pltpu.VMEM((d, d), jnp.bfloat16)],
        compiler_params=_params(("arbitrary",)),
        name="out_proj_ln",
    )(ya, yb, yc, w_out, x, g, b)


def _ffn_up_kernel(x_ref, wg_ref, wu_ref, wd_ref, o_ref, wd16_ref, wg16_ref, wu16_ref):
    @pl.when(pl.program_id(1) == 0)
    def _():
        wg16_ref[...] = wg_ref[...].astype(jnp.bfloat16)
        wu16_ref[...] = wu_ref[...].astype(jnp.bfloat16)
        wd16_ref[...] = wd_ref[...].astype(jnp.bfloat16)

    x = x_ref[...]
    a = _dot(x, wg16_ref[...])
    u = _dot(x, wu16_ref[...])
    o_ref[...] = (a * _sigmoid(a) * u).astype(o_ref.dtype)


def _ffn_up(xb, w_gate, w_up, w_down, li, tm=1024, tn=512):
    m, k = xb.shape
    n = w_gate.shape[2]
    d = w_down.shape[2]
    tm, tn = min(tm, m), min(tn, n)
    return pl.pallas_call(
        _ffn_up_kernel,
        grid=(n // tn, m // tm),
        in_specs=[
            pl.BlockSpec((tm, k), lambda j, i: (i, 0)),
            pl.BlockSpec((None, k, tn), lambda j, i: (li, 0, j)),
            pl.BlockSpec((None, k, tn), lambda j, i: (li, 0, j)),
            pl.BlockSpec((None, tn, d), lambda j, i: (li, j, 0)),
        ],
        out_specs=[pl.BlockSpec((tm, tn), lambda j, i: (i, j)),
                   pl.BlockSpec((tn, d), lambda j, i: (j, 0))],
        out_shape=[jax.ShapeDtypeStruct((m, n), jnp.bfloat16),
                   jax.ShapeDtypeStruct((n, d), jnp.bfloat16)],
        scratch_shapes=[pltpu.VMEM((k, tn), jnp.bfloat16), pltpu.VMEM((k, tn), jnp.bfloat16)],
        compiler_params=_params(("arbitrary", "arbitrary")),
        name="ffn_up",
    )(xb, w_gate, w_up, w_down)


def _ffn_down_ln_kernel(h_ref, w_ref, x_ref, g_ref, b_ref, *rest, alpha, next_gates):
    if next_gates:
        wg_ref, gb_ref, o32_ref, o16_ref, gate_ref = rest
    else:
        (o32_ref,) = rest
    kk = pl.program_id(1)

    @pl.when(kk == 0)
    def _():
        o32_ref[...] = jnp.zeros(o32_ref.shape, jnp.float32)

    o32_ref[...] += _dot(h_ref[...], w_ref[...])

    @pl.when(kk == pl.num_programs(1) - 1)
    def _():
        y = _layer_norm(alpha * x_ref[...] + o32_ref[...], g_ref[...], b_ref[...])
        o32_ref[...] = y
        if next_gates:
            y16 = y.astype(jnp.bfloat16)
            o16_ref[...] = y16
            gate_ref[...] = _gate_proj(y16, wg_ref, gb_ref)


def _ffn_down_ln(hid, wd16, x, g, b, li, alpha, w_in=None, gb_next=None, tm=512, tk=2816):
    m, f = hid.shape
    d = wd16.shape[1]
    tm, tk = min(tm, m), min(tk, f)
    next_gates = w_in is not None
    vec = pl.BlockSpec((None, 1, d), lambda i, kk: (li, 0, 0))
    row = pl.BlockSpec((tm, d), lambda i, kk: (i, 0))
    in_specs = [pl.BlockSpec((tm, tk), lambda i, kk: (i, kk)),
                pl.BlockSpec((tk, d), lambda i, kk: (kk, 0)),
                row, vec, vec]
    out_specs = [row]
    out_shape = [jax.ShapeDtypeStruct((m, d), jnp.float32)]
    args = [hid, wd16, x, g, b]
    if next_gates:
        in_specs += [_gate_weight_spec(li + 1, d), pl.BlockSpec((1, GATE_LANES), lambda i, kk: (0, 0))]
        out_specs += [row, pl.BlockSpec((tm, GATE_LANES), lambda i, kk: (i, 0))]
        out_shape += [jax.ShapeDtypeStruct((m, d), jnp.bfloat16),
                      jax.ShapeDtypeStruct((m, GATE_LANES), jnp.float32)]
        args += [w_in, gb_next]
    return pl.pallas_call(
        functools.partial(_ffn_down_ln_kernel, alpha=alpha, next_gates=next_gates),
        grid=(m // tm, f // tk),
        in_specs=in_specs,
        out_specs=out_specs,
        out_shape=out_shape,
        compiler_params=_params(("parallel", "arbitrary")),
        name="ffn_down_ln",
    )(*args)


def _gate_bias_tile(gate_bias, li):
    return jnp.pad(gate_bias[li], (0, GATE_LANES - 2 * C_HEADS)).reshape(1, GATE_LANES)


def kernel(x, w_in, gate_bias, conv_w, lam, subln_a, norm_c, rel_bias, w_out, ln1_g, ln1_b,
           w_gate, w_up, w_down, ln2_g, ln2_b):
    bsz, s_len, d = x.shape
    depth = w_in.shape[0]
    alpha = (2.0 * depth) ** 0.25
    m = bsz * s_len
    x32 = x.reshape(m, d)
    ln1_g, ln1_b, ln2_g, ln2_b = (p.reshape(depth, 1, d) for p in (ln1_g, ln1_b, ln2_g, ln2_b))
    w_in_t = jnp.swapaxes(w_in, 1, 2)
    x16, gates = _cast_gate(x32, w_in_t, _gate_bias_tile(gate_bias, 0), 0)
    for li in range(depth):
        proj = _in_proj(x16, w_in_t, li).reshape(bsz, s_len, OFF_G)
        gates = gates.reshape(bsz, s_len, GATE_LANES)
        ya = _attn_a(proj, lam, subln_a, li)
        yb = _attn_b(proj, rel_bias[li])
        norm_g = jnp.tile(norm_c[li], 2).reshape(1, C_PAIR)
        yc = _mlstm(proj, gates, conv_w, norm_g, li)
        x32, x16 = _out_proj_ln(ya.reshape(m, A_WIDTH), yb.reshape(m, B_WIDTH), yc.reshape(m, C_WIDTH),
                                w_out, x32, ln1_g, ln1_b, li, alpha)
        hid, wd16 = _ffn_up(x16, w_gate, w_up, w_down, li)
        if li + 1 < depth:
            x32, x16, gates = _ffn_down_ln(hid, wd16, x32, ln2_g, ln2_b, li, alpha,
                                           w_in_t, _gate_bias_tile(gate_bias, li + 1))
        else:
            (x32,) = _ffn_down_ln(hid, wd16, x32, ln2_g, ln2_b, li, alpha)
    return x32.reshape(bsz, s_len, d)
```

```python
import functools
import math

import numpy as np
import jax
import jax.numpy as jnp
from jax import lax
from jax.experimental import pallas as pl
from jax.experimental.pallas import tpu as pltpu

CHUNK = 64
A_HEADS = 4
A_HEAD_DIM = 128
A_HALF = 64
B_HEADS = 6
B_HEAD_DIM = 128
B_PAST_CHUNKS = 8
REL_CLIP = 256
C_HEADS = 4
C_HEAD_DIM = 192
C_PAIR = 2 * C_HEAD_DIM
C_WIN = 256
CONV_K = 4
NEG = -1e30
GATE_LANES = 128

A_WIDTH = A_HEADS * A_HEAD_DIM
B_WIDTH = B_HEADS * B_HEAD_DIM
C_WIDTH = C_HEADS * C_HEAD_DIM
OFF_A = 0
OFF_B = 3 * A_WIDTH
OFF_C = OFF_B + 3 * B_WIDTH
OFF_G = OFF_C + 4 * C_WIDTH
PROJ_TILE = 768

VMEM_LIMIT = 56 * 1024 * 1024

_NT = (((1,), (1,)), ((), ()))


def _params(sem):
    return pltpu.CompilerParams(dimension_semantics=sem, vmem_limit_bytes=VMEM_LIMIT)


def _dot(a, b):
    return jnp.dot(a, b, preferred_element_type=jnp.float32)


def _dot_nt(a, b):
    return lax.dot_general(a, b, _NT, preferred_element_type=jnp.float32)


def _log_sigmoid(x):
    return jnp.minimum(x, 0.0) - jnp.log(1.0 + jnp.exp(-jnp.abs(x)))


def _sigmoid(x):
    return 1.0 / (1.0 + jnp.exp(-x))


def _layer_norm(z, g, b):
    mu = jnp.mean(z, axis=1, keepdims=True)
    zc = z - mu
    var = jnp.mean(zc * zc, axis=1, keepdims=True)
    return zc * lax.rsqrt(var + 1e-5) * g + b


N_GATES = 2 * C_HEADS


def _gate_weight_spec(li, d):
    assert OFF_G % N_GATES == 0
    return pl.BlockSpec((None, N_GATES, d), lambda *_: (li, OFF_G // N_GATES, 0))


def _gate_proj(x16, wg_ref, gb_ref):
    wg = wg_ref[...].astype(jnp.bfloat16)
    wg = jnp.concatenate([wg, jnp.zeros((GATE_LANES - N_GATES, wg.shape[1]), wg.dtype)], axis=0)
    return _dot_nt(x16, wg) + gb_ref[...]


def _cast_gate_kernel(x_ref, wg_ref, gb_ref, x16_ref, g_ref):
    x16 = x_ref[...].astype(jnp.bfloat16)
    x16_ref[...] = x16
    g_ref[...] = _gate_proj(x16, wg_ref, gb_ref)


def _cast_gate(x, w_in_t, gb, li, tm=1024):
    m, d = x.shape
    tm = min(tm, m)
    return pl.pallas_call(
        _cast_gate_kernel,
        grid=(m // tm,),
        in_specs=[
            pl.BlockSpec((tm, d), lambda i: (i, 0)),
            _gate_weight_spec(li, d),
            pl.BlockSpec((1, GATE_LANES), lambda i: (0, 0)),
        ],
        out_specs=[pl.BlockSpec((tm, d), lambda i: (i, 0)),
                   pl.BlockSpec((tm, GATE_LANES), lambda i: (i, 0))],
        out_shape=[jax.ShapeDtypeStruct((m, d), jnp.bfloat16),
                   jax.ShapeDtypeStruct((m, GATE_LANES), jnp.float32)],
        compiler_params=_params(("parallel",)),
        name="cast_gate",
    )(x, w_in_t, gb)


def _in_proj_kernel(x_ref, w_ref, o_ref, w16_ref):
    @pl.when(pl.program_id(1) == 0)
    def _():
        w16_ref[...] = w_ref[...].astype(jnp.bfloat16)

    o_ref[...] = _dot_nt(x_ref[...], w16_ref[...]).astype(o_ref.dtype)


def _in_proj(xb, w_in_t, li, tm=2048, tn=PROJ_TILE):
    m, k = xb.shape
    tm = min(tm, m)
    n = OFF_G
    return pl.pallas_call(
        _in_proj_kernel,
        grid=(n // tn, m // tm),
        in_specs=[
            pl.BlockSpec((tm, k), lambda j, i: (i, 0)),
            pl.BlockSpec((None, tn, k), lambda j, i: (li, j, 0)),
        ],
        out_specs=pl.BlockSpec((tm, tn), lambda j, i: (i, j)),
        out_shape=jax.ShapeDtypeStruct((m, n), jnp.bfloat16),
        scratch_shapes=[pltpu.VMEM((tn, k), jnp.bfloat16)],
        compiler_params=_params(("arbitrary", "arbitrary")),
        name="in_proj",
    )(xb, w_in_t)


def _attn_a_kernel(q_ref, k_ref, v_ref, lam_ref, g_ref, slope_ref, wd_ref, o_ref, wd16_ref,
                   qt1_ref, qt2_ref, vt_ref, *, tq, lam_init):
    wd16_ref[...] = wd_ref[...].astype(jnp.bfloat16)
    s_len = q_ref.shape[0]
    nq = s_len // tq
    slope = slope_ref[0:1, 0:1]
    lf = lam_ref[...]
    lam_full = (jnp.exp(jnp.sum(lf[0:1] * lf[1:2], axis=1, keepdims=True))
                - jnp.exp(jnp.sum(lf[2:3] * lf[3:4], axis=1, keepdims=True)) + lam_init)
    feat = lax.broadcasted_iota(jnp.int32, (A_HEAD_DIM, tq), 0)
    first_half = feat < A_HALF
    for i in range(nq):
        rows = slice(i * tq, (i + 1) * tq)
        qt = (q_ref[rows, :].astype(jnp.float32) * (A_HALF ** -0.5)).T
        qt1_ref[:, rows] = jnp.where(first_half, qt, 0.0).astype(jnp.bfloat16)
        qt2_ref[:, rows] = jnp.where(first_half, 0.0, qt).astype(jnp.bfloat16)
        vt_ref[:, rows] = v_ref[rows, :].astype(jnp.float32).T.astype(jnp.bfloat16)

    key = lax.broadcasted_iota(jnp.int32, (tq, tq), 0)
    qry = lax.broadcasted_iota(jnp.int32, (tq, tq), 1)
    rel = (qry - key).astype(jnp.float32)
    bias_off = -slope * rel
    bias_diag = -slope * jnp.abs(rel)
    diag_ok = (key // CHUNK) <= (qry // CHUNK)

    def update(t, shift_c, vt, m_prev, l_prev, acc_prev):
        m_new = jnp.maximum(m_prev, jnp.max(t, axis=0, keepdims=True) + shift_c)
        alpha = jnp.exp(m_prev - m_new)
        p = jnp.exp(t - (m_new - shift_c))
        l_new = alpha * l_prev + jnp.sum(p, axis=0, keepdims=True)
        acc_new = alpha * acc_prev + _dot(vt, p.astype(jnp.bfloat16))
        return m_new, l_new, acc_new

    for i in range(nq):
        cols = slice(i * tq, (i + 1) * tq)
        qt1 = qt1_ref[:, cols]
        qt2 = qt2_ref[:, cols]
        st1 = (jnp.full((1, tq), NEG, jnp.float32), jnp.zeros((1, tq), jnp.float32),
               jnp.zeros((A_HEAD_DIM, tq), jnp.float32))
        st2 = st1
        for j in range(i + 1):
            krows = slice(j * tq, (j + 1) * tq)
            k = k_ref[krows, :]
            vt = vt_ref[:, krows]
            if j < i:
                shift_c = -slope * float((i - j) * tq)
                t1 = _dot(k, qt1) + bias_off
                t2 = _dot(k, qt2) + bias_off
            else:
                shift_c = jnp.zeros((1, 1), jnp.float32)
                t1 = jnp.where(diag_ok, _dot(k, qt1) + bias_diag, NEG)
                t2 = jnp.where(diag_ok, _dot(k, qt2) + bias_diag, NEG)
            st1 = update(t1, shift_c, vt, *st1)
            st2 = update(t2, shift_c, vt, *st2)
        ot = st1[2] * (1.0 / st1[1]) - (lam_full / st2[1]) * st2[2]
        o = ot.T
        ms = jnp.mean(o * o, axis=1, keepdims=True)
        o = o * lax.rsqrt(ms + 1e-6) * g_ref[...] * (1.0 - lam_init)
        o_ref[cols, :] = o.astype(o_ref.dtype)


def _attn_a(proj, lam, subln, w_down, li, tq=256):
    b, s, _ = proj.shape
    tq = min(tq, s)
    f, d = w_down.shape[1:]
    slab = f // (b * A_HEADS)
    assert slab * b * A_HEADS == f and slab % 16 == 0
    lam_init = 0.8 - 0.6 * math.exp(-0.3 * li)
    slopes = np.asarray([2.0 ** (-8.0 * (h + 1) / A_HEADS) for h in range(A_HEADS)], np.float32)
    slopes = jnp.asarray(np.broadcast_to(slopes[:, None, None], (A_HEADS, 1, 128)))
    qi = OFF_A // A_HEAD_DIM
    ki = (OFF_A + A_WIDTH) // A_HEAD_DIM
    vi = (OFF_A + 2 * A_WIDTH) // A_HEAD_DIM

    def hspec(base):
        return pl.BlockSpec((None, s, A_HEAD_DIM), lambda bi, h, base=base: (bi, 0, base + h))

    return pl.pallas_call(
        functools.partial(_attn_a_kernel, tq=tq, lam_init=lam_init),
        grid=(b, A_HEADS),
        in_specs=[
            hspec(qi), hspec(ki), hspec(vi),
            pl.BlockSpec((None, 4, A_HALF), lambda bi, h: (li, 0, 0)),
            pl.BlockSpec((None, 1, A_HEAD_DIM), lambda bi, h: (li, 0, 0)),
            pl.BlockSpec((None, 1, 128), lambda bi, h: (h, 0, 0)),
            pl.BlockSpec((None, slab, d), lambda bi, h: (li, bi * A_HEADS + h, 0)),
        ],
        out_specs=[pl.BlockSpec((None, s, A_HEAD_DIM), lambda bi, h: (bi, 0, h)),
                   pl.BlockSpec((slab, d), lambda bi, h: (bi * A_HEADS + h, 0))],
        out_shape=[jax.ShapeDtypeStruct((b, s, A_WIDTH), jnp.bfloat16),
                   jax.ShapeDtypeStruct((f, d), jnp.bfloat16)],
        scratch_shapes=[
            pltpu.VMEM((A_HEAD_DIM, s), jnp.bfloat16),
            pltpu.VMEM((A_HEAD_DIM, s), jnp.bfloat16),
            pltpu.VMEM((A_HEAD_DIM, s), jnp.bfloat16),
        ],
        compiler_params=_params(("parallel", "parallel")),
        name="attn_a",
    )(proj, proj, proj, lam, subln.reshape(-1, 1, A_HEAD_DIM), slopes, w_down)


def _attn_b_kernel(q_ref, k_ref, v_ref, t_ref, o_ref, kpad, vpad, *, tq, pad):
    s_len = q_ref.shape[0]
    win = pad + tq
    kpad[0:pad, :] = jnp.zeros((pad, B_HEAD_DIM), kpad.dtype)
    vpad[0:pad, :] = jnp.zeros((pad, B_HEAD_DIM), vpad.dtype)
    kpad[pad:pad + s_len, :] = k_ref[...]
    vpad[pad:pad + s_len, :] = v_ref[...]
    table = t_ref[...]
    jcol = lax.broadcasted_iota(jnp.int32, (tq, win), 1)
    scale = B_HEAD_DIM ** -0.5

    def q_block(i, carry):
        qs = pl.multiple_of(i * tq, tq)
        q = q_ref[pl.ds(qs, tq), :]
        kw = kpad[pl.ds(qs, win), :]
        vw = vpad[pl.ds(qs, win), :]
        s = _dot_nt(q, kw) * scale + table
        s = jnp.where(jcol >= pad - qs, s, NEG)
        m = jnp.max(s, axis=1, keepdims=True)
        p = jnp.exp(s - m)
        l = jnp.sum(p, axis=1, keepdims=True)
        o = _dot(p.astype(jnp.bfloat16), vw) / l
        o_ref[pl.ds(qs, tq), :] = o.astype(o_ref.dtype)
        return carry

    lax.fori_loop(0, s_len // tq, q_block, 0, unroll=4)


def _band_bias_table(rel_bias, tq, pad):
    win = pad + tq
    n = win + tq - 1
    n_hi = pad + tq - REL_CLIP
    lo = REL_CLIP - tq + 1
    assert n_hi >= 0 and lo >= 0
    rb = rel_bias.astype(jnp.float32)
    heads = rb.shape[0]
    u = jnp.concatenate([jnp.repeat(rb[:, 2 * REL_CLIP:], n_hi, axis=1), rb[:, lo:2 * REL_CLIP][:, ::-1]], axis=1)
    width = n + 1
    upad = jnp.pad(u, ((0, 0), (0, width + 1 - n)))
    skew = jnp.tile(upad, (1, tq))[:, :tq * width].reshape(heads, tq, width)
    table = skew[:, :, tq - 1:tq - 1 + win]
    r = np.arange(tq)[:, None]
    j = np.arange(win)[None, :]
    kc, qc = j // CHUNK, B_PAST_CHUNKS + r // CHUNK
    in_band = (kc <= qc) & (kc >= qc - B_PAST_CHUNKS)
    return jnp.where(jnp.asarray(in_band)[None], table, NEG)


def _attn_b(proj, rel_bias, tq=128):
    b, s, _ = proj.shape
    pad = B_PAST_CHUNKS * CHUNK
    table = _band_bias_table(rel_bias, tq, pad)
    qi = OFF_B // B_HEAD_DIM
    ki = (OFF_B + B_WIDTH) // B_HEAD_DIM
    vi = (OFF_B + 2 * B_WIDTH) // B_HEAD_DIM

    def hspec(base):
        return pl.BlockSpec((None, s, B_HEAD_DIM), lambda bi, h, base=base: (bi, 0, base + h))

    return pl.pallas_call(
        functools.partial(_attn_b_kernel, tq=tq, pad=pad),
        grid=(b, B_HEADS),
        in_specs=[
            hspec(qi), hspec(ki), hspec(vi),
            pl.BlockSpec((None, tq, pad + tq), lambda bi, h: (h, 0, 0)),
        ],
        out_specs=pl.BlockSpec((None, s, B_HEAD_DIM), lambda bi, h: (bi, 0, h)),
        out_shape=jax.ShapeDtypeStruct((b, s, B_WIDTH), jnp.bfloat16),
        scratch_shapes=[
            pltpu.VMEM((pad + s, B_HEAD_DIM), jnp.bfloat16),
            pltpu.VMEM((pad + s, B_HEAD_DIM), jnp.bfloat16),
        ],
        compiler_params=_params(("parallel", "parallel")),
        name="attn_b",
    )(proj, proj, proj, table)


def _cumsum_rows(x):
    n = x.shape[0]
    row = lax.broadcasted_iota(jnp.int32, x.shape, 0)
    sh = 1
    while sh < n:
        x = x + jnp.where(row >= sh, pltpu.roll(x, sh, 0), 0.0)
        sh *= 2
    return x


def _mlstm_kernel(q_ref, k_ref, v_ref, og_ref, gate_ref, cq_ref, ck_ref, ng_ref, o_ref,
                  qs_ref, ks_ref, state_ref, gt_ref, *, chunk, halo):
    s_len = q_ref.shape[0]
    n_chunks = s_len // chunk
    pair = pl.program_id(1)
    lane = lax.broadcasted_iota(jnp.int32, (chunk, C_WIN), 1)
    glane = lax.broadcasted_iota(jnp.int32, (chunk, GATE_LANES), 1)
    trow = lax.broadcasted_iota(jnp.int32, (chunk, chunk), 0)
    tcol = lax.broadcasted_iota(jnp.int32, (chunk, chunk), 1)
    causal = tcol <= trow
    windows = ((0, lane < C_HEAD_DIM, C_HEAD_DIM),
               (C_PAIR - C_WIN, lane >= C_WIN - C_HEAD_DIM, 0))

    def conv_chunk(c, carry):
        st = pl.multiple_of(c * chunk, chunk)
        prev_st = pl.multiple_of(jnp.maximum(st - halo, 0), halo)
        keep = jnp.where(c > 0, 1.0, 0.0)
        for src, w_ref, dst, post in ((q_ref, cq_ref, qs_ref, 1.0),
                                      (k_ref, ck_ref, ks_ref, C_HEAD_DIM ** -0.5)):
            cur = src[pl.ds(st, chunk), :].astype(jnp.float32)
            prev = src[pl.ds(prev_st, halo), :].astype(jnp.float32) * keep
            xc = jnp.concatenate([prev, cur], axis=0)
            w = w_ref[...]
            y = w[CONV_K - 1:CONV_K, :] * cur
            for back in range(1, CONV_K):
                y = y + w[CONV_K - 1 - back:CONV_K - back, :] * pltpu.roll(xc, back, 0)[halo:, :]
            y = y * _sigmoid(y) * post
            dst[pl.ds(st, chunk), :] = y.astype(dst.dtype)
        return carry

    lax.fori_loop(0, n_chunks, conv_chunk, 0)
    state_ref[...] = jnp.zeros(state_ref.shape, jnp.float32)

    def step(c, ms):
        st = pl.multiple_of(c * chunk, chunk)
        g = gate_ref[pl.ds(st, chunk), :]
        bcum = _cumsum_rows(_log_sigmoid(g))
        gb = jnp.where(glane < C_HEADS, g, bcum)
        gt_ref[...] = gb.T
        ng = ng_ref[...]
        hn, m_out = [], []
        for a, (w0, valid, spare) in enumerate(windows):
            h = 2 * pair + a
            m = ms[a]
            wcols = slice(w0, w0 + C_WIN)
            zero = jnp.zeros((), jnp.bfloat16)
            q = jnp.where(valid, qs_ref[pl.ds(st, chunk), wcols], zero)
            k = jnp.where(valid, ks_ref[pl.ds(st, chunk), wcols], zero)
            v = jnp.where(valid, v_ref[pl.ds(st, chunk), wcols], zero)
            v_ext = jnp.where(lane == spare, jnp.ones((), v.dtype), v)

            i_row = gt_ref[pl.ds(h, 1), :]
            b_row = gt_ref[pl.ds(C_HEADS + h, 1), :]
            i_col = jnp.sum(jnp.where(glane == h, gb, 0.0), axis=1, keepdims=True)
            b_col = jnp.sum(jnp.where(glane == C_HEADS + h, gb, 0.0), axis=1, keepdims=True)

            d = jnp.where(causal, b_col + (i_row - b_row), NEG)
            inter = b_col + m
            m_t = jnp.maximum(inter, jnp.max(d, axis=1, keepdims=True))
            sw = _dot_nt(q, k) * jnp.exp(d - m_t)
            w_inter = jnp.exp(inter - m_t)
            nd = (_dot(sw.astype(jnp.bfloat16), v_ext)
                  + w_inter * _dot(q, state_ref[a].astype(jnp.bfloat16)))
            den = jnp.sum(jnp.where(lane == spare, nd, 0.0), axis=1, keepdims=True)
            num = jnp.where(valid, nd, 0.0)
            hh = num / jnp.maximum(jnp.abs(den), jnp.exp(-m_t))
            msq = jnp.sum(hh * hh, axis=1, keepdims=True) * (1.0 / C_HEAD_DIM)
            hn.append(hh * lax.rsqrt(msq + 1e-6) * ng[:, wcols])

            b_last = b_col[chunk - 1:chunk, :]
            g_col = b_last - b_col + i_col
            m_new = jnp.maximum(b_last + m, jnp.max(g_col, axis=0, keepdims=True))
            wk = jnp.exp(g_col - m_new)
            decay = jnp.exp(b_last + m - m_new)
            kw_t = (k.astype(jnp.float32) * wk).T.astype(jnp.bfloat16)
            state_ref[a] = decay * state_ref[a] + _dot(kw_t, v_ext)
            m_out.append(m_new)

        mid = C_PAIR - C_WIN
        hn_pair = jnp.concatenate([hn[0][:, :mid], hn[0][:, mid:] + hn[1][:, :C_WIN - mid], hn[1][:, C_WIN - mid:]],
                                  axis=1)
        og = og_ref[pl.ds(st, chunk), :].astype(jnp.float32)
        o_ref[pl.ds(st, chunk), :] = (_sigmoid(og) * hn_pair).astype(o_ref.dtype)
        return tuple(m_out)

    m0 = jnp.zeros((1, 1), jnp.float32)
    lax.fori_loop(0, n_chunks, step, (m0, m0))


def _mlstm(proj, gates, conv_w, norm_g, li, chunk=256, halo=16):
    b, s, _ = proj.shape
    chunk = min(chunk, s)
    n_pairs = C_HEADS // 2
    base = OFF_C // C_PAIR

    def pspec(t):
        return pl.BlockSpec((None, s, C_PAIR), lambda bi, p, t=t: (bi, 0, base + n_pairs * t + p))

    return pl.pallas_call(
        functools.partial(_mlstm_kernel, chunk=chunk, halo=halo),
        grid=(b, n_pairs),
        in_specs=[
            pspec(0), pspec(1), pspec(2), pspec(3),
            pl.BlockSpec((None, s, GATE_LANES), lambda bi, p: (bi, 0, 0)),
            pl.BlockSpec((None, CONV_K, C_PAIR), lambda bi, p: (li, 0, p)),
            pl.BlockSpec((None, CONV_K, C_PAIR), lambda bi, p: (li, 0, n_pairs + p)),
            pl.BlockSpec((1, C_PAIR), lambda bi, p: (0, 0)),
        ],
        out_specs=pl.BlockSpec((None, s, C_PAIR), lambda bi, p: (bi, 0, p)),
        out_shape=jax.ShapeDtypeStruct((b, s, C_WIDTH), jnp.bfloat16),
        scratch_shapes=[
            pltpu.VMEM((s, C_PAIR), jnp.bfloat16),
            pltpu.VMEM((s, C_PAIR), jnp.bfloat16),
            pltpu.VMEM((2, C_WIN, C_WIN), jnp.float32),
            pltpu.VMEM((GATE_LANES, chunk), jnp.float32),
        ],
        compiler_params=_params(("parallel", "parallel")),
        name="mlstm",
    )(proj, proj, proj, proj, gates, conv_w, conv_w, norm_g)


def _out_proj_ln_kernel(ya_ref, yb_ref, yc_ref, w_ref, x_ref, g_ref, b_ref, o32_ref, o16_ref, w16_ref, *, alpha):
    @pl.when(pl.program_id(0) == 0)
    def _():
        w16_ref[...] = w_ref[...].astype(jnp.bfloat16)

    acc = (_dot(ya_ref[...], w16_ref[0:A_WIDTH, :])
           + _dot(yb_ref[...], w16_ref[A_WIDTH:A_WIDTH + B_WIDTH, :])
           + _dot(yc_ref[...], w16_ref[A_WIDTH + B_WIDTH:, :]))
    y = _layer_norm(alpha * x_ref[...] + acc, g_ref[...], b_ref[...])
    o32_ref[...] = y
    o16_ref[...] = y.astype(o16_ref.dtype)


def _out_proj_ln(ya, yb, yc, w_out, x, g, b, li, alpha, tm=512):
    m, d = x.shape
    tm = min(tm, m)
    row = lambda w: pl.BlockSpec((tm, w), lambda i: (i, 0))
    vec = pl.BlockSpec((None, 1, d), lambda i: (li, 0, 0))
    return pl.pallas_call(
        functools.partial(_out_proj_ln_kernel, alpha=alpha),
        grid=(m // tm,),
        in_specs=[row(ya.shape[1]), row(yb.shape[1]), row(yc.shape[1]),
                  pl.BlockSpec((None, d, d), lambda i: (li, 0, 0), pipeline_mode=pl.Buffered(1)),
                  row(d), vec, vec],
        out_specs=[row(d), row(d)],
        out_shape=[jax.ShapeDtypeStruct((m, d), jnp.float32),
                   jax.ShapeDtypeStruct((m, d), jnp.bfloat16)],
        scratch_shapes=[pltpu.VMEM((d, d), jnp.bfloat16)],
        compiler_params=_params(("arbitrary",)),
        name="out_proj_ln",
    )(ya, yb, yc, w_out, x, g, b)


def _ffn_up_kernel(x_ref, wg_ref, wu_ref, o_ref, wg16_ref, wu16_ref):
    @pl.when(pl.program_id(1) == 0)
    def _():
        wg16_ref[...] = wg_ref[...].astype(jnp.bfloat16)
        wu16_ref[...] = wu_ref[...].astype(jnp.bfloat16)

    x = x_ref[...]
    a = _dot(x, wg16_ref[...])
    u = _dot(x, wu16_ref[...])
    o_ref[...] = (a * _sigmoid(a) * u).astype(o_ref.dtype)


def _ffn_up(xb, w_gate, w_up, li, tm=2048, tn=512):
    m, k = xb.shape
    n = w_gate.shape[2]
    tm, tn = min(tm, m), min(tn, n)
    return pl.pallas_call(
        _ffn_up_kernel,
        grid=(n // tn, m // tm),
        in_specs=[
            pl.BlockSpec((tm, k), lambda j, i: (i, 0)),
            pl.BlockSpec((None, k, tn), lambda j, i: (li, 0, j)),
            pl.BlockSpec((None, k, tn), lambda j, i: (li, 0, j)),
        ],
        out_specs=pl.BlockSpec((tm, tn), lambda j, i: (i, j)),
        out_shape=jax.ShapeDtypeStruct((m, n), jnp.bfloat16),
        scratch_shapes=[pltpu.VMEM((k, tn), jnp.bfloat16), pltpu.VMEM((k, tn), jnp.bfloat16)],
        compiler_params=_params(("arbitrary", "arbitrary")),
        name="ffn_up",
    )(xb, w_gate, w_up)


def _ffn_down_ln_kernel(h_ref, w_ref, x_ref, g_ref, b_ref, *rest, alpha, next_gates):
    if next_gates:
        wg_ref, gb_ref, o32_ref, o16_ref, gate_ref = rest
    else:
        (o32_ref,) = rest
    kk = pl.program_id(1)

    @pl.when(kk == 0)
    def _():
        o32_ref[...] = jnp.zeros(o32_ref.shape, jnp.float32)

    o32_ref[...] += _dot(h_ref[...], w_ref[...])

    @pl.when(kk == pl.num_programs(1) - 1)
    def _():
        y = _layer_norm(alpha * x_ref[...] + o32_ref[...], g_ref[...], b_ref[...])
        o32_ref[...] = y
        if next_gates:
            y16 = y.astype(jnp.bfloat16)
            o16_ref[...] = y16
            gate_ref[...] = _gate_proj(y16, wg_ref, gb_ref)


def _ffn_down_ln(hid, wd16, x, g, b, li, alpha, w_in=None, gb_next=None, tm=512, tk=2816):
    m, f = hid.shape
    d = wd16.shape[1]
    tm, tk = min(tm, m), min(tk, f)
    next_gates = w_in is not None
    vec = pl.BlockSpec((None, 1, d), lambda i, kk: (li, 0, 0))
    row = pl.BlockSpec((tm, d), lambda i, kk: (i, 0))
    in_specs = [pl.BlockSpec((tm, tk), lambda i, kk: (i, kk)),
                pl.BlockSpec((tk, d), lambda i, kk: (kk, 0)),
                row, vec, vec]
    out_specs = [row]
    out_shape = [jax.ShapeDtypeStruct((m, d), jnp.float32)]
    args = [hid, wd16, x, g, b]
    if next_gates:
        in_specs += [_gate_weight_spec(li + 1, d), pl.BlockSpec((1, GATE_LANES), lambda i, kk: (0, 0))]
        out_specs += [row, pl.BlockSpec((tm, GATE_LANES), lambda i, kk: (i, 0))]
        out_shape += [jax.ShapeDtypeStruct((m, d), jnp.bfloat16),
                      jax.ShapeDtypeStruct((m, GATE_LANES), jnp.float32)]
        args += [w_in, gb_next]
    return pl.pallas_call(
        functools.partial(_ffn_down_ln_kernel, alpha=alpha, next_gates=next_gates),
        grid=(m // tm, f // tk),
        in_specs=in_specs,
        out_specs=out_specs,
        out_shape=out_shape,
        compiler_params=_params(("parallel", "arbitrary")),
        name="ffn_down_ln",
    )(*args)


def _gate_bias_tile(gate_bias, li):
    return jnp.pad(gate_bias[li], (0, GATE_LANES - 2 * C_HEADS)).reshape(1, GATE_LANES)


def kernel(x, w_in, gate_bias, conv_w, lam, subln_a, norm_c, rel_bias, w_out, ln1_g, ln1_b,
           w_gate, w_up, w_down, ln2_g, ln2_b):
    bsz, s_len, d = x.shape
    depth = w_in.shape[0]
    alpha = (2.0 * depth) ** 0.25
    m = bsz * s_len
    x32 = x.reshape(m, d)
    ln1_g, ln1_b, ln2_g, ln2_b = (p.reshape(depth, 1, d) for p in (ln1_g, ln1_b, ln2_g, ln2_b))
    w_in_t = jnp.swapaxes(w_in, 1, 2)
    x16, gates = _cast_gate(x32, w_in_t, _gate_bias_tile(gate_bias, 0), 0)
    for li in range(depth):
        proj = _in_proj(x16, w_in_t, li).reshape(bsz, s_len, OFF_G)
        gates = gates.reshape(bsz, s_len, GATE_LANES)
        ya, wd16 = _attn_a(proj, lam, subln_a, w_down, li)
        yb = _attn_b(proj, rel_bias[li])
        norm_g = jnp.tile(norm_c[li], 2).reshape(1, C_PAIR)
        yc = _mlstm(proj, gates, conv_w, norm_g, li)
        x32, x16 = _out_proj_ln(ya.reshape(m, A_WIDTH), yb.reshape(m, B_WIDTH), yc.reshape(m, C_WIDTH),
                                w_out, x32, ln1_g, ln1_b, li, alpha)
        hid = _ffn_up(x16, w_gate, w_up, li)
        if li + 1 < depth:
            x32, x16, gates = _ffn_down_ln(hid, wd16, x32, ln2_g, ln2_b, li, alpha,
                                           w_in_t, _gate_bias_tile(gate_bias, li + 1))
        else:
            (x32,) = _ffn_down_ln(hid, wd16, x32, ln2_g, ln2_b, li, alpha)
    return x32.reshape(bsz, s_len, d)
```

```python
import functools
import math

import numpy as np
import jax
import jax.numpy as jnp
from jax import lax
from jax.experimental import pallas as pl
from jax.experimental.pallas import tpu as pltpu

CHUNK = 64
A_HEADS = 4
A_HEAD_DIM = 128
A_HALF = 64
B_HEADS = 6
B_HEAD_DIM = 128
B_PAST_CHUNKS = 8
REL_CLIP = 256
C_HEADS = 4
C_HEAD_DIM = 192
C_PAIR = 2 * C_HEAD_DIM
C_WIN = 256
CONV_K = 4
NEG = -1e30
GATE_LANES = 128

A_WIDTH = A_HEADS * A_HEAD_DIM
B_WIDTH = B_HEADS * B_HEAD_DIM
C_WIDTH = C_HEADS * C_HEAD_DIM
OFF_A = 0
OFF_B = 3 * A_WIDTH
OFF_C = OFF_B + 3 * B_WIDTH
OFF_G = OFF_C + 4 * C_WIDTH
PROJ_TILE = 768

VMEM_LIMIT = 56 * 1024 * 1024

_NT = (((1,), (1,)), ((), ()))


def _params(sem):
    return pltpu.CompilerParams(dimension_semantics=sem, vmem_limit_bytes=VMEM_LIMIT)


def _dot(a, b):
    return jnp.dot(a, b, preferred_element_type=jnp.float32)


def _dot_nt(a, b):
    return lax.dot_general(a, b, _NT, preferred_element_type=jnp.float32)


def _log_sigmoid(x):
    return jnp.minimum(x, 0.0) - jnp.log(1.0 + jnp.exp(-jnp.abs(x)))


def _sigmoid(x):
    return 1.0 / (1.0 + jnp.exp(-x))


def _layer_norm(z, g, b):
    mu = jnp.mean(z, axis=1, keepdims=True)
    zc = z - mu
    var = jnp.mean(zc * zc, axis=1, keepdims=True)
    return zc * lax.rsqrt(var + 1e-5) * g + b


N_GATES = 2 * C_HEADS


def _gate_weight_spec(li, d):
    assert OFF_G % N_GATES == 0
    return pl.BlockSpec((None, N_GATES, d), lambda *_: (li, OFF_G // N_GATES, 0))


def _gate_proj(x16, wg_ref, gb_ref):
    wg = wg_ref[...].astype(jnp.bfloat16)
    wg = jnp.concatenate([wg, jnp.zeros((GATE_LANES - N_GATES, wg.shape[1]), wg.dtype)], axis=0)
    return _dot_nt(x16, wg) + gb_ref[...]


def _cast_gate_kernel(x_ref, wg_ref, gb_ref, x16_ref, g_ref):
    x16 = x_ref[...].astype(jnp.bfloat16)
    x16_ref[...] = x16
    g_ref[...] = _gate_proj(x16, wg_ref, gb_ref)


def _cast_gate(x, w_in_t, gb, li, tm=1024):
    m, d = x.shape
    tm = min(tm, m)
    return pl.pallas_call(
        _cast_gate_kernel,
        grid=(m // tm,),
        in_specs=[
            pl.BlockSpec((tm, d), lambda i: (i, 0)),
            _gate_weight_spec(li, d),
            pl.BlockSpec((1, GATE_LANES), lambda i: (0, 0)),
        ],
        out_specs=[pl.BlockSpec((tm, d), lambda i: (i, 0)),
                   pl.BlockSpec((tm, GATE_LANES), lambda i: (i, 0))],
        out_shape=[jax.ShapeDtypeStruct((m, d), jnp.bfloat16),
                   jax.ShapeDtypeStruct((m, GATE_LANES), jnp.float32)],
        compiler_params=_params(("parallel",)),
        name="cast_gate",
    )(x, w_in_t, gb)


def _in_proj_kernel(x_ref, w_ref, o_ref, w16_ref):
    @pl.when(pl.program_id(1) == 0)
    def _():
        w16_ref[...] = w_ref[...].astype(jnp.bfloat16)

    o_ref[...] = _dot_nt(x_ref[...], w16_ref[...]).astype(o_ref.dtype)


def _in_proj(xb, w_in_t, li, tm=2048, tn=PROJ_TILE):
    m, k = xb.shape
    tm = min(tm, m)
    n = OFF_G
    return pl.pallas_call(
        _in_proj_kernel,
        grid=(n // tn, m // tm),
        in_specs=[
            pl.BlockSpec((tm, k), lambda j, i: (i, 0)),
            pl.BlockSpec((None, tn, k), lambda j, i: (li, j, 0)),
        ],
        out_specs=pl.BlockSpec((tm, tn), lambda j, i: (i, j)),
        out_shape=jax.ShapeDtypeStruct((m, n), jnp.bfloat16),
        scratch_shapes=[pltpu.VMEM((tn, k), jnp.bfloat16)],
        compiler_params=_params(("arbitrary", "arbitrary")),
        name="in_proj",
    )(xb, w_in_t)


def _attn_a_kernel(q_ref, k_ref, v_ref, lam_ref, g_ref, slope_ref, wd_ref, o_ref, wd16_ref,
                   qt1_ref, qt2_ref, vt_ref, *, tq, lam_init):
    wd16_ref[...] = wd_ref[...].astype(jnp.bfloat16)
    s_len = q_ref.shape[0]
    nq = s_len // tq
    slope = slope_ref[0:1, 0:1]
    lf = lam_ref[...]
    lam_full = (jnp.exp(jnp.sum(lf[0:1] * lf[1:2], axis=1, keepdims=True))
                - jnp.exp(jnp.sum(lf[2:3] * lf[3:4], axis=1, keepdims=True)) + lam_init)
    feat = lax.broadcasted_iota(jnp.int32, (A_HEAD_DIM, tq), 0)
    first_half = feat < A_HALF
    for i in range(nq):
        rows = slice(i * tq, (i + 1) * tq)
        qt = (q_ref[rows, :].astype(jnp.float32) * (A_HALF ** -0.5)).T
        qt1_ref[:, rows] = jnp.where(first_half, qt, 0.0).astype(jnp.bfloat16)
        qt2_ref[:, rows] = jnp.where(first_half, 0.0, qt).astype(jnp.bfloat16)
        vt_ref[:, rows] = v_ref[rows, :].astype(jnp.float32).T.astype(jnp.bfloat16)

    key = lax.broadcasted_iota(jnp.int32, (tq, tq), 0)
    qry = lax.broadcasted_iota(jnp.int32, (tq, tq), 1)
    rel = (qry - key).astype(jnp.float32)
    bias_off = -slope * rel
    bias_diag = -slope * jnp.abs(rel)
    diag_ok = (key // CHUNK) <= (qry // CHUNK)

    def update(t, shift_c, vt, m_prev, l_prev, acc_prev):
        m_new = jnp.maximum(m_prev, jnp.max(t, axis=0, keepdims=True) + shift_c)
        alpha = jnp.exp(m_prev - m_new)
        p = jnp.exp(t - (m_new - shift_c))
        l_new = alpha * l_prev + jnp.sum(p, axis=0, keepdims=True)
        acc_new = alpha * acc_prev + _dot(vt, p.astype(jnp.bfloat16))
        return m_new, l_new, acc_new

    for i in range(nq):
        cols = slice(i * tq, (i + 1) * tq)
        qt1 = qt1_ref[:, cols]
        qt2 = qt2_ref[:, cols]
        st1 = (jnp.full((1, tq), NEG, jnp.float32), jnp.zeros((1, tq), jnp.float32),
               jnp.zeros((A_HEAD_DIM, tq), jnp.float32))
        st2 = st1
        for j in range(i + 1):
            krows = slice(j * tq, (j + 1) * tq)
            k = k_ref[krows, :]
            vt = vt_ref[:, krows]
            if j < i:
                shift_c = -slope * float((i - j) * tq)
                t1 = _dot(k, qt1) + bias_off
                t2 = _dot(k, qt2) + bias_off
            else:
                shift_c = jnp.zeros((1, 1), jnp.float32)
                t1 = jnp.where(diag_ok, _dot(k, qt1) + bias_diag, NEG)
                t2 = jnp.where(diag_ok, _dot(k, qt2) + bias_diag, NEG)
            st1 = update(t1, shift_c, vt, *st1)
            st2 = update(t2, shift_c, vt, *st2)
        ot = st1[2] * (1.0 / st1[1]) - (lam_full / st2[1]) * st2[2]
        o = ot.T
        ms = jnp.mean(o * o, axis=1, keepdims=True)
        o = o * lax.rsqrt(ms + 1e-6) * g_ref[...] * (1.0 - lam_init)
        o_ref[cols, :] = o.astype(o_ref.dtype)


def _attn_a(proj, lam, subln, w_down, li, tq=256):
    b, s, _ = proj.shape
    tq = min(tq, s)
    f, d = w_down.shape[1:]
    slab = f // (b * A_HEADS)
    assert slab * b * A_HEADS == f and slab % 16 == 0
    lam_init = 0.8 - 0.6 * math.exp(-0.3 * li)
    slopes = np.asarray([2.0 ** (-8.0 * (h + 1) / A_HEADS) for h in range(A_HEADS)], np.float32)
    slopes = jnp.asarray(np.broadcast_to(slopes[:, None, None], (A_HEADS, 1, 128)))
    qi = OFF_A // A_HEAD_DIM
    ki = (OFF_A + A_WIDTH) // A_HEAD_DIM
    vi = (OFF_A + 2 * A_WIDTH) // A_HEAD_DIM

    def hspec(base):
        return pl.BlockSpec((None, s, A_HEAD_DIM), lambda bi, h, base=base: (bi, 0, base + h))

    return pl.pallas_call(
        functools.partial(_attn_a_kernel, tq=tq, lam_init=lam_init),
        grid=(b, A_HEADS),
        in_specs=[
            hspec(qi), hspec(ki), hspec(vi),
            pl.BlockSpec((None, 4, A_HALF), lambda bi, h: (li, 0, 0)),
            pl.BlockSpec((None, 1, A_HEAD_DIM), lambda bi, h: (li, 0, 0)),
            pl.BlockSpec((None, 1, 128), lambda bi, h: (h, 0, 0)),
            pl.BlockSpec((None, slab, d), lambda bi, h: (li, bi * A_HEADS + h, 0)),
        ],
        out_specs=[pl.BlockSpec((None, s, A_HEAD_DIM), lambda bi, h: (bi, 0, h)),
                   pl.BlockSpec((slab, d), lambda bi, h: (bi * A_HEADS + h, 0))],
        out_shape=[jax.ShapeDtypeStruct((b, s, A_WIDTH), jnp.bfloat16),
                   jax.ShapeDtypeStruct((f, d), jnp.bfloat16)],
        scratch_shapes=[
            pltpu.VMEM((A_HEAD_DIM, s), jnp.bfloat16),
            pltpu.VMEM((A_HEAD_DIM, s), jnp.bfloat16),
            pltpu.VMEM((A_HEAD_DIM, s), jnp.bfloat16),
        ],
        compiler_params=_params(("parallel", "parallel")),
        name="attn_a",
    )(proj, proj, proj, lam, subln.reshape(-1, 1, A_HEAD_DIM), slopes, w_down)


def _attn_b_kernel(q_ref, k_ref, v_ref, t_ref, o_ref, kpad, vpad, *, tq, pad):
    s_len = q_ref.shape[0]
    win = pad + tq
    kpad[0:pad, :] = jnp.zeros((pad, B_HEAD_DIM), kpad.dtype)
    vpad[0:pad, :] = jnp.zeros((pad, B_HEAD_DIM), vpad.dtype)
    kpad[pad:pad + s_len, :] = k_ref[...]
    vpad[pad:pad + s_len, :] = v_ref[...]
    table = t_ref[...]
    jcol = lax.broadcasted_iota(jnp.int32, (tq, win), 1)
    scale = B_HEAD_DIM ** -0.5

    def q_block(i, carry):
        qs = pl.multiple_of(i * tq, tq)
        q = q_ref[pl.ds(qs, tq), :]
        kw = kpad[pl.ds(qs, win), :]
        vw = vpad[pl.ds(qs, win), :]
        s = _dot_nt(q, kw) * scale + table
        s = jnp.where(jcol >= pad - qs, s, NEG)
        m = jnp.max(s, axis=1, keepdims=True)
        p = jnp.exp(s - m)
        l = jnp.sum(p, axis=1, keepdims=True)
        o = _dot(p.astype(jnp.bfloat16), vw) / l
        o_ref[pl.ds(qs, tq), :] = o.astype(o_ref.dtype)
        return carry

    lax.fori_loop(0, s_len // tq, q_block, 0, unroll=8)


def _band_bias_table(rel_bias, tq, pad):
    win = pad + tq
    n = win + tq - 1
    n_hi = pad + tq - REL_CLIP
    lo = REL_CLIP - tq + 1
    assert n_hi >= 0 and lo >= 0
    rb = rel_bias.astype(jnp.float32)
    heads = rb.shape[0]
    u = jnp.concatenate([jnp.repeat(rb[:, 2 * REL_CLIP:], n_hi, axis=1), rb[:, lo:2 * REL_CLIP][:, ::-1]], axis=1)
    width = n + 1
    upad = jnp.pad(u, ((0, 0), (0, width + 1 - n)))
    skew = jnp.tile(upad, (1, tq))[:, :tq * width].reshape(heads, tq, width)
    table = skew[:, :, tq - 1:tq - 1 + win]
    r = np.arange(tq)[:, None]
    j = np.arange(win)[None, :]
    kc, qc = j // CHUNK, B_PAST_CHUNKS + r // CHUNK
    in_band = (kc <= qc) & (kc >= qc - B_PAST_CHUNKS)
    return jnp.where(jnp.asarray(in_band)[None], table, NEG)


def _attn_b(proj, rel_bias, tq=256):
    b, s, _ = proj.shape
    pad = B_PAST_CHUNKS * CHUNK
    table = _band_bias_table(rel_bias, tq, pad)
    qi = OFF_B // B_HEAD_DIM
    ki = (OFF_B + B_WIDTH) // B_HEAD_DIM
    vi = (OFF_B + 2 * B_WIDTH) // B_HEAD_DIM

    def hspec(base):
        return pl.BlockSpec((None, s, B_HEAD_DIM), lambda bi, h, base=base: (bi, 0, base + h))

    return pl.pallas_call(
        functools.partial(_attn_b_kernel, tq=tq, pad=pad),
        grid=(b, B_HEADS),
        in_specs=[
            hspec(qi), hspec(ki), hspec(vi),
            pl.BlockSpec((None, tq, pad + tq), lambda bi, h: (h, 0, 0)),
        ],
        out_specs=pl.BlockSpec((None, s, B_HEAD_DIM), lambda bi, h: (bi, 0, h)),
        out_shape=jax.ShapeDtypeStruct((b, s, B_WIDTH), jnp.bfloat16),
        scratch_shapes=[
            pltpu.VMEM((pad + s, B_HEAD_DIM), jnp.bfloat16),
            pltpu.VMEM((pad + s, B_HEAD_DIM), jnp.bfloat16),
        ],
        compiler_params=_params(("parallel", "parallel")),
        name="attn_b",
    )(proj, proj, proj, table)


def _cumsum_rows(x):
    n = x.shape[0]
    row = lax.broadcasted_iota(jnp.int32, x.shape, 0)
    sh = 1
    while sh < n:
        x = x + jnp.where(row >= sh, pltpu.roll(x, sh, 0), 0.0)
        sh *= 2
    return x


def _mlstm_kernel(q_ref, k_ref, v_ref, og_ref, gate_ref, cq_ref, ck_ref, ng_ref, wo_ref, o_ref, wo16_ref,
                  qs_ref, ks_ref, state_ref, gt_ref, *, chunk, halo):
    wo16_ref[...] = wo_ref[...].astype(jnp.bfloat16)
    s_len = q_ref.shape[0]
    n_chunks = s_len // chunk
    pair = pl.program_id(1)
    lane = lax.broadcasted_iota(jnp.int32, (chunk, C_WIN), 1)
    glane = lax.broadcasted_iota(jnp.int32, (chunk, GATE_LANES), 1)
    trow = lax.broadcasted_iota(jnp.int32, (chunk, chunk), 0)
    tcol = lax.broadcasted_iota(jnp.int32, (chunk, chunk), 1)
    causal = tcol <= trow
    windows = ((0, lane < C_HEAD_DIM, C_HEAD_DIM),
               (C_PAIR - C_WIN, lane >= C_WIN - C_HEAD_DIM, 0))

    def conv_chunk(c, carry):
        st = pl.multiple_of(c * chunk, chunk)
        prev_st = pl.multiple_of(jnp.maximum(st - halo, 0), halo)
        keep = jnp.where(c > 0, 1.0, 0.0)
        for src, w_ref, dst, post in ((q_ref, cq_ref, qs_ref, 1.0),
                                      (k_ref, ck_ref, ks_ref, C_HEAD_DIM ** -0.5)):
            cur = src[pl.ds(st, chunk), :].astype(jnp.float32)
            prev = src[pl.ds(prev_st, halo), :].astype(jnp.float32) * keep
            xc = jnp.concatenate([prev, cur], axis=0)
            w = w_ref[...]
            y = w[CONV_K - 1:CONV_K, :] * cur
            for back in range(1, CONV_K):
                y = y + w[CONV_K - 1 - back:CONV_K - back, :] * pltpu.roll(xc, back, 0)[halo:, :]
            y = y * _sigmoid(y) * post
            dst[pl.ds(st, chunk), :] = y.astype(dst.dtype)
        return carry

    lax.fori_loop(0, n_chunks, conv_chunk, 0)
    state_ref[...] = jnp.zeros(state_ref.shape, jnp.float32)

    def step(c, ms):
        st = pl.multiple_of(c * chunk, chunk)
        g = gate_ref[pl.ds(st, chunk), :]
        bcum = _cumsum_rows(_log_sigmoid(g))
        gb = jnp.where(glane < C_HEADS, g, bcum)
        gt_ref[...] = gb.T
        ng = ng_ref[...]
        hn, m_out = [], []
        for a, (w0, valid, spare) in enumerate(windows):
            h = 2 * pair + a
            m = ms[a]
            wcols = slice(w0, w0 + C_WIN)
            zero = jnp.zeros((), jnp.bfloat16)
            q = jnp.where(valid, qs_ref[pl.ds(st, chunk), wcols], zero)
            k = jnp.where(valid, ks_ref[pl.ds(st, chunk), wcols], zero)
            v = jnp.where(valid, v_ref[pl.ds(st, chunk), wcols], zero)
            v_ext = jnp.where(lane == spare, jnp.ones((), v.dtype), v)

            i_row = gt_ref[pl.ds(h, 1), :]
            b_row = gt_ref[pl.ds(C_HEADS + h, 1), :]
            i_col = jnp.sum(jnp.where(glane == h, gb, 0.0), axis=1, keepdims=True)
            b_col = jnp.sum(jnp.where(glane == C_HEADS + h, gb, 0.0), axis=1, keepdims=True)

            d = jnp.where(causal, b_col + (i_row - b_row), NEG)
            inter = b_col + m
            m_t = jnp.maximum(inter, jnp.max(d, axis=1, keepdims=True))
            sw = _dot_nt(q, k) * jnp.exp(d - m_t)
            w_inter = jnp.exp(inter - m_t)
            nd = (_dot(sw.astype(jnp.bfloat16), v_ext)
                  + w_inter * _dot(q, state_ref[a].astype(jnp.bfloat16)))
            den = jnp.sum(jnp.where(lane == spare, nd, 0.0), axis=1, keepdims=True)
            num = jnp.where(valid, nd, 0.0)
            hh = num / jnp.maximum(jnp.abs(den), jnp.exp(-m_t))
            msq = jnp.sum(hh * hh, axis=1, keepdims=True) * (1.0 / C_HEAD_DIM)
            hn.append(hh * lax.rsqrt(msq + 1e-6) * ng[:, wcols])

            b_last = b_col[chunk - 1:chunk, :]
            g_col = b_last - b_col + i_col
            m_new = jnp.maximum(b_last + m, jnp.max(g_col, axis=0, keepdims=True))
            wk = jnp.exp(g_col - m_new)
            decay = jnp.exp(b_last + m - m_new)
            kw_t = (k.astype(jnp.float32) * wk).T.astype(jnp.bfloat16)
            state_ref[a] = decay * state_ref[a] + _dot(kw_t, v_ext)
            m_out.append(m_new)

        mid = C_PAIR - C_WIN
        hn_pair = jnp.concatenate([hn[0][:, :mid], hn[0][:, mid:] + hn[1][:, :C_WIN - mid], hn[1][:, C_WIN - mid:]],
                                  axis=1)
        og = og_ref[pl.ds(st, chunk), :].astype(jnp.float32)
        o_ref[pl.ds(st, chunk), :] = (_sigmoid(og) * hn_pair).astype(o_ref.dtype)
        return tuple(m_out)

    m0 = jnp.zeros((1, 1), jnp.float32)
    lax.fori_loop(0, n_chunks, step, (m0, m0), unroll=2)


def _mlstm(proj, gates, conv_w, norm_g, w_out, li, chunk=256, halo=16):
    b, s, _ = proj.shape
    chunk = min(chunk, s)
    n_pairs = C_HEADS // 2
    base = OFF_C // C_PAIR
    d_in, d = w_out.shape[1:]
    slab = d_in // (b * n_pairs)
    assert slab * b * n_pairs == d_in and slab % 16 == 0

    def pspec(t):
        return pl.BlockSpec((None, s, C_PAIR), lambda bi, p, t=t: (bi, 0, base + n_pairs * t + p))

    return pl.pallas_call(
        functools.partial(_mlstm_kernel, chunk=chunk, halo=halo),
        grid=(b, n_pairs),
        in_specs=[
            pspec(0), pspec(1), pspec(2), pspec(3),
            pl.BlockSpec((None, s, GATE_LANES), lambda bi, p: (bi, 0, 0)),
            pl.BlockSpec((None, CONV_K, C_PAIR), lambda bi, p: (li, 0, p)),
            pl.BlockSpec((None, CONV_K, C_PAIR), lambda bi, p: (li, 0, n_pairs + p)),
            pl.BlockSpec((1, C_PAIR), lambda bi, p: (0, 0)),
            pl.BlockSpec((None, slab, d), lambda bi, p: (li, bi * n_pairs + p, 0)),
        ],
        out_specs=[pl.BlockSpec((None, s, C_PAIR), lambda bi, p: (bi, 0, p)),
                   pl.BlockSpec((slab, d), lambda bi, p: (bi * n_pairs + p, 0))],
        out_shape=[jax.ShapeDtypeStruct((b, s, C_WIDTH), jnp.bfloat16),
                   jax.ShapeDtypeStruct((d_in, d), jnp.bfloat16)],
        scratch_shapes=[
            pltpu.VMEM((s, C_PAIR), jnp.bfloat16),
            pltpu.VMEM((s, C_PAIR), jnp.bfloat16),
            pltpu.VMEM((2, C_WIN, C_WIN), jnp.float32),
            pltpu.VMEM((GATE_LANES, chunk), jnp.float32),
        ],
        compiler_params=_params(("parallel", "parallel")),
        name="mlstm",
    )(proj, proj, proj, proj, gates, conv_w, conv_w, norm_g, w_out)


def _out_proj_ln_kernel(ya_ref, yb_ref, yc_ref, w16_ref, x_ref, g_ref, b_ref, o32_ref, o16_ref, *, alpha):
    half = x_ref.shape[0] // 2
    for r in range(2):
        rows = slice(r * half, (r + 1) * half)
        mixed = jnp.concatenate([ya_ref[rows, :], yb_ref[rows, :], yc_ref[rows, :]], axis=1)
        y = _layer_norm(alpha * x_ref[rows, :] + _dot(mixed, w16_ref[...]), g_ref[...], b_ref[...])
        o32_ref[rows, :] = y
        o16_ref[rows, :] = y.astype(o16_ref.dtype)


def _out_proj_ln(ya, yb, yc, wo16, x, g, b, li, alpha, tm=512):
    m, d = x.shape
    tm = min(tm, m)
    row = lambda w: pl.BlockSpec((tm, w), lambda i: (i, 0))
    vec = pl.BlockSpec((None, 1, d), lambda i: (li, 0, 0))
    return pl.pallas_call(
        functools.partial(_out_proj_ln_kernel, alpha=alpha),
        grid=(m // tm,),
        in_specs=[row(ya.shape[1]), row(yb.shape[1]), row(yc.shape[1]),
                  pl.BlockSpec(wo16.shape, lambda i: (0, 0), pipeline_mode=pl.Buffered(1)),
                  row(d), vec, vec],
        out_specs=[row(d), row(d)],
        out_shape=[jax.ShapeDtypeStruct((m, d), jnp.float32),
                   jax.ShapeDtypeStruct((m, d), jnp.bfloat16)],
        compiler_params=_params(("parallel",)),
        name="out_proj_ln",
    )(ya, yb, yc, wo16, x, g, b)


def _ffn_up_kernel(x_ref, wg_ref, wu_ref, o_ref, wg16_ref, wu16_ref):
    @pl.when(pl.program_id(1) == 0)
    def _():
        wg16_ref[...] = wg_ref[...].astype(jnp.bfloat16)
        wu16_ref[...] = wu_ref[...].astype(jnp.bfloat16)

    x = x_ref[...]
    a = _dot(x, wg16_ref[...])
    u = _dot(x, wu16_ref[...])
    o_ref[...] = (a * _sigmoid(a) * u).astype(o_ref.dtype)


def _ffn_up(xb, w_gate, w_up, li, tm=2048, tn=512):
    m, k = xb.shape
    n = w_gate.shape[2]
    tm, tn = min(tm, m), min(tn, n)
    return pl.pallas_call(
        _ffn_up_kernel,
        grid=(n // tn, m // tm),
        in_specs=[
            pl.BlockSpec((tm, k), lambda j, i: (i, 0)),
            pl.BlockSpec((None, k, tn), lambda j, i: (li, 0, j)),
            pl.BlockSpec((None, k, tn), lambda j, i: (li, 0, j)),
        ],
        out_specs=pl.BlockSpec((tm, tn), lambda j, i: (i, j)),
        out_shape=jax.ShapeDtypeStruct((m, n), jnp.bfloat16),
        scratch_shapes=[pltpu.VMEM((k, tn), jnp.bfloat16), pltpu.VMEM((k, tn), jnp.bfloat16)],
        compiler_params=_params(("arbitrary", "arbitrary")),
        name="ffn_up",
    )(xb, w_gate, w_up)


def _ffn_down_ln_kernel(h_ref, w_ref, x_ref, g_ref, b_ref, *rest, alpha, next_gates):
    if next_gates:
        wg_ref, gb_ref, o32_ref, o16_ref, gate_ref = rest
    else:
        (o32_ref,) = rest
    kk = pl.program_id(1)
    last = pl.num_programs(1) - 1

    @pl.when(kk == 0)
    def _():
        o32_ref[...] = _dot(h_ref[...], w_ref[...])

    @pl.when((kk > 0) & (kk < last))
    def _():
        o32_ref[...] += _dot(h_ref[...], w_ref[...])

    @pl.when(kk == last)
    def _():
        z = alpha * x_ref[...] + (o32_ref[...] + _dot(h_ref[...], w_ref[...]))
        y = _layer_norm(z, g_ref[...], b_ref[...])
        o32_ref[...] = y
        if next_gates:
            y16 = y.astype(jnp.bfloat16)
            o16_ref[...] = y16
            gate_ref[...] = _gate_proj(y16, wg_ref, gb_ref)


def _ffn_down_ln(hid, wd16, x, g, b, li, alpha, w_in=None, gb_next=None, tm=512, tk=2816):
    m, f = hid.shape
    d = wd16.shape[1]
    tm, tk = min(tm, m), min(tk, f)
    assert f // tk >= 2
    next_gates = w_in is not None
    vec = pl.BlockSpec((None, 1, d), lambda i, kk: (li, 0, 0))
    row = pl.BlockSpec((tm, d), lambda i, kk: (i, 0))
    in_specs = [pl.BlockSpec((tm, tk), lambda i, kk: (i, kk)),
                pl.BlockSpec((tk, d), lambda i, kk: (kk, 0)),
                row, vec, vec]
    out_specs = [row]
    out_shape = [jax.ShapeDtypeStruct((m, d), jnp.float32)]
    args = [hid, wd16, x, g, b]
    if next_gates:
        in_specs += [_gate_weight_spec(li + 1, d), pl.BlockSpec((1, GATE_LANES), lambda i, kk: (0, 0))]
        out_specs += [row, pl.BlockSpec((tm, GATE_LANES), lambda i, kk: (i, 0))]
        out_shape += [jax.ShapeDtypeStruct((m, d), jnp.bfloat16),
                      jax.ShapeDtypeStruct((m, GATE_LANES), jnp.float32)]
        args += [w_in, gb_next]
    return pl.pallas_call(
        functools.partial(_ffn_down_ln_kernel, alpha=alpha, next_gates=next_gates),
        grid=(m // tm, f // tk),
        in_specs=in_specs,
        out_specs=out_specs,
        out_shape=out_shape,
        compiler_params=_params(("parallel", "arbitrary")),
        name="ffn_down_ln",
    )(*args)


def _gate_bias_tile(gate_bias, li):
    return jnp.pad(gate_bias[li], (0, GATE_LANES - 2 * C_HEADS)).reshape(1, GATE_LANES)


def kernel(x, w_in, gate_bias, conv_w, lam, subln_a, norm_c, rel_bias, w_out, ln1_g, ln1_b,
           w_gate, w_up, w_down, ln2_g, ln2_b):
    bsz, s_len, d = x.shape
    depth = w_in.shape[0]
    alpha = (2.0 * depth) ** 0.25
    m = bsz * s_len
    x32 = x.reshape(m, d)
    ln1_g, ln1_b, ln2_g, ln2_b = (p.reshape(depth, 1, d) for p in (ln1_g, ln1_b, ln2_g, ln2_b))
    w_in_t = jnp.swapaxes(w_in, 1, 2)
    x16, gates = _cast_gate(x32, w_in_t, _gate_bias_tile(gate_bias, 0), 0)
    for li in range(depth):
        proj = _in_proj(x16, w_in_t, li).reshape(bsz, s_len, OFF_G)
        gates = gates.reshape(bsz, s_len, GATE_LANES)
        ya, wd16 = _attn_a(proj, lam, subln_a, w_down, li)
        yb = _attn_b(proj, rel_bias[li])
        norm_g = jnp.tile(norm_c[li], 2).reshape(1, C_PAIR)
        yc, wo16 = _mlstm(proj, gates, conv_w, norm_g, w_out, li)
        x32, x16 = _out_proj_ln(ya.reshape(m, A_WIDTH), yb.reshape(m, B_WIDTH), yc.reshape(m, C_WIDTH),
                                wo16, x32, ln1_g, ln1_b, li, alpha)
        hid = _ffn_up(x16, w_gate, w_up, li)
        if li + 1 < depth:
            x32, x16, gates = _ffn_down_ln(hid, wd16, x32, ln2_g, ln2_b, li, alpha,
                                           w_in_t, _gate_bias_tile(gate_bias, li + 1))
        else:
            (x32,) = _ffn_down_ln(hid, wd16, x32, ln2_g, ln2_b, li, alpha)
    return x32.reshape(bsz, s_len, d)
```

```python
import functools
import math

import numpy as np
import jax
import jax.numpy as jnp
from jax import lax
from jax.experimental import pallas as pl
from jax.experimental.pallas import tpu as pltpu

CHUNK = 64
A_HEADS = 4
A_HEAD_DIM = 128
A_HALF = 64
B_HEADS = 6
B_HEAD_DIM = 128
B_PAST_CHUNKS = 8
REL_CLIP = 256
C_HEADS = 4
C_HEAD_DIM = 192
C_PAIR = 2 * C_HEAD_DIM
C_WIN = 256
CONV_K = 4
NEG = -1e30
GATE_LANES = 128

A_WIDTH = A_HEADS * A_HEAD_DIM
B_WIDTH = B_HEADS * B_HEAD_DIM
C_WIDTH = C_HEADS * C_HEAD_DIM
OFF_A = 0
OFF_B = 3 * A_WIDTH
OFF_C = OFF_B + 3 * B_WIDTH
OFF_G = OFF_C + 4 * C_WIDTH
PROJ_TILE = 768

VMEM_LIMIT = 56 * 1024 * 1024

_NT = (((1,), (1,)), ((), ()))


def _params(sem, flags=None):
    return pltpu.CompilerParams(dimension_semantics=sem, vmem_limit_bytes=VMEM_LIMIT, flags=flags)


def _dot(a, b):
    return jnp.dot(a, b, preferred_element_type=jnp.float32)


def _dot_nt(a, b):
    return lax.dot_general(a, b, _NT, preferred_element_type=jnp.float32)


def _log_sigmoid(x):
    return jnp.minimum(x, 0.0) - jnp.log(1.0 + jnp.exp(-jnp.abs(x)))


def _sigmoid(x):
    return 1.0 / (1.0 + jnp.exp(-x))


def _layer_norm(z, g, b):
    mu = jnp.mean(z, axis=1, keepdims=True)
    zc = z - mu
    var = jnp.mean(zc * zc, axis=1, keepdims=True)
    return zc * lax.rsqrt(var + 1e-5) * g + b


N_GATES = 2 * C_HEADS


def _gate_weight_spec(li, d):
    assert OFF_G % N_GATES == 0
    return pl.BlockSpec((None, N_GATES, d), lambda *_: (li, OFF_G // N_GATES, 0))


def _gate_proj(x16, wg_ref, gb_ref):
    wg = wg_ref[...].astype(jnp.bfloat16)
    wg = jnp.concatenate([wg, jnp.zeros((GATE_LANES - N_GATES, wg.shape[1]), wg.dtype)], axis=0)
    return _dot_nt(x16, wg) + gb_ref[...]


def _cast_gate_kernel(x_ref, wg_ref, gb_ref, x16_ref, g_ref):
    x16 = x_ref[...].astype(jnp.bfloat16)
    x16_ref[...] = x16
    g_ref[...] = _gate_proj(x16, wg_ref, gb_ref)


def _cast_gate(x, w_in_t, gb, li, tm=1024):
    m, d = x.shape
    tm = min(tm, m)
    return pl.pallas_call(
        _cast_gate_kernel,
        grid=(m // tm,),
        in_specs=[
            pl.BlockSpec((tm, d), lambda i: (i, 0)),
            _gate_weight_spec(li, d),
            pl.BlockSpec((1, GATE_LANES), lambda i: (0, 0)),
        ],
        out_specs=[pl.BlockSpec((tm, d), lambda i: (i, 0)),
                   pl.BlockSpec((tm, GATE_LANES), lambda i: (i, 0))],
        out_shape=[jax.ShapeDtypeStruct((m, d), jnp.bfloat16),
                   jax.ShapeDtypeStruct((m, GATE_LANES), jnp.float32)],
        compiler_params=_params(("parallel",)),
        name="cast_gate",
    )(x, w_in_t, gb)


def _in_proj_kernel(x_ref, w_ref, o_ref, w16_ref):
    @pl.when(pl.program_id(1) == 0)
    def _():
        w16_ref[...] = w_ref[...].astype(jnp.bfloat16)

    o_ref[...] = _dot_nt(x_ref[...], w16_ref[...]).astype(o_ref.dtype)


def _in_proj(xb, w_in_t, li, tm=2048, tn=PROJ_TILE):
    m, k = xb.shape
    tm = min(tm, m)
    n = OFF_G
    return pl.pallas_call(
        _in_proj_kernel,
        grid=(n // tn, m // tm),
        in_specs=[
            pl.BlockSpec((tm, k), lambda j, i: (i, 0)),
            pl.BlockSpec((None, tn, k), lambda j, i: (li, j, 0)),
        ],
        out_specs=pl.BlockSpec((tm, tn), lambda j, i: (i, j)),
        out_shape=jax.ShapeDtypeStruct((m, n), jnp.bfloat16),
        scratch_shapes=[pltpu.VMEM((tn, k), jnp.bfloat16)],
        compiler_params=_params(("arbitrary", "arbitrary")),
        name="in_proj",
    )(xb, w_in_t)


def _attn_a_kernel(q_ref, k_ref, v_ref, lam_ref, g_ref, slope_ref, wd_ref, o_ref, wd16_ref,
                   qt1_ref, qt2_ref, vt_ref, *, tq, lam_init, ahead):
    wd16_ref[...] = wd_ref[...].astype(jnp.bfloat16)
    s_len = q_ref.shape[0]
    nq = s_len // tq
    slope = slope_ref[0:1, 0:1]
    lf = lam_ref[...]
    lam_full = (jnp.exp(jnp.sum(lf[0:1] * lf[1:2], axis=1, keepdims=True))
                - jnp.exp(jnp.sum(lf[2:3] * lf[3:4], axis=1, keepdims=True)) + lam_init)
    feat = lax.broadcasted_iota(jnp.int32, (A_HEAD_DIM, tq), 0)
    first_half = feat < A_HALF
    for i in range(nq):
        rows = slice(i * tq, (i + 1) * tq)
        qt = (q_ref[rows, :].astype(jnp.float32) * (A_HALF ** -0.5)).T
        qt1_ref[:, rows] = jnp.where(first_half, qt, 0.0).astype(jnp.bfloat16)
        qt2_ref[:, rows] = jnp.where(first_half, 0.0, qt).astype(jnp.bfloat16)
        vt_ref[:, rows] = v_ref[rows, :].astype(jnp.float32).T.astype(jnp.bfloat16)

    key = lax.broadcasted_iota(jnp.int32, (tq, tq), 0)
    qry = lax.broadcasted_iota(jnp.int32, (tq, tq), 1)
    rel = (qry - key).astype(jnp.float32)
    bias_off = -slope * rel
    bias_diag = -slope * jnp.abs(rel)
    diag_ok = (key // CHUNK) <= (qry // CHUNK)

    def update2(t1, t2, shift_c, vt, st1, st2):
        ts, sts = (t1, t2), (st1, st2)
        m_new = [jnp.maximum(st[0], jnp.max(t, axis=0, keepdims=True) + shift_c) for t, st in zip(ts, sts)]
        alpha = [jnp.exp(st[0] - mn) for st, mn in zip(sts, m_new)]
        p = [jnp.exp(t - (mn - shift_c)) for t, mn in zip(ts, m_new)]
        l_new = [a * st[1] + jnp.sum(pp, axis=0, keepdims=True) for a, st, pp in zip(alpha, sts, p)]
        pv = [_dot(vt, pp.astype(jnp.bfloat16)) for pp in p]
        acc = [a * st[2] + x for a, st, x in zip(alpha, sts, pv)]
        return (m_new[0], l_new[0], acc[0]), (m_new[1], l_new[1], acc[1])

    for i in range(nq):
        cols = slice(i * tq, (i + 1) * tq)
        qt1 = qt1_ref[:, cols]
        qt2 = qt2_ref[:, cols]
        st1 = (jnp.full((1, tq), NEG, jnp.float32), jnp.zeros((1, tq), jnp.float32),
               jnp.zeros((A_HEAD_DIM, tq), jnp.float32))
        st2 = st1

        def scores(j):
            k = k_ref[j * tq:(j + 1) * tq, :]
            if j < i:
                return _dot(k, qt1) + bias_off, _dot(k, qt2) + bias_off
            return (jnp.where(diag_ok, _dot(k, qt1) + bias_diag, NEG),
                    jnp.where(diag_ok, _dot(k, qt2) + bias_diag, NEG))

        pending = [scores(j) for j in range(min(ahead, i + 1))]
        for j in range(i + 1):
            t1, t2 = pending.pop(0)
            if j + ahead <= i:
                pending.append(scores(j + ahead))
            if j < i:
                shift_c = -slope * float((i - j) * tq)
            else:
                shift_c = jnp.zeros((1, 1), jnp.float32)
            vt = vt_ref[:, j * tq:(j + 1) * tq]
            st1, st2 = update2(t1, t2, shift_c, vt, st1, st2)
        ot = st1[2] * (1.0 / st1[1]) - (lam_full / st2[1]) * st2[2]
        o = ot.T
        ms = jnp.mean(o * o, axis=1, keepdims=True)
        o = o * lax.rsqrt(ms + 1e-6) * g_ref[...] * (1.0 - lam_init)
        o_ref[cols, :] = o.astype(o_ref.dtype)


def _attn_a(proj, lam, subln, w_down, li, tq=256, ahead=1):
    b, s, _ = proj.shape
    tq = min(tq, s)
    f, d = w_down.shape[1:]
    slab = f // (b * A_HEADS)
    assert slab * b * A_HEADS == f and slab % 16 == 0
    lam_init = 0.8 - 0.6 * math.exp(-0.3 * li)
    slopes = np.asarray([2.0 ** (-8.0 * (h + 1) / A_HEADS) for h in range(A_HEADS)], np.float32)
    slopes = jnp.asarray(np.broadcast_to(slopes[:, None, None], (A_HEADS, 1, 128)))
    qi = OFF_A // A_HEAD_DIM
    ki = (OFF_A + A_WIDTH) // A_HEAD_DIM
    vi = (OFF_A + 2 * A_WIDTH) // A_HEAD_DIM

    def hspec(base):
        return pl.BlockSpec((None, s, A_HEAD_DIM), lambda bi, h, base=base: (bi, 0, base + h))

    return pl.pallas_call(
        functools.partial(_attn_a_kernel, tq=tq, lam_init=lam_init, ahead=ahead),
        grid=(b, A_HEADS),
        in_specs=[
            hspec(qi), hspec(ki), hspec(vi),
            pl.BlockSpec((None, 4, A_HALF), lambda bi, h: (li, 0, 0)),
            pl.BlockSpec((None, 1, A_HEAD_DIM), lambda bi, h: (li, 0, 0)),
            pl.BlockSpec((None, 1, 128), lambda bi, h: (h, 0, 0)),
            pl.BlockSpec((None, slab, d), lambda bi, h: (li, bi * A_HEADS + h, 0)),
        ],
        out_specs=[pl.BlockSpec((None, s, A_HEAD_DIM), lambda bi, h: (bi, 0, h)),
                   pl.BlockSpec((slab, d), lambda bi, h: (bi * A_HEADS + h, 0))],
        out_shape=[jax.ShapeDtypeStruct((b, s, A_WIDTH), jnp.bfloat16),
                   jax.ShapeDtypeStruct((f, d), jnp.bfloat16)],
        scratch_shapes=[
            pltpu.VMEM((A_HEAD_DIM, s), jnp.bfloat16),
            pltpu.VMEM((A_HEAD_DIM, s), jnp.bfloat16),
            pltpu.VMEM((A_HEAD_DIM, s), jnp.bfloat16),
        ],
        compiler_params=_params(("parallel", "parallel")),
        name="attn_a",
    )(proj, proj, proj, lam, subln.reshape(-1, 1, A_HEAD_DIM), slopes, w_down)


def _attn_b_kernel(q_ref, k_ref, v_ref, u_ref, o_ref, kpad, vpad, table_ref, *, tq, pad):
    s_len = q_ref.shape[0]
    win = pad + tq
    width = u_ref.shape[1]

    @pl.when(pl.program_id(1) == 0)
    def _():
        rolled = pltpu.roll(jnp.broadcast_to(u_ref[...], (tq, width)), width - (tq - 1), 1,
                            stride=1, stride_axis=0)
        r = lax.broadcasted_iota(jnp.int32, (tq, win), 0)
        j = lax.broadcasted_iota(jnp.int32, (tq, win), 1)
        kc, qc = j // CHUNK, B_PAST_CHUNKS + r // CHUNK
        table_ref[...] = jnp.where((kc <= qc) & (kc >= qc - B_PAST_CHUNKS), rolled[:, :win], NEG)

    kpad[0:pad, :] = jnp.zeros((pad, B_HEAD_DIM), kpad.dtype)
    vpad[0:pad, :] = jnp.zeros((pad, B_HEAD_DIM), vpad.dtype)
    kpad[pad:pad + s_len, :] = k_ref[...]
    vpad[pad:pad + s_len, :] = v_ref[...]
    table = table_ref[...]
    jcol = lax.broadcasted_iota(jnp.int32, (tq, win), 1)
    scale = B_HEAD_DIM ** -0.5
    nq = s_len // tq

    def scores(i):
        qs = i * tq
        s = _dot_nt(q_ref[qs:qs + tq, :], kpad[qs:qs + win, :]) * scale + table
        if qs < pad:
            s = jnp.where(jcol >= pad - qs, s, NEG)
        return s

    nxt = scores(0)
    for i in range(nq):
        s = nxt
        if i + 1 < nq:
            nxt = scores(i + 1)
        qs = i * tq
        m = jnp.max(s, axis=1, keepdims=True)
        p = jnp.exp(s - m)
        l = jnp.sum(p, axis=1, keepdims=True)
        o = _dot(p.astype(jnp.bfloat16), vpad[qs:qs + win, :]) / l
        o_ref[qs:qs + tq, :] = o.astype(o_ref.dtype)


def _band_bias_vector(rel_bias, tq, pad):
    n = pad + 2 * tq - 1
    n_hi = pad + tq - REL_CLIP
    lo = REL_CLIP - tq + 1
    assert n_hi >= 0 and lo >= 0
    rb = rel_bias.astype(jnp.float32)
    u = jnp.concatenate([jnp.repeat(rb[:, 2 * REL_CLIP:], n_hi, axis=1), rb[:, lo:2 * REL_CLIP][:, ::-1]], axis=1)
    assert u.shape[1] == n
    return jnp.pad(u, ((0, 0), (0, 1)))[:, None, :]


def _attn_b(proj, rel_bias, tq=256):
    b, s, _ = proj.shape
    pad = B_PAST_CHUNKS * CHUNK
    width = pad + 2 * tq
    assert width & (width - 1) == 0
    u = _band_bias_vector(rel_bias, tq, pad)
    qi = OFF_B // B_HEAD_DIM
    ki = (OFF_B + B_WIDTH) // B_HEAD_DIM
    vi = (OFF_B + 2 * B_WIDTH) // B_HEAD_DIM

    def hspec(base):
        return pl.BlockSpec((None, s, B_HEAD_DIM), lambda h, bi, base=base: (bi, 0, base + h))

    return pl.pallas_call(
        functools.partial(_attn_b_kernel, tq=tq, pad=pad),
        grid=(B_HEADS, b),
        in_specs=[
            hspec(qi), hspec(ki), hspec(vi),
            pl.BlockSpec((None, 1, width), lambda h, bi: (h, 0, 0)),
        ],
        out_specs=pl.BlockSpec((None, s, B_HEAD_DIM), lambda h, bi: (bi, 0, h)),
        out_shape=jax.ShapeDtypeStruct((b, s, B_WIDTH), jnp.bfloat16),
        scratch_shapes=[
            pltpu.VMEM((pad + s, B_HEAD_DIM), jnp.bfloat16),
            pltpu.VMEM((pad + s, B_HEAD_DIM), jnp.bfloat16),
            pltpu.VMEM((tq, pad + tq), jnp.float32),
        ],
        compiler_params=_params(("arbitrary", "arbitrary")),
        name="attn_b",
    )(proj, proj, proj, u)


def _cumsum_rows(x):
    n = x.shape[0]
    row = lax.broadcasted_iota(jnp.int32, x.shape, 0)
    sh = 1
    while sh < n:
        x = x + jnp.where(row >= sh, pltpu.roll(x, sh, 0), 0.0)
        sh *= 2
    return x


def _mlstm_kernel(q_ref, k_ref, v_ref, og_ref, gate_ref, cq_ref, ck_ref, ng_ref, wo_ref, o_ref, wo16_ref,
                  qs_ref, ks_ref, state_ref, gt_ref, *, chunk, halo):
    wo16_ref[...] = wo_ref[...].astype(jnp.bfloat16)
    s_len = q_ref.shape[0]
    n_chunks = s_len // chunk
    pair = pl.program_id(1)
    lane = lax.broadcasted_iota(jnp.int32, (chunk, C_WIN), 1)
    glane = lax.broadcasted_iota(jnp.int32, (chunk, GATE_LANES), 1)
    trow = lax.broadcasted_iota(jnp.int32, (chunk, chunk), 0)
    tcol = lax.broadcasted_iota(jnp.int32, (chunk, chunk), 1)
    causal = tcol <= trow
    windows = ((0, lane < C_HEAD_DIM, C_HEAD_DIM),
               (C_PAIR - C_WIN, lane >= C_WIN - C_HEAD_DIM, 0))

    def conv_chunk(c, carry):
        st = pl.multiple_of(c * chunk, chunk)
        prev_st = pl.multiple_of(jnp.maximum(st - halo, 0), halo)
        keep = jnp.where(c > 0, 1.0, 0.0)
        for src, w_ref, dst, post in ((q_ref, cq_ref, qs_ref, 1.0),
                                      (k_ref, ck_ref, ks_ref, C_HEAD_DIM ** -0.5)):
            cur = src[pl.ds(st, chunk), :].astype(jnp.float32)
            prev = src[pl.ds(prev_st, halo), :].astype(jnp.float32) * keep
            xc = jnp.concatenate([prev, cur], axis=0)
            w = w_ref[...]
            y = w[CONV_K - 1:CONV_K, :] * cur
            for back in range(1, CONV_K):
                y = y + w[CONV_K - 1 - back:CONV_K - back, :] * pltpu.roll(xc, back, 0)[halo:, :]
            y = y * _sigmoid(y) * post
            dst[pl.ds(st, chunk), :] = y.astype(dst.dtype)
        return carry

    lax.fori_loop(0, n_chunks, conv_chunk, 0)
    state_ref[...] = jnp.zeros(state_ref.shape, jnp.float32)

    def step(c, ms):
        st = pl.multiple_of(c * chunk, chunk)
        g = gate_ref[pl.ds(st, chunk), :]
        bcum = _cumsum_rows(_log_sigmoid(g))
        gb = jnp.where(glane < C_HEADS, g, bcum)
        gt_ref[...] = gb.T
        ng = ng_ref[...]
        zero = jnp.zeros((), jnp.bfloat16)
        ops, qk, qc = [], [], []
        for a, (w0, valid, spare) in enumerate(windows):
            wcols = slice(w0, w0 + C_WIN)
            q = jnp.where(valid, qs_ref[pl.ds(st, chunk), wcols], zero)
            k = jnp.where(valid, ks_ref[pl.ds(st, chunk), wcols], zero)
            v = jnp.where(valid, v_ref[pl.ds(st, chunk), wcols], zero)
            v_ext = jnp.where(lane == spare, jnp.ones((), v.dtype), v)
            ops.append((q, k, v_ext))
            qk.append(_dot_nt(q, k))
            qc.append(_dot(q, state_ref[a].astype(jnp.bfloat16)))

        gate, m_out = [], []
        for a in range(2):
            h = 2 * pair + a
            m = ms[a]
            i_row = gt_ref[pl.ds(h, 1), :]
            b_row = gt_ref[pl.ds(C_HEADS + h, 1), :]
            i_col = jnp.sum(jnp.where(glane == h, gb, 0.0), axis=1, keepdims=True)
            b_col = jnp.sum(jnp.where(glane == C_HEADS + h, gb, 0.0), axis=1, keepdims=True)
            d = jnp.where(causal, b_col + (i_row - b_row), NEG)
            inter = b_col + m
            m_t = jnp.maximum(inter, jnp.max(d, axis=1, keepdims=True))
            b_last = b_col[chunk - 1:chunk, :]
            g_col = b_last - b_col + i_col
            m_new = jnp.maximum(b_last + m, jnp.max(g_col, axis=0, keepdims=True))
            gate.append((d, inter, m_t, jnp.exp(g_col - m_new), jnp.exp(b_last + m - m_new)))
            m_out.append(m_new)

        nds = []
        for a in range(2):
            d, inter, m_t = gate[a][:3]
            sw = qk[a] * jnp.exp(d - m_t)
            nds.append(_dot(sw.astype(jnp.bfloat16), ops[a][2]) + jnp.exp(inter - m_t) * qc[a])

        for a in range(2):
            k, v_ext = ops[a][1:]
            wk, decay = gate[a][3:]
            kw_t = (k.astype(jnp.float32) * wk).T.astype(jnp.bfloat16)
            state_ref[a] = decay * state_ref[a] + _dot(kw_t, v_ext)

        hn = []
        for a, (w0, valid, spare) in enumerate(windows):
            nd, m_t = nds[a], gate[a][2]
            den = jnp.sum(jnp.where(lane == spare, nd, 0.0), axis=1, keepdims=True)
            num = jnp.where(valid, nd, 0.0)
            hh = num / jnp.maximum(jnp.abs(den), jnp.exp(-m_t))
            msq = jnp.sum(hh * hh, axis=1, keepdims=True) * (1.0 / C_HEAD_DIM)
            hn.append(hh * lax.rsqrt(msq + 1e-6) * ng[:, w0:w0 + C_WIN])

        mid = C_PAIR - C_WIN
        hn_pair = jnp.concatenate([hn[0][:, :mid], hn[0][:, mid:] + hn[1][:, :C_WIN - mid], hn[1][:, C_WIN - mid:]],
                                  axis=1)
        og = og_ref[pl.ds(st, chunk), :].astype(jnp.float32)
        o_ref[pl.ds(st, chunk), :] = (_sigmoid(og) * hn_pair).astype(o_ref.dtype)
        return tuple(m_out)

    m0 = jnp.zeros((1, 1), jnp.float32)
    lax.fori_loop(0, n_chunks, step, (m0, m0), unroll=2)


def _mlstm(proj, gates, conv_w, norm_g, w_out, li, chunk=256, halo=16):
    b, s, _ = proj.shape
    chunk = min(chunk, s)
    n_pairs = C_HEADS // 2
    base = OFF_C // C_PAIR
    d_in, d = w_out.shape[1:]
    slab = d_in // (b * n_pairs)
    assert slab * b * n_pairs == d_in and slab % 16 == 0

    def pspec(t):
        return pl.BlockSpec((None, s, C_PAIR), lambda bi, p, t=t: (bi, 0, base + n_pairs * t + p))

    return pl.pallas_call(
        functools.partial(_mlstm_kernel, chunk=chunk, halo=halo),
        grid=(b, n_pairs),
        in_specs=[
            pspec(0), pspec(1), pspec(2), pspec(3),
            pl.BlockSpec((None, s, GATE_LANES), lambda bi, p: (bi, 0, 0)),
            pl.BlockSpec((None, CONV_K, C_PAIR), lambda bi, p: (li, 0, p)),
            pl.BlockSpec((None, CONV_K, C_PAIR), lambda bi, p: (li, 0, n_pairs + p)),
            pl.BlockSpec((1, C_PAIR), lambda bi, p: (0, 0)),
            pl.BlockSpec((None, slab, d), lambda bi, p: (li, bi * n_pairs + p, 0)),
        ],
        out_specs=[pl.BlockSpec((None, s, C_PAIR), lambda bi, p: (bi, 0, p)),
                   pl.BlockSpec((slab, d), lambda bi, p: (bi * n_pairs + p, 0))],
        out_shape=[jax.ShapeDtypeStruct((b, s, C_WIDTH), jnp.bfloat16),
                   jax.ShapeDtypeStruct((d_in, d), jnp.bfloat16)],
        scratch_shapes=[
            pltpu.VMEM((s, C_PAIR), jnp.bfloat16),
            pltpu.VMEM((s, C_PAIR), jnp.bfloat16),
            pltpu.VMEM((2, C_WIN, C_WIN), jnp.float32),
            pltpu.VMEM((GATE_LANES, chunk), jnp.float32),
        ],
        compiler_params=_params(("parallel", "parallel")),
        name="mlstm",
    )(proj, proj, proj, proj, gates, conv_w, conv_w, norm_g, w_out)


def _out_proj_ln_kernel(ya_ref, yb_ref, yc_ref, w16_ref, x_ref, g_ref, b_ref, o32_ref, o16_ref, *, alpha):
    half = x_ref.shape[0] // 2
    for r in range(2):
        rows = slice(r * half, (r + 1) * half)
        mixed = jnp.concatenate([ya_ref[rows, :], yb_ref[rows, :], yc_ref[rows, :]], axis=1)
        y = _layer_norm(alpha * x_ref[rows, :] + _dot(mixed, w16_ref[...]), g_ref[...], b_ref[...])
        o32_ref[rows, :] = y
        o16_ref[rows, :] = y.astype(o16_ref.dtype)


def _out_proj_ln(ya, yb, yc, wo16, x, g, b, li, alpha, tm=512):
    m, d = x.shape
    tm = min(tm, m)
    row = lambda w: pl.BlockSpec((tm, w), lambda i: (i, 0))
    vec = pl.BlockSpec((None, 1, d), lambda i: (li, 0, 0))
    return pl.pallas_call(
        functools.partial(_out_proj_ln_kernel, alpha=alpha),
        grid=(m // tm,),
        in_specs=[row(ya.shape[1]), row(yb.shape[1]), row(yc.shape[1]),
                  pl.BlockSpec(wo16.shape, lambda i: (0, 0), pipeline_mode=pl.Buffered(1)),
                  row(d), vec, vec],
        out_specs=[row(d), row(d)],
        out_shape=[jax.ShapeDtypeStruct((m, d), jnp.float32),
                   jax.ShapeDtypeStruct((m, d), jnp.bfloat16)],
        compiler_params=_params(("parallel",)),
        name="out_proj_ln",
    )(ya, yb, yc, wo16, x, g, b)


def _ffn_up_kernel(x_ref, wg_ref, wu_ref, o_ref, wg16_ref, wu16_ref):
    @pl.when(pl.program_id(1) == 0)
    def _():
        wg16_ref[...] = wg_ref[...].astype(jnp.bfloat16)
        wu16_ref[...] = wu_ref[...].astype(jnp.bfloat16)

    x = x_ref[...]
    a = _dot(x, wg16_ref[...])
    u = _dot(x, wu16_ref[...])
    o_ref[...] = (a * _sigmoid(a) * u).astype(o_ref.dtype)


def _ffn_up(xb, w_gate, w_up, li, tm=2048, tn=512):
    m, k = xb.shape
    n = w_gate.shape[2]
    tm, tn = min(tm, m), min(tn, n)
    return pl.pallas_call(
        _ffn_up_kernel,
        grid=(n // tn, m // tm),
        in_specs=[
            pl.BlockSpec((tm, k), lambda j, i: (i, 0)),
            pl.BlockSpec((None, k, tn), lambda j, i: (li, 0, j)),
            pl.BlockSpec((None, k, tn), lambda j, i: (li, 0, j)),
        ],
        out_specs=pl.BlockSpec((tm, tn), lambda j, i: (i, j)),
        out_shape=jax.ShapeDtypeStruct((m, n), jnp.bfloat16),
        scratch_shapes=[pltpu.VMEM((k, tn), jnp.bfloat16), pltpu.VMEM((k, tn), jnp.bfloat16)],
        compiler_params=_params(("arbitrary", "arbitrary")),
        name="ffn_up",
    )(xb, w_gate, w_up)


def _ffn_down_ln_kernel(h_ref, w_ref, x_ref, g_ref, b_ref, *rest, alpha, next_gates):
    if next_gates:
        wg_ref, gb_ref, o32_ref, o16_ref, gate_ref = rest
    else:
        (o32_ref,) = rest
    kk = pl.program_id(1)

    @pl.when(kk == 0)
    def _():
        o32_ref[...] = jnp.zeros(o32_ref.shape, jnp.float32)

    o32_ref[...] += _dot(h_ref[...], w_ref[...])

    @pl.when(kk == pl.num_programs(1) - 1)
    def _():
        y = _layer_norm(alpha * x_ref[...] + o32_ref[...], g_ref[...], b_ref[...])
        o32_ref[...] = y
        if next_gates:
            y16 = y.astype(jnp.bfloat16)
            o16_ref[...] = y16
            gate_ref[...] = _gate_proj(y16, wg_ref, gb_ref)


def _ffn_down_ln(hid, wd16, x, g, b, li, alpha, w_in=None, gb_next=None, tm=512, tk=2816):
    m, f = hid.shape
    d = wd16.shape[1]
    tm, tk = min(tm, m), min(tk, f)
    next_gates = w_in is not None
    vec = pl.BlockSpec((None, 1, d), lambda i, kk: (li, 0, 0))
    row = pl.BlockSpec((tm, d), lambda i, kk: (i, 0))
    in_specs = [pl.BlockSpec((tm, tk), lambda i, kk: (i, kk)),
                pl.BlockSpec((tk, d), lambda i, kk: (kk, 0)),
                row, vec, vec]
    out_specs = [row]
    out_shape = [jax.ShapeDtypeStruct((m, d), jnp.float32)]
    args = [hid, wd16, x, g, b]
    if next_gates:
        in_specs += [_gate_weight_spec(li + 1, d), pl.BlockSpec((1, GATE_LANES), lambda i, kk: (0, 0))]
        out_specs += [row, pl.BlockSpec((tm, GATE_LANES), lambda i, kk: (i, 0))]
        out_shape += [jax.ShapeDtypeStruct((m, d), jnp.bfloat16),
                      jax.ShapeDtypeStruct((m, GATE_LANES), jnp.float32)]
        args += [w_in, gb_next]
    return pl.pallas_call(
        functools.partial(_ffn_down_ln_kernel, alpha=alpha, next_gates=next_gates),
        grid=(m // tm, f // tk),
        in_specs=in_specs,
        out_specs=out_specs,
        out_shape=out_shape,
        compiler_params=_params(("parallel", "arbitrary")),
        name="ffn_down_ln",
    )(*args)


def _gate_bias_tile(gate_bias, li):
    return jnp.pad(gate_bias[li], (0, GATE_LANES - 2 * C_HEADS)).reshape(1, GATE_LANES)


def kernel(x, w_in, gate_bias, conv_w, lam, subln_a, norm_c, rel_bias, w_out, ln1_g, ln1_b,
           w_gate, w_up, w_down, ln2_g, ln2_b):
    bsz, s_len, d = x.shape
    depth = w_in.shape[0]
    alpha = (2.0 * depth) ** 0.25
    m = bsz * s_len
    x32 = x.reshape(m, d)
    ln1_g, ln1_b, ln2_g, ln2_b = (p.reshape(depth, 1, d) for p in (ln1_g, ln1_b, ln2_g, ln2_b))
    w_in_t = jnp.swapaxes(w_in, 1, 2)
    x16, gates = _cast_gate(x32, w_in_t, _gate_bias_tile(gate_bias, 0), 0)
    for li in range(depth):
        proj = _in_proj(x16, w_in_t, li).reshape(bsz, s_len, OFF_G)
        gates = gates.reshape(bsz, s_len, GATE_LANES)
        ya, wd16 = _attn_a(proj, lam, subln_a, w_down, li)
        yb = _attn_b(proj, rel_bias[li])
        norm_g = jnp.tile(norm_c[li], 2).reshape(1, C_PAIR)
        yc, wo16 = _mlstm(proj, gates, conv_w, norm_g, w_out, li)
        x32, x16 = _out_proj_ln(ya.reshape(m, A_WIDTH), yb.reshape(m, B_WIDTH), yc.reshape(m, C_WIDTH),
                                wo16, x32, ln1_g, ln1_b, li, alpha)
        hid = _ffn_up(x16, w_gate, w_up, li)
        if li + 1 < depth:
            x32, x16, gates = _ffn_down_ln(hid, wd16, x32, ln2_g, ln2_b, li, alpha,
                                           w_in_t, _gate_bias_tile(gate_bias, li + 1))
        else:
            (x32,) = _ffn_down_ln(hid, wd16, x32, ln2_g, ln2_b, li, alpha)
    return x32.reshape(bsz, s_len, d)
```

```python
import functools
import math

import numpy as np
import jax
import jax.numpy as jnp
from jax import lax
from jax.experimental import pallas as pl
from jax.experimental.pallas import tpu as pltpu

CHUNK = 64
A_HEADS = 4
A_HEAD_DIM = 128
A_HALF = 64
B_HEADS = 6
B_HEAD_DIM = 128
B_PAST_CHUNKS = 8
REL_CLIP = 256
C_HEADS = 4
C_HEAD_DIM = 192
C_PAIR = 2 * C_HEAD_DIM
C_WIN = 256
CONV_K = 4
NEG = -1e30
GATE_LANES = 128

A_WIDTH = A_HEADS * A_HEAD_DIM
B_WIDTH = B_HEADS * B_HEAD_DIM
C_WIDTH = C_HEADS * C_HEAD_DIM
OFF_A = 0
OFF_B = 3 * A_WIDTH
OFF_C = OFF_B + 3 * B_WIDTH
OFF_G = OFF_C + 4 * C_WIDTH
PROJ_TILE = 768

VMEM_LIMIT = 56 * 1024 * 1024

_NT = (((1,), (1,)), ((), ()))


def _params(sem, flags=None):
    return pltpu.CompilerParams(dimension_semantics=sem, vmem_limit_bytes=VMEM_LIMIT, flags=flags)


def _dot(a, b):
    return jnp.dot(a, b, preferred_element_type=jnp.float32)


def _dot_nt(a, b):
    return lax.dot_general(a, b, _NT, preferred_element_type=jnp.float32)


def _log_sigmoid(x):
    return jnp.minimum(x, 0.0) - jnp.log(1.0 + jnp.exp(-jnp.abs(x)))


def _sigmoid(x):
    return 1.0 / (1.0 + jnp.exp(-x))


def _layer_norm(z, g, b):
    mu = jnp.mean(z, axis=1, keepdims=True)
    zc = z - mu
    var = jnp.mean(zc * zc, axis=1, keepdims=True)
    return zc * lax.rsqrt(var + 1e-5) * g + b


N_GATES = 2 * C_HEADS


def _gate_weight_spec(li, d):
    assert OFF_G % N_GATES == 0
    return pl.BlockSpec((None, N_GATES, d), lambda *_: (li, OFF_G // N_GATES, 0))


def _gate_proj(x16, wg_ref, gb_ref):
    wg = wg_ref[...].astype(jnp.bfloat16)
    wg = jnp.concatenate([wg, jnp.zeros((GATE_LANES - N_GATES, wg.shape[1]), wg.dtype)], axis=0)
    return _dot_nt(x16, wg) + gb_ref[...]


def _cast_gate_kernel(x_ref, wg_ref, gb_ref, x16_ref, g_ref):
    x16 = x_ref[...].astype(jnp.bfloat16)
    x16_ref[...] = x16
    g_ref[...] = _gate_proj(x16, wg_ref, gb_ref)


def _cast_gate(x, w_in_t, gb, li, tm=1024):
    m, d = x.shape
    tm = min(tm, m)
    return pl.pallas_call(
        _cast_gate_kernel,
        grid=(m // tm,),
        in_specs=[
            pl.BlockSpec((tm, d), lambda i: (i, 0)),
            _gate_weight_spec(li, d),
            pl.BlockSpec((1, GATE_LANES), lambda i: (0, 0)),
        ],
        out_specs=[pl.BlockSpec((tm, d), lambda i: (i, 0)),
                   pl.BlockSpec((tm, GATE_LANES), lambda i: (i, 0))],
        out_shape=[jax.ShapeDtypeStruct((m, d), jnp.bfloat16),
                   jax.ShapeDtypeStruct((m, GATE_LANES), jnp.float32)],
        compiler_params=_params(("parallel",)),
        name="cast_gate",
    )(x, w_in_t, gb)


def _in_proj_kernel(x_ref, w_ref, o_ref, w16_ref):
    @pl.when(pl.program_id(1) == 0)
    def _():
        w16_ref[...] = w_ref[...].astype(jnp.bfloat16)

    o_ref[...] = _dot_nt(x_ref[...], w16_ref[...]).astype(o_ref.dtype)


def _in_proj(xb, w_in_t, li, tm=2048, tn=PROJ_TILE):
    m, k = xb.shape
    tm = min(tm, m)
    n = OFF_G
    return pl.pallas_call(
        _in_proj_kernel,
        grid=(n // tn, m // tm),
        in_specs=[
            pl.BlockSpec((tm, k), lambda j, i: (i, 0)),
            pl.BlockSpec((None, tn, k), lambda j, i: (li, j, 0)),
        ],
        out_specs=pl.BlockSpec((tm, tn), lambda j, i: (i, j)),
        out_shape=jax.ShapeDtypeStruct((m, n), jnp.bfloat16),
        scratch_shapes=[pltpu.VMEM((tn, k), jnp.bfloat16)],
        compiler_params=_params(("arbitrary", "arbitrary")),
        name="in_proj",
    )(xb, w_in_t)


def _attn_a_kernel(q_ref, k_ref, v_ref, lam_ref, g_ref, slope_ref, wd_ref, o_ref, wd16_ref,
                   qt1_ref, qt2_ref, vt_ref, *, tq, lam_init, ahead):
    wd16_ref[...] = wd_ref[...].astype(jnp.bfloat16)
    s_len = q_ref.shape[0]
    nq = s_len // tq
    slope = slope_ref[0:1, 0:1]
    lf = lam_ref[...]
    lam_full = (jnp.exp(jnp.sum(lf[0:1] * lf[1:2], axis=1, keepdims=True))
                - jnp.exp(jnp.sum(lf[2:3] * lf[3:4], axis=1, keepdims=True)) + lam_init)
    feat = lax.broadcasted_iota(jnp.int32, (A_HEAD_DIM, tq), 0)
    first_half = feat < A_HALF
    for i in range(nq):
        rows = slice(i * tq, (i + 1) * tq)
        qt = (q_ref[rows, :].astype(jnp.float32) * (A_HALF ** -0.5)).T
        qt1_ref[:, rows] = jnp.where(first_half, qt, 0.0).astype(jnp.bfloat16)
        qt2_ref[:, rows] = jnp.where(first_half, 0.0, qt).astype(jnp.bfloat16)
        vt_ref[:, rows] = v_ref[rows, :].astype(jnp.float32).T.astype(jnp.bfloat16)

    key = lax.broadcasted_iota(jnp.int32, (tq, tq), 0)
    qry = lax.broadcasted_iota(jnp.int32, (tq, tq), 1)
    rel = (qry - key).astype(jnp.float32)
    bias_off = -slope * rel
    bias_diag = -slope * jnp.abs(rel)
    diag_ok = (key // CHUNK) <= (qry // CHUNK)

    def update2(t1, t2, shift_c, vt, st1, st2):
        ts, sts = (t1, t2), (st1, st2)
        m_new = [jnp.maximum(st[0], jnp.max(t, axis=0, keepdims=True) + shift_c) for t, st in zip(ts, sts)]
        alpha = [jnp.exp(st[0] - mn) for st, mn in zip(sts, m_new)]
        p = [jnp.exp(t - (mn - shift_c)) for t, mn in zip(ts, m_new)]
        l_new = [a * st[1] + jnp.sum(pp, axis=0, keepdims=True) for a, st, pp in zip(alpha, sts, p)]
        pv = [_dot(vt, pp.astype(jnp.bfloat16)) for pp in p]
        acc = [a * st[2] + x for a, st, x in zip(alpha, sts, pv)]
        return (m_new[0], l_new[0], acc[0]), (m_new[1], l_new[1], acc[1])

    def scores(i, j):
        k = k_ref[j * tq:(j + 1) * tq, :]
        qt1 = qt1_ref[:, i * tq:(i + 1) * tq]
        qt2 = qt2_ref[:, i * tq:(i + 1) * tq]
        if j < i:
            return _dot(k, qt1) + bias_off, _dot(k, qt2) + bias_off
        return (jnp.where(diag_ok, _dot(k, qt1) + bias_diag, NEG),
                jnp.where(diag_ok, _dot(k, qt2) + bias_diag, NEG))

    pairs = [(i, j) for i in range(nq) for j in range(i + 1)]
    pending = [scores(*pr) for pr in pairs[:ahead]]
    fresh = (jnp.full((1, tq), NEG, jnp.float32), jnp.zeros((1, tq), jnp.float32),
             jnp.zeros((A_HEAD_DIM, tq), jnp.float32))
    for n, (i, j) in enumerate(pairs):
        if j == 0:
            st1 = st2 = fresh
        t1, t2 = pending.pop(0)
        if n + ahead < len(pairs):
            pending.append(scores(*pairs[n + ahead]))
        if j < i:
            shift_c = -slope * float((i - j) * tq)
        else:
            shift_c = jnp.zeros((1, 1), jnp.float32)
        st1, st2 = update2(t1, t2, shift_c, vt_ref[:, j * tq:(j + 1) * tq], st1, st2)
        if j == i:
            ot = st1[2] * (1.0 / st1[1]) - (lam_full / st2[1]) * st2[2]
            o = ot.T
            ms = jnp.mean(o * o, axis=1, keepdims=True)
            o = o * lax.rsqrt(ms + 1e-6) * g_ref[...] * (1.0 - lam_init)
            o_ref[i * tq:(i + 1) * tq, :] = o.astype(o_ref.dtype)


def _attn_a(proj, lam, subln, w_down, li, tq=256, ahead=1):
    b, s, _ = proj.shape
    tq = min(tq, s)
    f, d = w_down.shape[1:]
    slab = f // (b * A_HEADS)
    assert slab * b * A_HEADS == f and slab % 16 == 0
    lam_init = 0.8 - 0.6 * math.exp(-0.3 * li)
    slopes = np.asarray([2.0 ** (-8.0 * (h + 1) / A_HEADS) for h in range(A_HEADS)], np.float32)
    slopes = jnp.asarray(np.broadcast_to(slopes[:, None, None], (A_HEADS, 1, 128)))
    qi = OFF_A // A_HEAD_DIM
    ki = (OFF_A + A_WIDTH) // A_HEAD_DIM
    vi = (OFF_A + 2 * A_WIDTH) // A_HEAD_DIM

    def hspec(base):
        return pl.BlockSpec((None, s, A_HEAD_DIM), lambda bi, h, base=base: (bi, 0, base + h))

    return pl.pallas_call(
        functools.partial(_attn_a_kernel, tq=tq, lam_init=lam_init, ahead=ahead),
        grid=(b, A_HEADS),
        in_specs=[
            hspec(qi), hspec(ki), hspec(vi),
            pl.BlockSpec((None, 4, A_HALF), lambda bi, h: (li, 0, 0)),
            pl.BlockSpec((None, 1, A_HEAD_DIM), lambda bi, h: (li, 0, 0)),
            pl.BlockSpec((None, 1, 128), lambda bi, h: (h, 0, 0)),
            pl.BlockSpec((None, slab, d), lambda bi, h: (li, bi * A_HEADS + h, 0)),
        ],
        out_specs=[pl.BlockSpec((None, s, A_HEAD_DIM), lambda bi, h: (bi, 0, h)),
                   pl.BlockSpec((slab, d), lambda bi, h: (bi * A_HEADS + h, 0))],
        out_shape=[jax.ShapeDtypeStruct((b, s, A_WIDTH), jnp.bfloat16),
                   jax.ShapeDtypeStruct((f, d), jnp.bfloat16)],
        scratch_shapes=[
            pltpu.VMEM((A_HEAD_DIM, s), jnp.bfloat16),
            pltpu.VMEM((A_HEAD_DIM, s), jnp.bfloat16),
            pltpu.VMEM((A_HEAD_DIM, s), jnp.bfloat16),
        ],
        compiler_params=_params(("parallel", "parallel")),
        name="attn_a",
    )(proj, proj, proj, lam, subln.reshape(-1, 1, A_HEAD_DIM), slopes, w_down)


def _attn_b_kernel(q_ref, k_ref, v_ref, u_ref, o_ref, kpad, vpad, table_ref, *, tq, pad):
    s_len = q_ref.shape[0]
    win = pad + tq
    width = u_ref.shape[1]

    @pl.when(pl.program_id(1) == 0)
    def _():
        rolled = pltpu.roll(jnp.broadcast_to(u_ref[...], (tq, width)), width - (tq - 1), 1,
                            stride=1, stride_axis=0)
        r = lax.broadcasted_iota(jnp.int32, (tq, win), 0)
        j = lax.broadcasted_iota(jnp.int32, (tq, win), 1)
        kc, qc = j // CHUNK, B_PAST_CHUNKS + r // CHUNK
        table_ref[...] = jnp.where((kc <= qc) & (kc >= qc - B_PAST_CHUNKS), rolled[:, :win], NEG)

    kpad[0:pad, :] = jnp.zeros((pad, B_HEAD_DIM), kpad.dtype)
    vpad[0:pad, :] = jnp.zeros((pad, B_HEAD_DIM), vpad.dtype)
    kpad[pad:pad + s_len, :] = k_ref[...]
    vpad[pad:pad + s_len, :] = v_ref[...]
    table = table_ref[...]
    jcol = lax.broadcasted_iota(jnp.int32, (tq, win), 1)
    scale = B_HEAD_DIM ** -0.5
    nq = s_len // tq

    def scores(i):
        qs = i * tq
        s = _dot_nt(q_ref[qs:qs + tq, :], kpad[qs:qs + win, :]) * scale + table
        if qs < pad:
            s = jnp.where(jcol >= pad - qs, s, NEG)
        return s

    nxt = scores(0)
    for i in range(nq):
        s = nxt
        if i + 1 < nq:
            nxt = scores(i + 1)
        qs = i * tq
        m = jnp.max(s, axis=1, keepdims=True)
        p = jnp.exp(s - m)
        l = jnp.sum(p, axis=1, keepdims=True)
        o = _dot(p.astype(jnp.bfloat16), vpad[qs:qs + win, :]) / l
        o_ref[qs:qs + tq, :] = o.astype(o_ref.dtype)


def _band_bias_vector(rel_bias, tq, pad):
    n = pad + 2 * tq - 1
    n_hi = pad + tq - REL_CLIP
    lo = REL_CLIP - tq + 1
    assert n_hi >= 0 and lo >= 0
    rb = rel_bias.astype(jnp.float32)
    u = jnp.concatenate([jnp.repeat(rb[:, 2 * REL_CLIP:], n_hi, axis=1), rb[:, lo:2 * REL_CLIP][:, ::-1]], axis=1)
    assert u.shape[1] == n
    return jnp.pad(u, ((0, 0), (0, 1)))[:, None, :]


def _attn_b(proj, rel_bias, tq=256):
    b, s, _ = proj.shape
    pad = B_PAST_CHUNKS * CHUNK
    width = pad + 2 * tq
    assert width & (width - 1) == 0
    u = _band_bias_vector(rel_bias, tq, pad)
    qi = OFF_B // B_HEAD_DIM
    ki = (OFF_B + B_WIDTH) // B_HEAD_DIM
    vi = (OFF_B + 2 * B_WIDTH) // B_HEAD_DIM

    def hspec(base):
        return pl.BlockSpec((None, s, B_HEAD_DIM), lambda h, bi, base=base: (bi, 0, base + h))

    return pl.pallas_call(
        functools.partial(_attn_b_kernel, tq=tq, pad=pad),
        grid=(B_HEADS, b),
        in_specs=[
            hspec(qi), hspec(ki), hspec(vi),
            pl.BlockSpec((None, 1, width), lambda h, bi: (h, 0, 0)),
        ],
        out_specs=pl.BlockSpec((None, s, B_HEAD_DIM), lambda h, bi: (bi, 0, h)),
        out_shape=jax.ShapeDtypeStruct((b, s, B_WIDTH), jnp.bfloat16),
        scratch_shapes=[
            pltpu.VMEM((pad + s, B_HEAD_DIM), jnp.bfloat16),
            pltpu.VMEM((pad + s, B_HEAD_DIM), jnp.bfloat16),
            pltpu.VMEM((tq, pad + tq), jnp.float32),
        ],
        compiler_params=_params(("arbitrary", "arbitrary")),
        name="attn_b",
    )(proj, proj, proj, u)


def _cumsum_rows(x):
    n = x.shape[0]
    row = lax.broadcasted_iota(jnp.int32, x.shape, 0)
    sh = 1
    while sh < n:
        x = x + jnp.where(row >= sh, pltpu.roll(x, sh, 0), 0.0)
        sh *= 2
    return x


def _mlstm_kernel(q_ref, k_ref, v_ref, og_ref, gate_ref, cq_ref, ck_ref, ng_ref, wo_ref, o_ref, wo16_ref,
                  qs_ref, ks_ref, state_ref, gt_ref, *, chunk, halo):
    wo16_ref[...] = wo_ref[...].astype(jnp.bfloat16)
    s_len = q_ref.shape[0]
    n_chunks = s_len // chunk
    pair = pl.program_id(1)
    lane = lax.broadcasted_iota(jnp.int32, (chunk, C_WIN), 1)
    glane = lax.broadcasted_iota(jnp.int32, (chunk, GATE_LANES), 1)
    trow = lax.broadcasted_iota(jnp.int32, (chunk, chunk), 0)
    tcol = lax.broadcasted_iota(jnp.int32, (chunk, chunk), 1)
    causal = tcol <= trow
    windows = ((0, lane < C_HEAD_DIM, C_HEAD_DIM),
               (C_PAIR - C_WIN, lane >= C_WIN - C_HEAD_DIM, 0))

    def conv_chunk(c, carry):
        st = pl.multiple_of(c * chunk, chunk)
        prev_st = pl.multiple_of(jnp.maximum(st - halo, 0), halo)
        keep = jnp.where(c > 0, 1.0, 0.0)
        for src, w_ref, dst, post in ((q_ref, cq_ref, qs_ref, 1.0),
                                      (k_ref, ck_ref, ks_ref, C_HEAD_DIM ** -0.5)):
            cur = src[pl.ds(st, chunk), :].astype(jnp.float32)
            prev = src[pl.ds(prev_st, halo), :].astype(jnp.float32) * keep
            xc = jnp.concatenate([prev, cur], axis=0)
            w = w_ref[...]
            y = w[CONV_K - 1:CONV_K, :] * cur
            for back in range(1, CONV_K):
                y = y + w[CONV_K - 1 - back:CONV_K - back, :] * pltpu.roll(xc, back, 0)[halo:, :]
            y = y * _sigmoid(y) * post
            dst[pl.ds(st, chunk), :] = y.astype(dst.dtype)
        return carry

    lax.fori_loop(0, n_chunks, conv_chunk, 0)
    state_ref[...] = jnp.zeros(state_ref.shape, jnp.float32)

    def step(c, ms):
        st = pl.multiple_of(c * chunk, chunk)
        g = gate_ref[pl.ds(st, chunk), :]
        bcum = _cumsum_rows(_log_sigmoid(g))
        gb = jnp.where(glane < C_HEADS, g, bcum)
        gt_ref[...] = gb.T
        ng = ng_ref[...]
        zero = jnp.zeros((), jnp.bfloat16)
        ops, qk, qc = [], [], []
        for a, (w0, valid, spare) in enumerate(windows):
            wcols = slice(w0, w0 + C_WIN)
            q = jnp.where(valid, qs_ref[pl.ds(st, chunk), wcols], zero)
            k = jnp.where(valid, ks_ref[pl.ds(st, chunk), wcols], zero)
            v = jnp.where(valid, v_ref[pl.ds(st, chunk), wcols], zero)
            v_ext = jnp.where(lane == spare, jnp.ones((), v.dtype), v)
            ops.append((q, k, v_ext))
            qk.append(_dot_nt(q, k))
            qc.append(_dot(q, state_ref[a].astype(jnp.bfloat16)))

        gate, m_out = [], []
        for a in range(2):
            h = 2 * pair + a
            m = ms[a]
            i_row = gt_ref[pl.ds(h, 1), :]
            b_row = gt_ref[pl.ds(C_HEADS + h, 1), :]
            i_col = jnp.sum(jnp.where(glane == h, gb, 0.0), axis=1, keepdims=True)
            b_col = jnp.sum(jnp.where(glane == C_HEADS + h, gb, 0.0), axis=1, keepdims=True)
            d = jnp.where(causal, b_col + (i_row - b_row), NEG)
            inter = b_col + m
            m_t = jnp.maximum(inter, jnp.max(d, axis=1, keepdims=True))
            b_last = b_col[chunk - 1:chunk, :]
            g_col = b_last - b_col + i_col
            m_new = jnp.maximum(b_last + m, jnp.max(g_col, axis=0, keepdims=True))
            gate.append((d, inter, m_t, jnp.exp(g_col - m_new), jnp.exp(b_last + m - m_new)))
            m_out.append(m_new)

        nds = []
        for a in range(2):
            d, inter, m_t = gate[a][:3]
            sw = qk[a] * jnp.exp(d - m_t)
            nds.append(_dot(sw.astype(jnp.bfloat16), ops[a][2]) + jnp.exp(inter - m_t) * qc[a])

        for a in range(2):
            k, v_ext = ops[a][1:]
            wk, decay = gate[a][3:]
            kw_t = (k.astype(jnp.float32) * wk).T.astype(jnp.bfloat16)
            state_ref[a] = decay * state_ref[a] + _dot(kw_t, v_ext)

        hn = []
        for a, (w0, valid, spare) in enumerate(windows):
            nd, m_t = nds[a], gate[a][2]
            den = jnp.sum(jnp.where(lane == spare, nd, 0.0), axis=1, keepdims=True)
            num = jnp.where(valid, nd, 0.0)
            hh = num / jnp.maximum(jnp.abs(den), jnp.exp(-m_t))
            msq = jnp.sum(hh * hh, axis=1, keepdims=True) * (1.0 / C_HEAD_DIM)
            hn.append(hh * lax.rsqrt(msq + 1e-6) * ng[:, w0:w0 + C_WIN])

        mid = C_PAIR - C_WIN
        hn_pair = jnp.concatenate([hn[0][:, :mid], hn[0][:, mid:] + hn[1][:, :C_WIN - mid], hn[1][:, C_WIN - mid:]],
                                  axis=1)
        og = og_ref[pl.ds(st, chunk), :].astype(jnp.float32)
        o_ref[pl.ds(st, chunk), :] = (_sigmoid(og) * hn_pair).astype(o_ref.dtype)
        return tuple(m_out)

    m0 = jnp.zeros((1, 1), jnp.float32)
    lax.fori_loop(0, n_chunks, step, (m0, m0), unroll=2)


def _mlstm(proj, gates, conv_w, norm_g, w_out, li, chunk=256, halo=16):
    b, s, _ = proj.shape
    chunk = min(chunk, s)
    n_pairs = C_HEADS // 2
    base = OFF_C // C_PAIR
    d_in, d = w_out.shape[1:]
    slab = d_in // (b * n_pairs)
    assert slab * b * n_pairs == d_in and slab % 16 == 0

    def pspec(t):
        return pl.BlockSpec((None, s, C_PAIR), lambda bi, p, t=t: (bi, 0, base + n_pairs * t + p))

    return pl.pallas_call(
        functools.partial(_mlstm_kernel, chunk=chunk, halo=halo),
        grid=(b, n_pairs),
        in_specs=[
            pspec(0), pspec(1), pspec(2), pspec(3),
            pl.BlockSpec((None, s, GATE_LANES), lambda bi, p: (bi, 0, 0)),
            pl.BlockSpec((None, CONV_K, C_PAIR), lambda bi, p: (li, 0, p)),
            pl.BlockSpec((None, CONV_K, C_PAIR), lambda bi, p: (li, 0, n_pairs + p)),
            pl.BlockSpec((1, C_PAIR), lambda bi, p: (0, 0)),
            pl.BlockSpec((None, slab, d), lambda bi, p: (li, bi * n_pairs + p, 0)),
        ],
        out_specs=[pl.BlockSpec((None, s, C_PAIR), lambda bi, p: (bi, 0, p)),
                   pl.BlockSpec((slab, d), lambda bi, p: (bi * n_pairs + p, 0))],
        out_shape=[jax.ShapeDtypeStruct((b, s, C_WIDTH), jnp.bfloat16),
                   jax.ShapeDtypeStruct((d_in, d), jnp.bfloat16)],
        scratch_shapes=[
            pltpu.VMEM((s, C_PAIR), jnp.bfloat16),
            pltpu.VMEM((s, C_PAIR), jnp.bfloat16),
            pltpu.VMEM((2, C_WIN, C_WIN), jnp.float32),
            pltpu.VMEM((GATE_LANES, chunk), jnp.float32),
        ],
        compiler_params=_params(("parallel", "parallel")),
        name="mlstm",
    )(proj, proj, proj, proj, gates, conv_w, conv_w, norm_g, w_out)


def _out_proj_ln_kernel(ya_ref, yb_ref, yc_ref, w16_ref, x_ref, g_ref, b_ref, o32_ref, o16_ref, *, alpha):
    half = x_ref.shape[0] // 2

    def project(r):
        rows = slice(r * half, (r + 1) * half)
        mixed = jnp.concatenate([ya_ref[rows, :], yb_ref[rows, :], yc_ref[rows, :]], axis=1)
        return _dot(mixed, w16_ref[...])

    nxt = project(0)
    for r in range(2):
        mix = nxt
        if r == 0:
            nxt = project(1)
        rows = slice(r * half, (r + 1) * half)
        y = _layer_norm(alpha * x_ref[rows, :] + mix, g_ref[...], b_ref[...])
        o32_ref[rows, :] = y
        o16_ref[rows, :] = y.astype(o16_ref.dtype)


def _out_proj_ln(ya, yb, yc, wo16, x, g, b, li, alpha, tm=512):
    m, d = x.shape
    tm = min(tm, m)
    row = lambda w: pl.BlockSpec((tm, w), lambda i: (i, 0))
    vec = pl.BlockSpec((None, 1, d), lambda i: (li, 0, 0))
    return pl.pallas_call(
        functools.partial(_out_proj_ln_kernel, alpha=alpha),
        grid=(m // tm,),
        in_specs=[row(ya.shape[1]), row(yb.shape[1]), row(yc.shape[1]),
                  pl.BlockSpec(wo16.shape, lambda i: (0, 0), pipeline_mode=pl.Buffered(1)),
                  row(d), vec, vec],
        out_specs=[row(d), row(d)],
        out_shape=[jax.ShapeDtypeStruct((m, d), jnp.float32),
                   jax.ShapeDtypeStruct((m, d), jnp.bfloat16)],
        compiler_params=_params(("parallel",)),
        name="out_proj_ln",
    )(ya, yb, yc, wo16, x, g, b)


FFN_SLABS = 4


def _ffn_up_kernel(x_ref, wg_ref, wu_ref, o_ref, wg16_ref, wu16_ref):
    @pl.when(pl.program_id(1) == 0)
    def _():
        wg16_ref[...] = wg_ref[...].astype(jnp.bfloat16)
        wu16_ref[...] = wu_ref[...].astype(jnp.bfloat16)

    slab = x_ref.shape[0] // FFN_SLABS

    def products(r):
        x = x_ref[r * slab:(r + 1) * slab, :]
        return _dot(x, wg16_ref[...]), _dot(x, wu16_ref[...])

    nxt = products(0)
    for r in range(FFN_SLABS):
        a, u = nxt
        if r + 1 < FFN_SLABS:
            nxt = products(r + 1)
        o_ref[r * slab:(r + 1) * slab, :] = (a * _sigmoid(a) * u).astype(o_ref.dtype)


def _ffn_up(xb, w_gate, w_up, li, tm=2048, tn=512):
    m, k = xb.shape
    n = w_gate.shape[2]
    tm, tn = min(tm, m), min(tn, n)
    return pl.pallas_call(
        _ffn_up_kernel,
        grid=(n // tn, m // tm),
        in_specs=[
            pl.BlockSpec((tm, k), lambda j, i: (i, 0)),
            pl.BlockSpec((None, k, tn), lambda j, i: (li, 0, j)),
            pl.BlockSpec((None, k, tn), lambda j, i: (li, 0, j)),
        ],
        out_specs=pl.BlockSpec((tm, tn), lambda j, i: (i, j)),
        out_shape=jax.ShapeDtypeStruct((m, n), jnp.bfloat16),
        scratch_shapes=[pltpu.VMEM((k, tn), jnp.bfloat16), pltpu.VMEM((k, tn), jnp.bfloat16)],
        compiler_params=_params(("arbitrary", "arbitrary")),
        name="ffn_up",
    )(xb, w_gate, w_up)


def _ffn_down_ln_kernel(h_ref, w_ref, x_ref, g_ref, b_ref, *rest, alpha, next_gates):
    if next_gates:
        wg_ref, gb_ref, o32_ref, o16_ref, gate_ref = rest
    else:
        (o32_ref,) = rest
    kk = pl.program_id(1)

    @pl.when(kk == 0)
    def _():
        o32_ref[...] = jnp.zeros(o32_ref.shape, jnp.float32)

    o32_ref[...] += _dot(h_ref[...], w_ref[...])

    @pl.when(kk == pl.num_programs(1) - 1)
    def _():
        y = _layer_norm(alpha * x_ref[...] + o32_ref[...], g_ref[...], b_ref[...])
        o32_ref[...] = y
        if next_gates:
            y16 = y.astype(jnp.bfloat16)
            o16_ref[...] = y16
            gate_ref[...] = _gate_proj(y16, wg_ref, gb_ref)


def _ffn_down_ln(hid, wd16, x, g, b, li, alpha, w_in=None, gb_next=None, tm=512, tk=2816):
    m, f = hid.shape
    d = wd16.shape[1]
    tm, tk = min(tm, m), min(tk, f)
    next_gates = w_in is not None
    vec = pl.BlockSpec((None, 1, d), lambda i, kk: (li, 0, 0))
    row = pl.BlockSpec((tm, d), lambda i, kk: (i, 0))
    in_specs = [pl.BlockSpec((tm, tk), lambda i, kk: (i, kk)),
                pl.BlockSpec((tk, d), lambda i, kk: (kk, 0)),
                row, vec, vec]
    out_specs = [row]
    out_shape = [jax.ShapeDtypeStruct((m, d), jnp.float32)]
    args = [hid, wd16, x, g, b]
    if next_gates:
        in_specs += [_gate_weight_spec(li + 1, d), pl.BlockSpec((1, GATE_LANES), lambda i, kk: (0, 0))]
        out_specs += [row, pl.BlockSpec((tm, GATE_LANES), lambda i, kk: (i, 0))]
        out_shape += [jax.ShapeDtypeStruct((m, d), jnp.bfloat16),
                      jax.ShapeDtypeStruct((m, GATE_LANES), jnp.float32)]
        args += [w_in, gb_next]
    return pl.pallas_call(
        functools.partial(_ffn_down_ln_kernel, alpha=alpha, next_gates=next_gates),
        grid=(m // tm, f // tk),
        in_specs=in_specs,
        out_specs=out_specs,
        out_shape=out_shape,
        compiler_params=_params(("parallel", "arbitrary")),
        name="ffn_down_ln",
    )(*args)


def _gate_bias_tile(gate_bias, li):
    return jnp.pad(gate_bias[li], (0, GATE_LANES - 2 * C_HEADS)).reshape(1, GATE_LANES)


def kernel(x, w_in, gate_bias, conv_w, lam, subln_a, norm_c, rel_bias, w_out, ln1_g, ln1_b,
           w_gate, w_up, w_down, ln2_g, ln2_b):
    bsz, s_len, d = x.shape
    depth = w_in.shape[0]
    alpha = (2.0 * depth) ** 0.25
    m = bsz * s_len
    x32 = x.reshape(m, d)
    ln1_g, ln1_b, ln2_g, ln2_b = (p.reshape(depth, 1, d) for p in (ln1_g, ln1_b, ln2_g, ln2_b))
    w_in_t = jnp.swapaxes(w_in, 1, 2)
    x16, gates = _cast_gate(x32, w_in_t, _gate_bias_tile(gate_bias, 0), 0)
    for li in range(depth):
        proj = _in_proj(x16, w_in_t, li).reshape(bsz, s_len, OFF_G)
        gates = gates.reshape(bsz, s_len, GATE_LANES)
        ya, wd16 = _attn_a(proj, lam, subln_a, w_down, li)
        yb = _attn_b(proj, rel_bias[li])
        norm_g = jnp.tile(norm_c[li], 2).reshape(1, C_PAIR)
        yc, wo16 = _mlstm(proj, gates, conv_w, norm_g, w_out, li)
        x32, x16 = _out_proj_ln(ya.reshape(m, A_WIDTH), yb.reshape(m, B_WIDTH), yc.reshape(m, C_WIDTH),
                                wo16, x32, ln1_g, ln1_b, li, alpha)
        hid = _ffn_up(x16, w_gate, w_up, li)
        if li + 1 < depth:
            x32, x16, gates = _ffn_down_ln(hid, wd16, x32, ln2_g, ln2_b, li, alpha,
                                           w_in_t, _gate_bias_tile(gate_bias, li + 1))
        else:
            (x32,) = _ffn_down_ln(hid, wd16, x32, ln2_g, ln2_b, li, alpha)
    return x32.reshape(bsz, s_len, d)
```

```python
import functools
import math

import numpy as np
import jax
import jax.numpy as jnp
from jax import lax
from jax.experimental import pallas as pl
from jax.experimental.pallas import tpu as pltpu

CHUNK = 64
A_HEADS = 4
A_HEAD_DIM = 128
A_HALF = 64
B_HEADS = 6
B_HEAD_DIM = 128
B_PAST_CHUNKS = 8
REL_CLIP = 256
C_HEADS = 4
C_HEAD_DIM = 192
C_PAIR = 2 * C_HEAD_DIM
C_WIN = 256
CONV_K = 4
NEG = -1e30
GATE_LANES = 128

A_WIDTH = A_HEADS * A_HEAD_DIM
B_WIDTH = B_HEADS * B_HEAD_DIM
C_WIDTH = C_HEADS * C_HEAD_DIM
OFF_A = 0
OFF_B = 3 * A_WIDTH
OFF_C = OFF_B + 3 * B_WIDTH
OFF_G = OFF_C + 4 * C_WIDTH
PROJ_TILE = 768

A_VT_ROWS = A_HEAD_DIM + 16
LOG2E = math.log2(math.e)

VMEM_LIMIT = 56 * 1024 * 1024

_NT = (((1,), (1,)), ((), ()))


def _params(sem, flags=None):
    return pltpu.CompilerParams(dimension_semantics=sem, vmem_limit_bytes=VMEM_LIMIT, flags=flags)


def _dot(a, b):
    return jnp.dot(a, b, preferred_element_type=jnp.float32)


def _dot_nt(a, b):
    return lax.dot_general(a, b, _NT, preferred_element_type=jnp.float32)


def _log_sigmoid(x):
    return jnp.minimum(x, 0.0) - jnp.log(1.0 + jnp.exp(-jnp.abs(x)))


def _sigmoid(x):
    return 1.0 / (1.0 + jnp.exp(-x))


def _layer_norm(z, g, b):
    mu = jnp.mean(z, axis=1, keepdims=True)
    zc = z - mu
    var = jnp.mean(zc * zc, axis=1, keepdims=True)
    return zc * lax.rsqrt(var + 1e-5) * g + b


N_GATES = 2 * C_HEADS


def _gate_weight_spec(li, d):
    assert OFF_G % N_GATES == 0
    return pl.BlockSpec((None, N_GATES, d), lambda *_: (li, OFF_G // N_GATES, 0))


def _gate_proj(x16, wg_ref, gb_ref):
    wg = wg_ref[...].astype(jnp.bfloat16)
    wg = jnp.concatenate([wg, jnp.zeros((GATE_LANES - N_GATES, wg.shape[1]), wg.dtype)], axis=0)
    return _dot_nt(x16, wg) + gb_ref[...]


def _cast_gate_kernel(x_ref, wg_ref, gb_ref, x16_ref, g_ref):
    x16 = x_ref[...].astype(jnp.bfloat16)
    x16_ref[...] = x16
    g_ref[...] = _gate_proj(x16, wg_ref, gb_ref)


def _cast_gate(x, w_in_t, gb, li, tm=1024):
    m, d = x.shape
    tm = min(tm, m)
    return pl.pallas_call(
        _cast_gate_kernel,
        grid=(m // tm,),
        in_specs=[
            pl.BlockSpec((tm, d), lambda i: (i, 0)),
            _gate_weight_spec(li, d),
            pl.BlockSpec((1, GATE_LANES), lambda i: (0, 0)),
        ],
        out_specs=[pl.BlockSpec((tm, d), lambda i: (i, 0)),
                   pl.BlockSpec((tm, GATE_LANES), lambda i: (i, 0))],
        out_shape=[jax.ShapeDtypeStruct((m, d), jnp.bfloat16),
                   jax.ShapeDtypeStruct((m, GATE_LANES), jnp.float32)],
        compiler_params=_params(("parallel",)),
        name="cast_gate",
    )(x, w_in_t, gb)


def _in_proj_kernel(x_ref, w_ref, o_ref, w16_ref):
    @pl.when(pl.program_id(1) == 0)
    def _():
        w16_ref[...] = w_ref[...].astype(jnp.bfloat16)

    o_ref[...] = _dot_nt(x_ref[...], w16_ref[...]).astype(o_ref.dtype)


def _in_proj(xb, w_in_t, li, tm=2048, tn=PROJ_TILE):
    m, k = xb.shape
    tm = min(tm, m)
    n = OFF_G
    return pl.pallas_call(
        _in_proj_kernel,
        grid=(n // tn, m // tm),
        in_specs=[
            pl.BlockSpec((tm, k), lambda j, i: (i, 0)),
            pl.BlockSpec((None, tn, k), lambda j, i: (li, j, 0)),
        ],
        out_specs=pl.BlockSpec((tm, tn), lambda j, i: (i, j)),
        out_shape=jax.ShapeDtypeStruct((m, n), jnp.bfloat16),
        scratch_shapes=[pltpu.VMEM((tn, k), jnp.bfloat16)],
        compiler_params=_params(("arbitrary", "arbitrary")),
        name="in_proj",
    )(xb, w_in_t)


def _attn_a_kernel(q_ref, k_ref, v_ref, lam_ref, g_ref, slope_ref, wd_ref, o_ref, wd16_ref,
                   qt1_ref, qt2_ref, k1_ref, k2_ref, vt_ref, *, tq, lam_init, ahead):
    wd16_ref[...] = wd_ref[...].astype(jnp.bfloat16)
    s_len = q_ref.shape[0]
    nq = s_len // tq
    slope2 = slope_ref[0:1, 0:1] * LOG2E
    lf = lam_ref[...]
    lam_full = (jnp.exp(jnp.sum(lf[0:1] * lf[1:2], axis=1, keepdims=True))
                - jnp.exp(jnp.sum(lf[2:3] * lf[3:4], axis=1, keepdims=True)) + lam_init)
    feat = lax.broadcasted_iota(jnp.int32, (A_HEAD_DIM, tq), 0)
    lane = lax.broadcasted_iota(jnp.int32, (tq, A_HEAD_DIM), 1)
    key_bias = slope2 * lax.broadcasted_iota(jnp.int32, (tq, A_HEAD_DIM), 0).astype(jnp.float32)
    kb_hi = key_bias.astype(jnp.bfloat16).astype(jnp.float32)
    kb_lo = (key_bias - kb_hi).astype(jnp.bfloat16).astype(jnp.float32)
    ones_row = jnp.where(lax.broadcasted_iota(jnp.int32, (A_VT_ROWS - A_HEAD_DIM, tq), 0) == 0, 1.0, 0.0)
    for i in range(nq):
        rows = slice(i * tq, (i + 1) * tq)
        qt = (q_ref[rows, :].astype(jnp.float32) * (A_HALF ** -0.5 * LOG2E)).T
        qt1_ref[:, rows] = jnp.where(feat < A_HALF, qt, jnp.where(feat < A_HALF + 2, 1.0, 0.0)).astype(jnp.bfloat16)
        qt2_ref[:, rows] = jnp.where(feat >= A_HALF, qt, jnp.where(feat < 2, 1.0, 0.0)).astype(jnp.bfloat16)
        k = k_ref[rows, :].astype(jnp.float32)
        k1 = jnp.where(lane < A_HALF, k, jnp.where(lane == A_HALF, kb_hi, jnp.where(lane == A_HALF + 1, kb_lo, 0.0)))
        k2 = jnp.where(lane >= A_HALF, k, jnp.where(lane == 0, kb_hi, jnp.where(lane == 1, kb_lo, 0.0)))
        k1_ref[rows, :] = k1.astype(jnp.bfloat16)
        k2_ref[rows, :] = k2.astype(jnp.bfloat16)
        vt_ref[0:A_HEAD_DIM, rows] = v_ref[rows, :].astype(jnp.float32).T.astype(jnp.bfloat16)
        vt_ref[A_HEAD_DIM:, rows] = ones_row.astype(jnp.bfloat16)

    key = lax.broadcasted_iota(jnp.int32, (tq, tq), 0)
    qry = lax.broadcasted_iota(jnp.int32, (tq, tq), 1)
    rel = (qry - key).astype(jnp.float32)
    bias_diag = slope2 * (rel - jnp.abs(rel))
    diag_ok = (key // CHUNK) <= (qry // CHUNK)

    def update2(t1, t2, shift_c, vt, st1, st2):
        ts, sts = (t1, t2), (st1, st2)
        m_new = [jnp.maximum(st[0], jnp.max(t, axis=0, keepdims=True) + shift_c) for t, st in zip(ts, sts)]
        alpha = [jnp.exp2(st[0] - mn) for st, mn in zip(sts, m_new)]
        p = [jnp.exp2(t - (mn - shift_c)) for t, mn in zip(ts, m_new)]
        pv = [_dot(vt, pp.astype(jnp.bfloat16)) for pp in p]
        acc = [a * st[1] + x for a, st, x in zip(alpha, sts, pv)]
        return (m_new[0], acc[0]), (m_new[1], acc[1])

    def scores(i, j):
        t1 = _dot(k1_ref[j * tq:(j + 1) * tq, :], qt1_ref[:, i * tq:(i + 1) * tq])
        t2 = _dot(k2_ref[j * tq:(j + 1) * tq, :], qt2_ref[:, i * tq:(i + 1) * tq])
        if j < i:
            return t1, t2
        return jnp.where(diag_ok, t1 + bias_diag, NEG), jnp.where(diag_ok, t2 + bias_diag, NEG)

    pairs = [(i, j) for i in range(nq) for j in range(i + 1)]
    pending = [scores(*pr) for pr in pairs[:ahead]]
    fresh = (jnp.full((1, tq), NEG, jnp.float32), jnp.zeros((A_VT_ROWS, tq), jnp.float32))
    for n, (i, j) in enumerate(pairs):
        if j == 0:
            st1 = st2 = fresh
        t1, t2 = pending.pop(0)
        if n + ahead < len(pairs):
            pending.append(scores(*pairs[n + ahead]))
        if j < i:
            shift_c = -slope2 * float((i - j) * tq)
        else:
            shift_c = jnp.zeros((1, 1), jnp.float32)
        st1, st2 = update2(t1, t2, shift_c, vt_ref[:, j * tq:(j + 1) * tq], st1, st2)
        if j == i:
            (num1, l1), (num2, l2) = [(st[1][:A_HEAD_DIM], st[1][A_HEAD_DIM:A_HEAD_DIM + 1]) for st in (st1, st2)]
            ot = num1 * (1.0 / l1) - (lam_full / l2) * num2
            o = ot.T
            ms = jnp.mean(o * o, axis=1, keepdims=True)
            o = o * lax.rsqrt(ms + 1e-6) * g_ref[...] * (1.0 - lam_init)
            o_ref[i * tq:(i + 1) * tq, :] = o.astype(o_ref.dtype)


def _attn_a(proj, lam, subln, w_down, li, tq=256, ahead=1):
    b, s, _ = proj.shape
    tq = min(tq, s)
    f, d = w_down.shape[1:]
    slab = f // (b * A_HEADS)
    assert slab * b * A_HEADS == f and slab % 16 == 0
    lam_init = 0.8 - 0.6 * math.exp(-0.3 * li)
    slopes = np.asarray([2.0 ** (-8.0 * (h + 1) / A_HEADS) for h in range(A_HEADS)], np.float32)
    slopes = jnp.asarray(np.broadcast_to(slopes[:, None, None], (A_HEADS, 1, 128)))
    qi = OFF_A // A_HEAD_DIM
    ki = (OFF_A + A_WIDTH) // A_HEAD_DIM
    vi = (OFF_A + 2 * A_WIDTH) // A_HEAD_DIM

    def hspec(base):
        return pl.BlockSpec((None, s, A_HEAD_DIM), lambda bi, h, base=base: (bi, 0, base + h))

    return pl.pallas_call(
        functools.partial(_attn_a_kernel, tq=tq, lam_init=lam_init, ahead=ahead),
        grid=(b, A_HEADS),
        in_specs=[
            hspec(qi), hspec(ki), hspec(vi),
            pl.BlockSpec((None, 4, A_HALF), lambda bi, h: (li, 0, 0)),
            pl.BlockSpec((None, 1, A_HEAD_DIM), lambda bi, h: (li, 0, 0)),
            pl.BlockSpec((None, 1, 128), lambda bi, h: (h, 0, 0)),
            pl.BlockSpec((None, slab, d), lambda bi, h: (li, bi * A_HEADS + h, 0)),
        ],
        out_specs=[pl.BlockSpec((None, s, A_HEAD_DIM), lambda bi, h: (bi, 0, h)),
                   pl.BlockSpec((slab, d), lambda bi, h: (bi * A_HEADS + h, 0))],
        out_shape=[jax.ShapeDtypeStruct((b, s, A_WIDTH), jnp.bfloat16),
                   jax.ShapeDtypeStruct((f, d), jnp.bfloat16)],
        scratch_shapes=[
            pltpu.VMEM((A_HEAD_DIM, s), jnp.bfloat16),
            pltpu.VMEM((A_HEAD_DIM, s), jnp.bfloat16),
            pltpu.VMEM((s, A_HEAD_DIM), jnp.bfloat16),
            pltpu.VMEM((s, A_HEAD_DIM), jnp.bfloat16),
            pltpu.VMEM((A_VT_ROWS, s), jnp.bfloat16),
        ],
        compiler_params=_params(("parallel", "parallel")),
        name="attn_a",
    )(proj, proj, proj, lam, subln.reshape(-1, 1, A_HEAD_DIM), slopes, w_down)


def _attn_b_kernel(q_ref, k_ref, v_ref, u_ref, o_ref, kpad, vpad, table_ref, *, tq, pad):
    s_len = q_ref.shape[0]
    win = pad + tq
    width = u_ref.shape[1]

    @pl.when(pl.program_id(1) == 0)
    def _():
        rolled = pltpu.roll(jnp.broadcast_to(u_ref[...], (tq, width)), width - (tq - 1), 1,
                            stride=1, stride_axis=0)
        r = lax.broadcasted_iota(jnp.int32, (tq, win), 0)
        j = lax.broadcasted_iota(jnp.int32, (tq, win), 1)
        kc, qc = j // CHUNK, B_PAST_CHUNKS + r // CHUNK
        table_ref[...] = jnp.where((kc <= qc) & (kc >= qc - B_PAST_CHUNKS), rolled[:, :win], NEG)

    kpad[0:pad, :] = jnp.zeros((pad, B_HEAD_DIM), kpad.dtype)
    vpad[0:pad, :] = jnp.zeros((pad, B_HEAD_DIM), vpad.dtype)
    kpad[pad:pad + s_len, :] = k_ref[...]
    vpad[pad:pad + s_len, :] = v_ref[...]
    table = table_ref[...]
    jcol = lax.broadcasted_iota(jnp.int32, (tq, win), 1)
    scale = B_HEAD_DIM ** -0.5
    nq = s_len // tq

    def scores(i):
        qs = i * tq
        s = _dot_nt(q_ref[qs:qs + tq, :], kpad[qs:qs + win, :]) * scale + table
        if qs < pad:
            s = jnp.where(jcol >= pad - qs, s, NEG)
        return s

    nxt = scores(0)
    for i in range(nq):
        s = nxt
        if i + 1 < nq:
            nxt = scores(i + 1)
        qs = i * tq
        m = jnp.max(s, axis=1, keepdims=True)
        p = jnp.exp(s - m)
        l = jnp.sum(p, axis=1, keepdims=True)
        o = _dot(p.astype(jnp.bfloat16), vpad[qs:qs + win, :]) / l
        o_ref[qs:qs + tq, :] = o.astype(o_ref.dtype)


def _band_bias_vector(rel_bias, tq, pad):
    n = pad + 2 * tq - 1
    n_hi = pad + tq - REL_CLIP
    lo = REL_CLIP - tq + 1
    assert n_hi >= 0 and lo >= 0
    rb = rel_bias.astype(jnp.float32)
    u = jnp.concatenate([jnp.repeat(rb[:, 2 * REL_CLIP:], n_hi, axis=1), rb[:, lo:2 * REL_CLIP][:, ::-1]], axis=1)
    assert u.shape[1] == n
    return jnp.pad(u, ((0, 0), (0, 1)))[:, None, :]


def _attn_b(proj, rel_bias, tq=256):
    b, s, _ = proj.shape
    pad = B_PAST_CHUNKS * CHUNK
    width = pad + 2 * tq
    assert width & (width - 1) == 0
    u = _band_bias_vector(rel_bias, tq, pad)
    qi = OFF_B // B_HEAD_DIM
    ki = (OFF_B + B_WIDTH) // B_HEAD_DIM
    vi = (OFF_B + 2 * B_WIDTH) // B_HEAD_DIM

    def hspec(base):
        return pl.BlockSpec((None, s, B_HEAD_DIM), lambda h, bi, base=base: (bi, 0, base + h))

    return pl.pallas_call(
        functools.partial(_attn_b_kernel, tq=tq, pad=pad),
        grid=(B_HEADS, b),
        in_specs=[
            hspec(qi), hspec(ki), hspec(vi),
            pl.BlockSpec((None, 1, width), lambda h, bi: (h, 0, 0)),
        ],
        out_specs=pl.BlockSpec((None, s, B_HEAD_DIM), lambda h, bi: (bi, 0, h)),
        out_shape=jax.ShapeDtypeStruct((b, s, B_WIDTH), jnp.bfloat16),
        scratch_shapes=[
            pltpu.VMEM((pad + s, B_HEAD_DIM), jnp.bfloat16),
            pltpu.VMEM((pad + s, B_HEAD_DIM), jnp.bfloat16),
            pltpu.VMEM((tq, pad + tq), jnp.float32),
        ],
        compiler_params=_params(("arbitrary", "arbitrary")),
        name="attn_b",
    )(proj, proj, proj, u)


def _cumsum_rows(x):
    n = x.shape[0]
    row = lax.broadcasted_iota(jnp.int32, x.shape, 0)
    sh = 1
    while sh < n:
        x = x + jnp.where(row >= sh, pltpu.roll(x, sh, 0), 0.0)
        sh *= 2
    return x


def _mlstm_kernel(q_ref, k_ref, v_ref, og_ref, gate_ref, cq_ref, ck_ref, ng_ref, wo_ref, o_ref, wo16_ref,
                  qs_ref, ks_ref, state_ref, gt_ref, *, chunk, halo):
    wo16_ref[...] = wo_ref[...].astype(jnp.bfloat16)
    s_len = q_ref.shape[0]
    n_chunks = s_len // chunk
    pair = pl.program_id(1)
    lane = lax.broadcasted_iota(jnp.int32, (chunk, C_WIN), 1)
    glane = lax.broadcasted_iota(jnp.int32, (chunk, GATE_LANES), 1)
    trow = lax.broadcasted_iota(jnp.int32, (chunk, chunk), 0)
    tcol = lax.broadcasted_iota(jnp.int32, (chunk, chunk), 1)
    causal = tcol <= trow
    windows = ((0, lane < C_HEAD_DIM, C_HEAD_DIM),
               (C_PAIR - C_WIN, lane >= C_WIN - C_HEAD_DIM, 0))

    def conv_chunk(c, carry):
        st = pl.multiple_of(c * chunk, chunk)
        prev_st = pl.multiple_of(jnp.maximum(st - halo, 0), halo)
        keep = jnp.where(c > 0, 1.0, 0.0)
        for src, w_ref, dst, post in ((q_ref, cq_ref, qs_ref, 1.0),
                                      (k_ref, ck_ref, ks_ref, C_HEAD_DIM ** -0.5)):
            cur = src[pl.ds(st, chunk), :].astype(jnp.float32)
            prev = src[pl.ds(prev_st, halo), :].astype(jnp.float32) * keep
            xc = jnp.concatenate([prev, cur], axis=0)
            w = w_ref[...]
            y = w[CONV_K - 1:CONV_K, :] * cur
            for back in range(1, CONV_K):
                y = y + w[CONV_K - 1 - back:CONV_K - back, :] * pltpu.roll(xc, back, 0)[halo:, :]
            y = y * _sigmoid(y) * post
            dst[pl.ds(st, chunk), :] = y.astype(dst.dtype)
        return carry

    lax.fori_loop(0, n_chunks, conv_chunk, 0)
    state_ref[...] = jnp.zeros(state_ref.shape, jnp.float32)

    def step(c, ms):
        st = pl.multiple_of(c * chunk, chunk)
        g = gate_ref[pl.ds(st, chunk), :]
        bcum = _cumsum_rows(_log_sigmoid(g))
        gb = jnp.where(glane < C_HEADS, g, bcum)
        gt_ref[...] = gb.T
        ng = ng_ref[...]
        zero = jnp.zeros((), jnp.bfloat16)
        ops, qk, qc = [], [], []
        for a, (w0, valid, spare) in enumerate(windows):
            wcols = slice(w0, w0 + C_WIN)
            q = jnp.where(valid, qs_ref[pl.ds(st, chunk), wcols], zero)
            k = jnp.where(valid, ks_ref[pl.ds(st, chunk), wcols], zero)
            v = jnp.where(valid, v_ref[pl.ds(st, chunk), wcols], zero)
            v_ext = jnp.where(lane == spare, jnp.ones((), v.dtype), v)
            ops.append((q, k, v_ext))
            qk.append(_dot_nt(q, k))
            qc.append(_dot(q, state_ref[a].astype(jnp.bfloat16)))

        gate, m_out = [], []
        for a in range(2):
            h = 2 * pair + a
            m = ms[a]
            i_row = gt_ref[pl.ds(h, 1), :]
            b_row = gt_ref[pl.ds(C_HEADS + h, 1), :]
            i_col = jnp.sum(jnp.where(glane == h, gb, 0.0), axis=1, keepdims=True)
            b_col = jnp.sum(jnp.where(glane == C_HEADS + h, gb, 0.0), axis=1, keepdims=True)
            d = jnp.where(causal, b_col + (i_row - b_row), NEG)
            inter = b_col + m
            m_t = jnp.maximum(inter, jnp.max(d, axis=1, keepdims=True))
            b_last = b_col[chunk - 1:chunk, :]
            g_col = b_last - b_col + i_col
            m_new = jnp.maximum(b_last + m, jnp.max(g_col, axis=0, keepdims=True))
            gate.append((d, inter, m_t, jnp.exp(g_col - m_new), jnp.exp(b_last + m - m_new)))
            m_out.append(m_new)

        nds = []
        for a in range(2):
            d, inter, m_t = gate[a][:3]
            sw = qk[a] * jnp.exp(d - m_t)
            nds.append(_dot(sw.astype(jnp.bfloat16), ops[a][2]) + jnp.exp(inter - m_t) * qc[a])

        for a in range(2):
            k, v_ext = ops[a][1:]
            wk, decay = gate[a][3:]
            kw_t = (k.astype(jnp.float32) * wk).T.astype(jnp.bfloat16)
            state_ref[a] = decay * state_ref[a] + _dot(kw_t, v_ext)

        hn = []
        for a, (w0, valid, spare) in enumerate(windows):
            nd, m_t = nds[a], gate[a][2]
            den = jnp.sum(jnp.where(lane == spare, nd, 0.0), axis=1, keepdims=True)
            num = jnp.where(valid, nd, 0.0)
            hh = num / jnp.maximum(jnp.abs(den), jnp.exp(-m_t))
            msq = jnp.sum(hh * hh, axis=1, keepdims=True) * (1.0 / C_HEAD_DIM)
            hn.append(hh * lax.rsqrt(msq + 1e-6) * ng[:, w0:w0 + C_WIN])

        mid = C_PAIR - C_WIN
        hn_pair = jnp.concatenate([hn[0][:, :mid], hn[0][:, mid:] + hn[1][:, :C_WIN - mid], hn[1][:, C_WIN - mid:]],
                                  axis=1)
        og = og_ref[pl.ds(st, chunk), :].astype(jnp.float32)
        o_ref[pl.ds(st, chunk), :] = (_sigmoid(og) * hn_pair).astype(o_ref.dtype)
        return tuple(m_out)

    m0 = jnp.zeros((1, 1), jnp.float32)
    lax.fori_loop(0, n_chunks, step, (m0, m0), unroll=2)


def _mlstm(proj, gates, conv_w, norm_g, w_out, li, chunk=256, halo=16):
    b, s, _ = proj.shape
    chunk = min(chunk, s)
    n_pairs = C_HEADS // 2
    base = OFF_C // C_PAIR
    d_in, d = w_out.shape[1:]
    slab = d_in // (b * n_pairs)
    assert slab * b * n_pairs == d_in and slab % 16 == 0

    def pspec(t):
        return pl.BlockSpec((None, s, C_PAIR), lambda bi, p, t=t: (bi, 0, base + n_pairs * t + p))

    return pl.pallas_call(
        functools.partial(_mlstm_kernel, chunk=chunk, halo=halo),
        grid=(b, n_pairs),
        in_specs=[
            pspec(0), pspec(1), pspec(2), pspec(3),
            pl.BlockSpec((None, s, GATE_LANES), lambda bi, p: (bi, 0, 0)),
            pl.BlockSpec((None, CONV_K, C_PAIR), lambda bi, p: (li, 0, p)),
            pl.BlockSpec((None, CONV_K, C_PAIR), lambda bi, p: (li, 0, n_pairs + p)),
            pl.BlockSpec((1, C_PAIR), lambda bi, p: (0, 0)),
            pl.BlockSpec((None, slab, d), lambda bi, p: (li, bi * n_pairs + p, 0)),
        ],
        out_specs=[pl.BlockSpec((None, s, C_PAIR), lambda bi, p: (bi, 0, p)),
                   pl.BlockSpec((slab, d), lambda bi, p: (bi * n_pairs + p, 0))],
        out_shape=[jax.ShapeDtypeStruct((b, s, C_WIDTH), jnp.bfloat16),
                   jax.ShapeDtypeStruct((d_in, d), jnp.bfloat16)],
        scratch_shapes=[
            pltpu.VMEM((s, C_PAIR), jnp.bfloat16),
            pltpu.VMEM((s, C_PAIR), jnp.bfloat16),
            pltpu.VMEM((2, C_WIN, C_WIN), jnp.float32),
            pltpu.VMEM((GATE_LANES, chunk), jnp.float32),
        ],
        compiler_params=_params(("parallel", "parallel")),
        name="mlstm",
    )(proj, proj, proj, proj, gates, conv_w, conv_w, norm_g, w_out)


def _out_proj_ln_kernel(ya_ref, yb_ref, yc_ref, w16_ref, x_ref, g_ref, b_ref, o32_ref, o16_ref, *, alpha):
    half = x_ref.shape[0] // 2

    def project(r):
        rows = slice(r * half, (r + 1) * half)
        mixed = jnp.concatenate([ya_ref[rows, :], yb_ref[rows, :], yc_ref[rows, :]], axis=1)
        return _dot(mixed, w16_ref[...])

    nxt = project(0)
    for r in range(2):
        mix = nxt
        if r == 0:
            nxt = project(1)
        rows = slice(r * half, (r + 1) * half)
        y = _layer_norm(alpha * x_ref[rows, :] + mix, g_ref[...], b_ref[...])
        o32_ref[rows, :] = y
        o16_ref[rows, :] = y.astype(o16_ref.dtype)


def _out_proj_ln(ya, yb, yc, wo16, x, g, b, li, alpha, tm=512):
    m, d = x.shape
    tm = min(tm, m)
    row = lambda w: pl.BlockSpec((tm, w), lambda i: (i, 0))
    vec = pl.BlockSpec((None, 1, d), lambda i: (li, 0, 0))
    return pl.pallas_call(
        functools.partial(_out_proj_ln_kernel, alpha=alpha),
        grid=(m // tm,),
        in_specs=[row(ya.shape[1]), row(yb.shape[1]), row(yc.shape[1]),
                  pl.BlockSpec(wo16.shape, lambda i: (0, 0), pipeline_mode=pl.Buffered(1)),
                  row(d), vec, vec],
        out_specs=[row(d), row(d)],
        out_shape=[jax.ShapeDtypeStruct((m, d), jnp.float32),
                   jax.ShapeDtypeStruct((m, d), jnp.bfloat16)],
        compiler_params=_params(("parallel",)),
        name="out_proj_ln",
    )(ya, yb, yc, wo16, x, g, b)


FFN_SLABS = 4


def _ffn_up_kernel(x_ref, wg_ref, wu_ref, o_ref, wg16_ref, wu16_ref):
    @pl.when(pl.program_id(1) == 0)
    def _():
        wg16_ref[...] = wg_ref[...].astype(jnp.bfloat16)
        wu16_ref[...] = wu_ref[...].astype(jnp.bfloat16)

    slab = x_ref.shape[0] // FFN_SLABS

    def products(r):
        x = x_ref[r * slab:(r + 1) * slab, :]
        return _dot(x, wg16_ref[...]), _dot(x, wu16_ref[...])

    nxt = products(0)
    for r in range(FFN_SLABS):
        a, u = nxt
        if r + 1 < FFN_SLABS:
            nxt = products(r + 1)
        o_ref[r * slab:(r + 1) * slab, :] = (a * _sigmoid(a) * u).astype(o_ref.dtype)


def _ffn_up(xb, w_gate, w_up, li, tm=2048, tn=512):
    m, k = xb.shape
    n = w_gate.shape[2]
    tm, tn = min(tm, m), min(tn, n)
    return pl.pallas_call(
        _ffn_up_kernel,
        grid=(n // tn, m // tm),
        in_specs=[
            pl.BlockSpec((tm, k), lambda j, i: (i, 0)),
            pl.BlockSpec((None, k, tn), lambda j, i: (li, 0, j)),
            pl.BlockSpec((None, k, tn), lambda j, i: (li, 0, j)),
        ],
        out_specs=pl.BlockSpec((tm, tn), lambda j, i: (i, j)),
        out_shape=jax.ShapeDtypeStruct((m, n), jnp.bfloat16),
        scratch_shapes=[pltpu.VMEM((k, tn), jnp.bfloat16), pltpu.VMEM((k, tn), jnp.bfloat16)],
        compiler_params=_params(("arbitrary", "arbitrary")),
        name="ffn_up",
    )(xb, w_gate, w_up)


def _ffn_down_ln_kernel(h_ref, w_ref, x_ref, g_ref, b_ref, *rest, alpha, next_gates):
    if next_gates:
        wg_ref, gb_ref, o32_ref, o16_ref, gate_ref = rest
    else:
        (o32_ref,) = rest
    kk = pl.program_id(1)

    @pl.when(kk == 0)
    def _():
        o32_ref[...] = jnp.zeros(o32_ref.shape, jnp.float32)

    o32_ref[...] += _dot(h_ref[...], w_ref[...])

    @pl.when(kk == pl.num_programs(1) - 1)
    def _():
        y = _layer_norm(alpha * x_ref[...] + o32_ref[...], g_ref[...], b_ref[...])
        o32_ref[...] = y
        if next_gates:
            y16 = y.astype(jnp.bfloat16)
            o16_ref[...] = y16
            gate_ref[...] = _gate_proj(y16, wg_ref, gb_ref)


def _ffn_down_ln(hid, wd16, x, g, b, li, alpha, w_in=None, gb_next=None, tm=512, tk=2816):
    m, f = hid.shape
    d = wd16.shape[1]
    tm, tk = min(tm, m), min(tk, f)
    next_gates = w_in is not None
    vec = pl.BlockSpec((None, 1, d), lambda i, kk: (li, 0, 0))
    row = pl.BlockSpec((tm, d), lambda i, kk: (i, 0))
    in_specs = [pl.BlockSpec((tm, tk), lambda i, kk: (i, kk)),
                pl.BlockSpec((tk, d), lambda i, kk: (kk, 0)),
                row, vec, vec]
    out_specs = [row]
    out_shape = [jax.ShapeDtypeStruct((m, d), jnp.float32)]
    args = [hid, wd16, x, g, b]
    if next_gates:
        in_specs += [_gate_weight_spec(li + 1, d), pl.BlockSpec((1, GATE_LANES), lambda i, kk: (0, 0))]
        out_specs += [row, pl.BlockSpec((tm, GATE_LANES), lambda i, kk: (i, 0))]
        out_shape += [jax.ShapeDtypeStruct((m, d), jnp.bfloat16),
                      jax.ShapeDtypeStruct((m, GATE_LANES), jnp.float32)]
        args += [w_in, gb_next]
    return pl.pallas_call(
        functools.partial(_ffn_down_ln_kernel, alpha=alpha, next_gates=next_gates),
        grid=(m // tm, f // tk),
        in_specs=in_specs,
        out_specs=out_specs,
        out_shape=out_shape,
        compiler_params=_params(("parallel", "arbitrary")),
        name="ffn_down_ln",
    )(*args)


def _gate_bias_tile(gate_bias, li):
    return jnp.pad(gate_bias[li], (0, GATE_LANES - 2 * C_HEADS)).reshape(1, GATE_LANES)


def kernel(x, w_in, gate_bias, conv_w, lam, subln_a, norm_c, rel_bias, w_out, ln1_g, ln1_b,
           w_gate, w_up, w_down, ln2_g, ln2_b):
    bsz, s_len, d = x.shape
    depth = w_in.shape[0]
    alpha = (2.0 * depth) ** 0.25
    m = bsz * s_len
    x32 = x.reshape(m, d)
    ln1_g, ln1_b, ln2_g, ln2_b = (p.reshape(depth, 1, d) for p in (ln1_g, ln1_b, ln2_g, ln2_b))
    w_in_t = jnp.swapaxes(w_in, 1, 2)
    x16, gates = _cast_gate(x32, w_in_t, _gate_bias_tile(gate_bias, 0), 0)
    for li in range(depth):
        proj = _in_proj(x16, w_in_t, li).reshape(bsz, s_len, OFF_G)
        gates = gates.reshape(bsz, s_len, GATE_LANES)
        ya, wd16 = _attn_a(proj, lam, subln_a, w_down, li)
        yb = _attn_b(proj, rel_bias[li])
        norm_g = jnp.tile(norm_c[li], 2).reshape(1, C_PAIR)
        yc, wo16 = _mlstm(proj, gates, conv_w, norm_g, w_out, li)
        x32, x16 = _out_proj_ln(ya.reshape(m, A_WIDTH), yb.reshape(m, B_WIDTH), yc.reshape(m, C_WIDTH),
                                wo16, x32, ln1_g, ln1_b, li, alpha)
        hid = _ffn_up(x16, w_gate, w_up, li)
        if li + 1 < depth:
            x32, x16, gates = _ffn_down_ln(hid, wd16, x32, ln2_g, ln2_b, li, alpha,
                                           w_in_t, _gate_bias_tile(gate_bias, li + 1))
        else:
            (x32,) = _ffn_down_ln(hid, wd16, x32, ln2_g, ln2_b, li, alpha)
    return x32.reshape(bsz, s_len, d)
```

```python
import functools
import math

import numpy as np
import jax
import jax.numpy as jnp
from jax import lax
from jax.experimental import pallas as pl
from jax.experimental.pallas import tpu as pltpu

CHUNK = 64
A_HEADS = 4
A_HEAD_DIM = 128
A_HALF = 64
B_HEADS = 6
B_HEAD_DIM = 128
B_PAST_CHUNKS = 8
REL_CLIP = 256
C_HEADS = 4
C_HEAD_DIM = 192
C_PAIR = 2 * C_HEAD_DIM
C_WIN = 256
CONV_K = 4
NEG = -1e30
GATE_LANES = 128

A_WIDTH = A_HEADS * A_HEAD_DIM
B_WIDTH = B_HEADS * B_HEAD_DIM
C_WIDTH = C_HEADS * C_HEAD_DIM
OFF_A = 0
OFF_B = 3 * A_WIDTH
OFF_C = OFF_B + 3 * B_WIDTH
OFF_G = OFF_C + 4 * C_WIDTH
PROJ_TILE = 768

A_VT_ROWS = A_HEAD_DIM + 16
LOG2E = math.log2(math.e)

VMEM_LIMIT = 56 * 1024 * 1024

_NT = (((1,), (1,)), ((), ()))


def _params(sem, flags=None):
    return pltpu.CompilerParams(dimension_semantics=sem, vmem_limit_bytes=VMEM_LIMIT, flags=flags)


def _dot(a, b):
    return jnp.dot(a, b, preferred_element_type=jnp.float32)


def _dot_nt(a, b):
    return lax.dot_general(a, b, _NT, preferred_element_type=jnp.float32)


def _log_sigmoid(x):
    return jnp.minimum(x, 0.0) - jnp.log(1.0 + jnp.exp(-jnp.abs(x)))


def _sigmoid(x):
    return 1.0 / (1.0 + jnp.exp(-x))


def _layer_norm(z, g, b):
    mu = jnp.mean(z, axis=1, keepdims=True)
    zc = z - mu
    var = jnp.mean(zc * zc, axis=1, keepdims=True)
    return zc * lax.rsqrt(var + 1e-5) * g + b


N_GATES = 2 * C_HEADS


def _gate_weight_spec(li, d):
    assert OFF_G % N_GATES == 0
    return pl.BlockSpec((None, N_GATES, d), lambda *_: (li, OFF_G // N_GATES, 0))


def _gate_proj(x16, wg_ref, gb_ref):
    wg = wg_ref[...].astype(jnp.bfloat16)
    wg = jnp.concatenate([wg, jnp.zeros((GATE_LANES - N_GATES, wg.shape[1]), wg.dtype)], axis=0)
    return _dot_nt(x16, wg) + gb_ref[...]


def _cast_gate_kernel(x_ref, wg_ref, gb_ref, x16_ref, g_ref):
    x16 = x_ref[...].astype(jnp.bfloat16)
    x16_ref[...] = x16
    g_ref[...] = _gate_proj(x16, wg_ref, gb_ref)


def _cast_gate(x, w_in_t, gb, li, tm=1024):
    m, d = x.shape
    tm = min(tm, m)
    return pl.pallas_call(
        _cast_gate_kernel,
        grid=(m // tm,),
        in_specs=[
            pl.BlockSpec((tm, d), lambda i: (i, 0)),
            _gate_weight_spec(li, d),
            pl.BlockSpec((1, GATE_LANES), lambda i: (0, 0)),
        ],
        out_specs=[pl.BlockSpec((tm, d), lambda i: (i, 0)),
                   pl.BlockSpec((tm, GATE_LANES), lambda i: (i, 0))],
        out_shape=[jax.ShapeDtypeStruct((m, d), jnp.bfloat16),
                   jax.ShapeDtypeStruct((m, GATE_LANES), jnp.float32)],
        compiler_params=_params(("parallel",)),
        name="cast_gate",
    )(x, w_in_t, gb)


def _in_proj_kernel(x_ref, w_ref, o_ref, w16_ref):
    @pl.when(pl.program_id(1) == 0)
    def _():
        w16_ref[...] = w_ref[...].astype(jnp.bfloat16)

    o_ref[...] = _dot_nt(x_ref[...], w16_ref[...]).astype(o_ref.dtype)


def _in_proj(xb, w_in_t, li, tm=2048, tn=PROJ_TILE):
    m, k = xb.shape
    tm = min(tm, m)
    n = OFF_G
    return pl.pallas_call(
        _in_proj_kernel,
        grid=(n // tn, m // tm),
        in_specs=[
            pl.BlockSpec((tm, k), lambda j, i: (i, 0)),
            pl.BlockSpec((None, tn, k), lambda j, i: (li, j, 0)),
        ],
        out_specs=pl.BlockSpec((tm, tn), lambda j, i: (i, j)),
        out_shape=jax.ShapeDtypeStruct((m, n), jnp.bfloat16),
        scratch_shapes=[pltpu.VMEM((tn, k), jnp.bfloat16)],
        compiler_params=_params(("arbitrary", "arbitrary")),
        name="in_proj",
    )(xb, w_in_t)


def _attn_a_kernel(q_ref, k_ref, v_ref, lam_ref, g_ref, slope_ref, wd_ref, o_ref, wd16_ref,
                   qt1_ref, qt2_ref, k1_ref, k2_ref, vt_ref, *, tq, lam_init, ahead):
    wd16_ref[...] = wd_ref[...].astype(jnp.bfloat16)
    s_len = q_ref.shape[0]
    nq = s_len // tq
    slope2 = slope_ref[0:1, 0:1] * LOG2E
    lf = lam_ref[...]
    lam_full = (jnp.exp(jnp.sum(lf[0:1] * lf[1:2], axis=1, keepdims=True))
                - jnp.exp(jnp.sum(lf[2:3] * lf[3:4], axis=1, keepdims=True)) + lam_init)
    feat = lax.broadcasted_iota(jnp.int32, (A_HEAD_DIM, tq), 0)
    lane = lax.broadcasted_iota(jnp.int32, (tq, A_HEAD_DIM), 1)
    key_bias = slope2 * lax.broadcasted_iota(jnp.int32, (tq, A_HEAD_DIM), 0).astype(jnp.float32)
    kb_hi = key_bias.astype(jnp.bfloat16).astype(jnp.float32)
    kb_lo = (key_bias - kb_hi).astype(jnp.bfloat16).astype(jnp.float32)
    ones_row = jnp.where(lax.broadcasted_iota(jnp.int32, (A_VT_ROWS - A_HEAD_DIM, tq), 0) == 0, 1.0, 0.0)
    for i in range(nq):
        rows = slice(i * tq, (i + 1) * tq)
        qt = (q_ref[rows, :].astype(jnp.float32) * (A_HALF ** -0.5 * LOG2E)).T
        qt1_ref[:, rows] = jnp.where(feat < A_HALF, qt, jnp.where(feat < A_HALF + 2, 1.0, 0.0)).astype(jnp.bfloat16)
        qt2_ref[:, rows] = jnp.where(feat >= A_HALF, qt, jnp.where(feat < 2, 1.0, 0.0)).astype(jnp.bfloat16)
        k = k_ref[rows, :].astype(jnp.float32)
        k1 = jnp.where(lane < A_HALF, k, jnp.where(lane == A_HALF, kb_hi, jnp.where(lane == A_HALF + 1, kb_lo, 0.0)))
        k2 = jnp.where(lane >= A_HALF, k, jnp.where(lane == 0, kb_hi, jnp.where(lane == 1, kb_lo, 0.0)))
        k1_ref[rows, :] = k1.astype(jnp.bfloat16)
        k2_ref[rows, :] = k2.astype(jnp.bfloat16)
        vt_ref[0:A_HEAD_DIM, rows] = v_ref[rows, :].astype(jnp.float32).T.astype(jnp.bfloat16)
        vt_ref[A_HEAD_DIM:, rows] = ones_row.astype(jnp.bfloat16)

    key = lax.broadcasted_iota(jnp.int32, (tq, tq), 0)
    qry = lax.broadcasted_iota(jnp.int32, (tq, tq), 1)
    rel = (qry - key).astype(jnp.float32)
    bias_diag = slope2 * (rel - jnp.abs(rel))
    diag_ok = (key // CHUNK) <= (qry // CHUNK)

    def update2(t1, t2, shift_c, vt, st1, st2):
        ts, sts = (t1, t2), (st1, st2)
        m_new = [jnp.maximum(st[0], jnp.max(t, axis=0, keepdims=True) + shift_c) for t, st in zip(ts, sts)]
        alpha = [jnp.exp2(st[0] - mn) for st, mn in zip(sts, m_new)]
        p = [jnp.exp2(t - (mn - shift_c)) for t, mn in zip(ts, m_new)]
        pv = [_dot(vt, pp.astype(jnp.bfloat16)) for pp in p]
        acc = [a * st[1] + x for a, st, x in zip(alpha, sts, pv)]
        return (m_new[0], acc[0]), (m_new[1], acc[1])

    def scores(i, j):
        t1 = _dot(k1_ref[j * tq:(j + 1) * tq, :], qt1_ref[:, i * tq:(i + 1) * tq])
        t2 = _dot(k2_ref[j * tq:(j + 1) * tq, :], qt2_ref[:, i * tq:(i + 1) * tq])
        if j < i:
            return t1, t2
        return jnp.where(diag_ok, t1 + bias_diag, NEG), jnp.where(diag_ok, t2 + bias_diag, NEG)

    pairs = [(i, j) for i in range(nq) for j in range(i + 1)]
    pending = [scores(*pr) for pr in pairs[:ahead]]
    fresh = (jnp.full((1, tq), NEG, jnp.float32), jnp.zeros((A_VT_ROWS, tq), jnp.float32))
    for n, (i, j) in enumerate(pairs):
        if j == 0:
            st1 = st2 = fresh
        t1, t2 = pending.pop(0)
        if n + ahead < len(pairs):
            pending.append(scores(*pairs[n + ahead]))
        if j < i:
            shift_c = -slope2 * float((i - j) * tq)
        else:
            shift_c = jnp.zeros((1, 1), jnp.float32)
        st1, st2 = update2(t1, t2, shift_c, vt_ref[:, j * tq:(j + 1) * tq], st1, st2)
        if j == i:
            (num1, l1), (num2, l2) = [(st[1][:A_HEAD_DIM], st[1][A_HEAD_DIM:A_HEAD_DIM + 1]) for st in (st1, st2)]
            ot = num1 * (1.0 / l1) - (lam_full / l2) * num2
            o = ot.T
            ms = jnp.mean(o * o, axis=1, keepdims=True)
            o = o * lax.rsqrt(ms + 1e-6) * g_ref[...] * (1.0 - lam_init)
            o_ref[i * tq:(i + 1) * tq, :] = o.astype(o_ref.dtype)


def _attn_a(proj, lam, subln, w_down, li, tq=256, ahead=2):
    b, s, _ = proj.shape
    tq = min(tq, s)
    f, d = w_down.shape[1:]
    slab = f // (b * A_HEADS)
    assert slab * b * A_HEADS == f and slab % 16 == 0
    lam_init = 0.8 - 0.6 * math.exp(-0.3 * li)
    slopes = np.asarray([2.0 ** (-8.0 * (h + 1) / A_HEADS) for h in range(A_HEADS)], np.float32)
    slopes = jnp.asarray(np.broadcast_to(slopes[:, None, None], (A_HEADS, 1, 128)))
    qi = OFF_A // A_HEAD_DIM
    ki = (OFF_A + A_WIDTH) // A_HEAD_DIM
    vi = (OFF_A + 2 * A_WIDTH) // A_HEAD_DIM

    def hspec(base):
        return pl.BlockSpec((None, s, A_HEAD_DIM), lambda bi, h, base=base: (bi, 0, base + h))

    return pl.pallas_call(
        functools.partial(_attn_a_kernel, tq=tq, lam_init=lam_init, ahead=ahead),
        grid=(b, A_HEADS),
        in_specs=[
            hspec(qi), hspec(ki), hspec(vi),
            pl.BlockSpec((None, 4, A_HALF), lambda bi, h: (li, 0, 0)),
            pl.BlockSpec((None, 1, A_HEAD_DIM), lambda bi, h: (li, 0, 0)),
            pl.BlockSpec((None, 1, 128), lambda bi, h: (h, 0, 0)),
            pl.BlockSpec((None, slab, d), lambda bi, h: (li, bi * A_HEADS + h, 0)),
        ],
        out_specs=[pl.BlockSpec((None, s, A_HEAD_DIM), lambda bi, h: (bi, 0, h)),
                   pl.BlockSpec((slab, d), lambda bi, h: (bi * A_HEADS + h, 0))],
        out_shape=[jax.ShapeDtypeStruct((b, s, A_WIDTH), jnp.bfloat16),
                   jax.ShapeDtypeStruct((f, d), jnp.bfloat16)],
        scratch_shapes=[
            pltpu.VMEM((A_HEAD_DIM, s), jnp.bfloat16),
            pltpu.VMEM((A_HEAD_DIM, s), jnp.bfloat16),
            pltpu.VMEM((s, A_HEAD_DIM), jnp.bfloat16),
            pltpu.VMEM((s, A_HEAD_DIM), jnp.bfloat16),
            pltpu.VMEM((A_VT_ROWS, s), jnp.bfloat16),
        ],
        compiler_params=_params(("parallel", "parallel")),
        name="attn_a",
    )(proj, proj, proj, lam, subln.reshape(-1, 1, A_HEAD_DIM), slopes, w_down)


def _attn_b_kernel(q_ref, k_ref, v_ref, u_ref, o_ref, kpad, vpad, table_ref, *, tq, pad):
    s_len = q_ref.shape[0]
    win = pad + tq
    width = u_ref.shape[1]

    @pl.when(pl.program_id(1) == 0)
    def _():
        rolled = pltpu.roll(jnp.broadcast_to(u_ref[...], (tq, width)), width - (tq - 1), 1,
                            stride=1, stride_axis=0)
        r = lax.broadcasted_iota(jnp.int32, (tq, win), 0)
        j = lax.broadcasted_iota(jnp.int32, (tq, win), 1)
        kc, qc = j // CHUNK, B_PAST_CHUNKS + r // CHUNK
        table_ref[...] = jnp.where((kc <= qc) & (kc >= qc - B_PAST_CHUNKS), rolled[:, :win] * LOG2E, NEG)

    kpad[0:pad, :] = jnp.zeros((pad, B_HEAD_DIM), kpad.dtype)
    vpad[0:pad, :] = jnp.zeros((pad, B_HEAD_DIM), vpad.dtype)
    kpad[pad:pad + s_len, :] = k_ref[...]
    vpad[pad:pad + s_len, :] = v_ref[...]
    table = table_ref[...]
    jcol = lax.broadcasted_iota(jnp.int32, (tq, win), 1)
    scale2 = B_HEAD_DIM ** -0.5 * LOG2E
    nq = s_len // tq

    def scores(i):
        qs = i * tq
        q = (q_ref[qs:qs + tq, :].astype(jnp.float32) * scale2).astype(jnp.bfloat16)
        s = _dot_nt(q, kpad[qs:qs + win, :]) + table
        if qs < pad:
            s = jnp.where(jcol >= pad - qs, s, NEG)
        return s

    nxt = scores(0)
    for i in range(nq):
        s = nxt
        if i + 1 < nq:
            nxt = scores(i + 1)
        qs = i * tq
        m = jnp.max(s, axis=1, keepdims=True)
        p = jnp.exp2(s - m)
        l = jnp.sum(p, axis=1, keepdims=True)
        o = _dot(p.astype(jnp.bfloat16), vpad[qs:qs + win, :]) / l
        o_ref[qs:qs + tq, :] = o.astype(o_ref.dtype)


def _band_bias_vector(rel_bias, tq, pad):
    n = pad + 2 * tq - 1
    n_hi = pad + tq - REL_CLIP
    lo = REL_CLIP - tq + 1
    assert n_hi >= 0 and lo >= 0
    rb = rel_bias.astype(jnp.float32)
    u = jnp.concatenate([jnp.repeat(rb[:, 2 * REL_CLIP:], n_hi, axis=1), rb[:, lo:2 * REL_CLIP][:, ::-1]], axis=1)
    assert u.shape[1] == n
    return jnp.pad(u, ((0, 0), (0, 1)))[:, None, :]


def _attn_b(proj, rel_bias, tq=256):
    b, s, _ = proj.shape
    pad = B_PAST_CHUNKS * CHUNK
    width = pad + 2 * tq
    assert width & (width - 1) == 0
    u = _band_bias_vector(rel_bias, tq, pad)
    qi = OFF_B // B_HEAD_DIM
    ki = (OFF_B + B_WIDTH) // B_HEAD_DIM
    vi = (OFF_B + 2 * B_WIDTH) // B_HEAD_DIM

    def hspec(base):
        return pl.BlockSpec((None, s, B_HEAD_DIM), lambda h, bi, base=base: (bi, 0, base + h))

    return pl.pallas_call(
        functools.partial(_attn_b_kernel, tq=tq, pad=pad),
        grid=(B_HEADS, b),
        in_specs=[
            hspec(qi), hspec(ki), hspec(vi),
            pl.BlockSpec((None, 1, width), lambda h, bi: (h, 0, 0)),
        ],
        out_specs=pl.BlockSpec((None, s, B_HEAD_DIM), lambda h, bi: (bi, 0, h)),
        out_shape=jax.ShapeDtypeStruct((b, s, B_WIDTH), jnp.bfloat16),
        scratch_shapes=[
            pltpu.VMEM((pad + s, B_HEAD_DIM), jnp.bfloat16),
            pltpu.VMEM((pad + s, B_HEAD_DIM), jnp.bfloat16),
            pltpu.VMEM((tq, pad + tq), jnp.float32),
        ],
        compiler_params=_params(("arbitrary", "arbitrary")),
        name="attn_b",
    )(proj, proj, proj, u)


def _cumsum_rows(x):
    n = x.shape[0]
    row = lax.broadcasted_iota(jnp.int32, x.shape, 0)
    sh = 1
    while sh < n:
        x = x + jnp.where(row >= sh, pltpu.roll(x, sh, 0), 0.0)
        sh *= 2
    return x


def _mlstm_kernel(q_ref, k_ref, v_ref, og_ref, gate_ref, cq_ref, ck_ref, ng_ref, wo_ref, o_ref, wo16_ref,
                  qs_ref, ks_ref, state_ref, gt_ref, *, chunk, halo):
    wo16_ref[...] = wo_ref[...].astype(jnp.bfloat16)
    s_len = q_ref.shape[0]
    n_chunks = s_len // chunk
    pair = pl.program_id(1)
    lane = lax.broadcasted_iota(jnp.int32, (chunk, C_WIN), 1)
    glane = lax.broadcasted_iota(jnp.int32, (chunk, GATE_LANES), 1)
    trow = lax.broadcasted_iota(jnp.int32, (chunk, chunk), 0)
    tcol = lax.broadcasted_iota(jnp.int32, (chunk, chunk), 1)
    causal = tcol <= trow
    windows = ((0, lane < C_HEAD_DIM, C_HEAD_DIM),
               (C_PAIR - C_WIN, lane >= C_WIN - C_HEAD_DIM, 0))

    def conv_chunk(c, carry):
        st = pl.multiple_of(c * chunk, chunk)
        prev_st = pl.multiple_of(jnp.maximum(st - halo, 0), halo)
        keep = jnp.where(c > 0, 1.0, 0.0)
        for src, w_ref, dst, post in ((q_ref, cq_ref, qs_ref, 1.0),
                                      (k_ref, ck_ref, ks_ref, C_HEAD_DIM ** -0.5)):
            cur = src[pl.ds(st, chunk), :].astype(jnp.float32)
            prev = src[pl.ds(prev_st, halo), :].astype(jnp.float32) * keep
            xc = jnp.concatenate([prev, cur], axis=0)
            w = w_ref[...]
            y = w[CONV_K - 1:CONV_K, :] * cur
            for back in range(1, CONV_K):
                y = y + w[CONV_K - 1 - back:CONV_K - back, :] * pltpu.roll(xc, back, 0)[halo:, :]
            y = y * _sigmoid(y) * post
            dst[pl.ds(st, chunk), :] = y.astype(dst.dtype)
        return carry

    lax.fori_loop(0, n_chunks, conv_chunk, 0)
    state_ref[...] = jnp.zeros(state_ref.shape, jnp.float32)

    def step(c, ms):
        st = pl.multiple_of(c * chunk, chunk)
        g = gate_ref[pl.ds(st, chunk), :]
        bcum = _cumsum_rows(_log_sigmoid(g))
        gb = jnp.where(glane < C_HEADS, g, bcum)
        gt_ref[...] = gb.T
        ng = ng_ref[...]
        zero = jnp.zeros((), jnp.bfloat16)
        ops, qk, qc = [], [], []
        for a, (w0, valid, spare) in enumerate(windows):
            wcols = slice(w0, w0 + C_WIN)
            q = jnp.where(valid, qs_ref[pl.ds(st, chunk), wcols], zero)
            k = jnp.where(valid, ks_ref[pl.ds(st, chunk), wcols], zero)
            v = jnp.where(valid, v_ref[pl.ds(st, chunk), wcols], zero)
            v_ext = jnp.where(lane == spare, jnp.ones((), v.dtype), v)
            ops.append((q, k, v_ext))
            qk.append(_dot_nt(q, k))
            qc.append(_dot(q, state_ref[a].astype(jnp.bfloat16)))

        gate, m_out = [], []
        for a in range(2):
            h = 2 * pair + a
            m = ms[a]
            i_row = gt_ref[pl.ds(h, 1), :]
            b_row = gt_ref[pl.ds(C_HEADS + h, 1), :]
            i_col = jnp.sum(jnp.where(glane == h, gb, 0.0), axis=1, keepdims=True)
            b_col = jnp.sum(jnp.where(glane == C_HEADS + h, gb, 0.0), axis=1, keepdims=True)
            d = jnp.where(causal, b_col + (i_row - b_row), NEG)
            inter = b_col + m
            m_t = jnp.maximum(inter, jnp.max(d, axis=1, keepdims=True))
            b_last = b_col[chunk - 1:chunk, :]
            g_col = b_last - b_col + i_col
            m_new = jnp.maximum(b_last + m, jnp.max(g_col, axis=0, keepdims=True))
            gate.append((d, inter, m_t, jnp.exp(g_col - m_new), jnp.exp(b_last + m - m_new)))
            m_out.append(m_new)

        nds = []
        for a in range(2):
            d, inter, m_t = gate[a][:3]
            sw = qk[a] * jnp.exp(d - m_t)
            nds.append(_dot(sw.astype(jnp.bfloat16), ops[a][2]) + jnp.exp(inter - m_t) * qc[a])

        for a in range(2):
            k, v_ext = ops[a][1:]
            wk, decay = gate[a][3:]
            kw_t = (k.astype(jnp.float32) * wk).T.astype(jnp.bfloat16)
            state_ref[a] = decay * state_ref[a] + _dot(kw_t, v_ext)

        hn = []
        for a, (w0, valid, spare) in enumerate(windows):
            nd, m_t = nds[a], gate[a][2]
            den = jnp.sum(jnp.where(lane == spare, nd, 0.0), axis=1, keepdims=True)
            num = jnp.where(valid, nd, 0.0)
            hh = num / jnp.maximum(jnp.abs(den), jnp.exp(-m_t))
            msq = jnp.sum(hh * hh, axis=1, keepdims=True) * (1.0 / C_HEAD_DIM)
            hn.append(hh * lax.rsqrt(msq + 1e-6) * ng[:, w0:w0 + C_WIN])

        mid = C_PAIR - C_WIN
        hn_pair = jnp.concatenate([hn[0][:, :mid], hn[0][:, mid:] + hn[1][:, :C_WIN - mid], hn[1][:, C_WIN - mid:]],
                                  axis=1)
        og = og_ref[pl.ds(st, chunk), :].astype(jnp.float32)
        o_ref[pl.ds(st, chunk), :] = (_sigmoid(og) * hn_pair).astype(o_ref.dtype)
        return tuple(m_out)

    m0 = jnp.zeros((1, 1), jnp.float32)
    lax.fori_loop(0, n_chunks, step, (m0, m0), unroll=2)


def _mlstm(proj, gates, conv_w, norm_g, w_out, li, chunk=256, halo=16):
    b, s, _ = proj.shape
    chunk = min(chunk, s)
    n_pairs = C_HEADS // 2
    base = OFF_C // C_PAIR
    d_in, d = w_out.shape[1:]
    slab = d_in // (b * n_pairs)
    assert slab * b * n_pairs == d_in and slab % 16 == 0

    def pspec(t):
        return pl.BlockSpec((None, s, C_PAIR), lambda bi, p, t=t: (bi, 0, base + n_pairs * t + p))

    return pl.pallas_call(
        functools.partial(_mlstm_kernel, chunk=chunk, halo=halo),
        grid=(b, n_pairs),
        in_specs=[
            pspec(0), pspec(1), pspec(2), pspec(3),
            pl.BlockSpec((None, s, GATE_LANES), lambda bi, p: (bi, 0, 0)),
            pl.BlockSpec((None, CONV_K, C_PAIR), lambda bi, p: (li, 0, p)),
            pl.BlockSpec((None, CONV_K, C_PAIR), lambda bi, p: (li, 0, n_pairs + p)),
            pl.BlockSpec((1, C_PAIR), lambda bi, p: (0, 0)),
            pl.BlockSpec((None, slab, d), lambda bi, p: (li, bi * n_pairs + p, 0)),
        ],
        out_specs=[pl.BlockSpec((None, s, C_PAIR), lambda bi, p: (bi, 0, p)),
                   pl.BlockSpec((slab, d), lambda bi, p: (bi * n_pairs + p, 0))],
        out_shape=[jax.ShapeDtypeStruct((b, s, C_WIDTH), jnp.bfloat16),
                   jax.ShapeDtypeStruct((d_in, d), jnp.bfloat16)],
        scratch_shapes=[
            pltpu.VMEM((s, C_PAIR), jnp.bfloat16),
            pltpu.VMEM((s, C_PAIR), jnp.bfloat16),
            pltpu.VMEM((2, C_WIN, C_WIN), jnp.float32),
            pltpu.VMEM((GATE_LANES, chunk), jnp.float32),
        ],
        compiler_params=_params(("parallel", "parallel")),
        name="mlstm",
    )(proj, proj, proj, proj, gates, conv_w, conv_w, norm_g, w_out)


def _out_proj_ln_kernel(ya_ref, yb_ref, yc_ref, w16_ref, x_ref, g_ref, b_ref, o32_ref, o16_ref, *, alpha):
    half = x_ref.shape[0] // 2

    def project(r):
        rows = slice(r * half, (r + 1) * half)
        mixed = jnp.concatenate([ya_ref[rows, :], yb_ref[rows, :], yc_ref[rows, :]], axis=1)
        return _dot(mixed, w16_ref[...])

    nxt = project(0)
    for r in range(2):
        mix = nxt
        if r == 0:
            nxt = project(1)
        rows = slice(r * half, (r + 1) * half)
        y = _layer_norm(alpha * x_ref[rows, :] + mix, g_ref[...], b_ref[...])
        o32_ref[rows, :] = y
        o16_ref[rows, :] = y.astype(o16_ref.dtype)


def _out_proj_ln(ya, yb, yc, wo16, x, g, b, li, alpha, tm=512):
    m, d = x.shape
    tm = min(tm, m)
    row = lambda w: pl.BlockSpec((tm, w), lambda i: (i, 0))
    vec = pl.BlockSpec((None, 1, d), lambda i: (li, 0, 0))
    return pl.pallas_call(
        functools.partial(_out_proj_ln_kernel, alpha=alpha),
        grid=(m // tm,),
        in_specs=[row(ya.shape[1]), row(yb.shape[1]), row(yc.shape[1]),
                  pl.BlockSpec(wo16.shape, lambda i: (0, 0), pipeline_mode=pl.Buffered(1)),
                  row(d), vec, vec],
        out_specs=[row(d), row(d)],
        out_shape=[jax.ShapeDtypeStruct((m, d), jnp.float32),
                   jax.ShapeDtypeStruct((m, d), jnp.bfloat16)],
        compiler_params=_params(("parallel",)),
        name="out_proj_ln",
    )(ya, yb, yc, wo16, x, g, b)


FFN_SLABS = 4


def _ffn_up_kernel(x_ref, wg_ref, wu_ref, o_ref, wg16_ref, wu16_ref):
    @pl.when(pl.program_id(1) == 0)
    def _():
        wg16_ref[...] = wg_ref[...].astype(jnp.bfloat16)
        wu16_ref[...] = wu_ref[...].astype(jnp.bfloat16)

    slab = x_ref.shape[0] // FFN_SLABS

    def products(r):
        x = x_ref[r * slab:(r + 1) * slab, :]
        return _dot(x, wg16_ref[...]), _dot(x, wu16_ref[...])

    nxt = products(0)
    for r in range(FFN_SLABS):
        a, u = nxt
        if r + 1 < FFN_SLABS:
            nxt = products(r + 1)
        o_ref[r * slab:(r + 1) * slab, :] = (a * _sigmoid(a) * u).astype(o_ref.dtype)


def _ffn_up(xb, w_gate, w_up, li, tm=2048, tn=512):
    m, k = xb.shape
    n = w_gate.shape[2]
    tm, tn = min(tm, m), min(tn, n)
    return pl.pallas_call(
        _ffn_up_kernel,
        grid=(n // tn, m // tm),
        in_specs=[
            pl.BlockSpec((tm, k), lambda j, i: (i, 0)),
            pl.BlockSpec((None, k, tn), lambda j, i: (li, 0, j)),
            pl.BlockSpec((None, k, tn), lambda j, i: (li, 0, j)),
        ],
        out_specs=pl.BlockSpec((tm, tn), lambda j, i: (i, j)),
        out_shape=jax.ShapeDtypeStruct((m, n), jnp.bfloat16),
        scratch_shapes=[pltpu.VMEM((k, tn), jnp.bfloat16), pltpu.VMEM((k, tn), jnp.bfloat16)],
        compiler_params=_params(("arbitrary", "arbitrary")),
        name="ffn_up",
    )(xb, w_gate, w_up)


def _ffn_down_ln_kernel(h_ref, w_ref, x_ref, g_ref, b_ref, *rest, alpha, next_gates):
    if next_gates:
        wg_ref, gb_ref, o32_ref, o16_ref, gate_ref = rest
    else:
        (o32_ref,) = rest
    kk = pl.program_id(1)

    @pl.when(kk == 0)
    def _():
        o32_ref[...] = jnp.zeros(o32_ref.shape, jnp.float32)

    o32_ref[...] += _dot(h_ref[...], w_ref[...])

    @pl.when(kk == pl.num_programs(1) - 1)
    def _():
        y = _layer_norm(alpha * x_ref[...] + o32_ref[...], g_ref[...], b_ref[...])
        o32_ref[...] = y
        if next_gates:
            y16 = y.astype(jnp.bfloat16)
            o16_ref[...] = y16
            gate_ref[...] = _gate_proj(y16, wg_ref, gb_ref)


def _ffn_down_ln(hid, wd16, x, g, b, li, alpha, w_in=None, gb_next=None, tm=512, tk=2816):
    m, f = hid.shape
    d = wd16.shape[1]
    tm, tk = min(tm, m), min(tk, f)
    next_gates = w_in is not None
    vec = pl.BlockSpec((None, 1, d), lambda i, kk: (li, 0, 0))
    row = pl.BlockSpec((tm, d), lambda i, kk: (i, 0))
    in_specs = [pl.BlockSpec((tm, tk), lambda i, kk: (i, kk)),
                pl.BlockSpec((tk, d), lambda i, kk: (kk, 0)),
                row, vec, vec]
    out_specs = [row]
    out_shape = [jax.ShapeDtypeStruct((m, d), jnp.float32)]
    args = [hid, wd16, x, g, b]
    if next_gates:
        in_specs += [_gate_weight_spec(li + 1, d), pl.BlockSpec((1, GATE_LANES), lambda i, kk: (0, 0))]
        out_specs += [row, pl.BlockSpec((tm, GATE_LANES), lambda i, kk: (i, 0))]
        out_shape += [jax.ShapeDtypeStruct((m, d), jnp.bfloat16),
                      jax.ShapeDtypeStruct((m, GATE_LANES), jnp.float32)]
        args += [w_in, gb_next]
    return pl.pallas_call(
        functools.partial(_ffn_down_ln_kernel, alpha=alpha, next_gates=next_gates),
        grid=(m // tm, f // tk),
        in_specs=in_specs,
        out_specs=out_specs,
        out_shape=out_shape,
        compiler_params=_params(("parallel", "arbitrary")),
        name="ffn_down_ln",
    )(*args)


def _gate_bias_tile(gate_bias, li):
    return jnp.pad(gate_bias[li], (0, GATE_LANES - 2 * C_HEADS)).reshape(1, GATE_LANES)


def kernel(x, w_in, gate_bias, conv_w, lam, subln_a, norm_c, rel_bias, w_out, ln1_g, ln1_b,
           w_gate, w_up, w_down, ln2_g, ln2_b):
    bsz, s_len, d = x.shape
    depth = w_in.shape[0]
    alpha = (2.0 * depth) ** 0.25
    m = bsz * s_len
    x32 = x.reshape(m, d)
    ln1_g, ln1_b, ln2_g, ln2_b = (p.reshape(depth, 1, d) for p in (ln1_g, ln1_b, ln2_g, ln2_b))
    w_in_t = jnp.swapaxes(w_in, 1, 2)
    x16, gates = _cast_gate(x32, w_in_t, _gate_bias_tile(gate_bias, 0), 0)
    for li in range(depth):
        proj = _in_proj(x16, w_in_t, li).reshape(bsz, s_len, OFF_G)
        gates = gates.reshape(bsz, s_len, GATE_LANES)
        ya, wd16 = _attn_a(proj, lam, subln_a, w_down, li)
        yb = _attn_b(proj, rel_bias[li])
        norm_g = jnp.tile(norm_c[li], 2).reshape(1, C_PAIR)
        yc, wo16 = _mlstm(proj, gates, conv_w, norm_g, w_out, li)
        x32, x16 = _out_proj_ln(ya.reshape(m, A_WIDTH), yb.reshape(m, B_WIDTH), yc.reshape(m, C_WIDTH),
                                wo16, x32, ln1_g, ln1_b, li, alpha)
        hid = _ffn_up(x16, w_gate, w_up, li)
        if li + 1 < depth:
            x32, x16, gates = _ffn_down_ln(hid, wd16, x32, ln2_g, ln2_b, li, alpha,
                                           w_in_t, _gate_bias_tile(gate_bias, li + 1))
        else:
            (x32,) = _ffn_down_ln(hid, wd16, x32, ln2_g, ln2_b, li, alpha)
    return x32.reshape(bsz, s_len, d)
```

```python
import functools
import math

import numpy as np
import jax
import jax.numpy as jnp
from jax import lax
from jax.experimental import pallas as pl
from jax.experimental.pallas import tpu as pltpu

CHUNK = 64
A_HEADS = 4
A_HEAD_DIM = 128
A_HALF = 64
B_HEADS = 6
B_HEAD_DIM = 128
B_PAST_CHUNKS = 8
REL_CLIP = 256
C_HEADS = 4
C_HEAD_DIM = 192
C_PAIR = 2 * C_HEAD_DIM
C_WIN = 256
CONV_K = 4
NEG = -1e30
GATE_LANES = 128

A_WIDTH = A_HEADS * A_HEAD_DIM
B_WIDTH = B_HEADS * B_HEAD_DIM
C_WIDTH = C_HEADS * C_HEAD_DIM
OFF_A = 0
OFF_B = 3 * A_WIDTH
OFF_C = OFF_B + 3 * B_WIDTH
OFF_G = OFF_C + 4 * C_WIDTH
PROJ_TILE = 768

A_VT_ROWS = A_HEAD_DIM + 16
LOG2E = math.log2(math.e)

VMEM_LIMIT = 56 * 1024 * 1024

_NT = (((1,), (1,)), ((), ()))


def _params(sem, flags=None):
    return pltpu.CompilerParams(dimension_semantics=sem, vmem_limit_bytes=VMEM_LIMIT, flags=flags)


def _dot(a, b):
    return jnp.dot(a, b, preferred_element_type=jnp.float32)


def _dot_nt(a, b):
    return lax.dot_general(a, b, _NT, preferred_element_type=jnp.float32)


def _log_sigmoid(x):
    return jnp.minimum(x, 0.0) - jnp.log(1.0 + jnp.exp(-jnp.abs(x)))


def _sigmoid(x):
    return 1.0 / (1.0 + jnp.exp(-x))


def _layer_norm(z, g, b):
    mu = jnp.mean(z, axis=1, keepdims=True)
    zc = z - mu
    var = jnp.mean(zc * zc, axis=1, keepdims=True)
    return zc * lax.rsqrt(var + 1e-5) * g + b


N_GATES = 2 * C_HEADS


def _gate_weight_spec(li, d):
    assert OFF_G % N_GATES == 0
    return pl.BlockSpec((None, N_GATES, d), lambda *_: (li, OFF_G // N_GATES, 0))


def _gate_proj(x16, wg_ref, gb_ref):
    wg = wg_ref[...].astype(jnp.bfloat16)
    wg = jnp.concatenate([wg, jnp.zeros((GATE_LANES - N_GATES, wg.shape[1]), wg.dtype)], axis=0)
    return _dot_nt(x16, wg) + gb_ref[...]


def _cast_gate_kernel(x_ref, wg_ref, gb_ref, x16_ref, g_ref):
    x16 = x_ref[...].astype(jnp.bfloat16)
    x16_ref[...] = x16
    g_ref[...] = _gate_proj(x16, wg_ref, gb_ref)


def _cast_gate(x, w_in_t, gb, li, tm=1024):
    m, d = x.shape
    tm = min(tm, m)
    return pl.pallas_call(
        _cast_gate_kernel,
        grid=(m // tm,),
        in_specs=[
            pl.BlockSpec((tm, d), lambda i: (i, 0)),
            _gate_weight_spec(li, d),
            pl.BlockSpec((1, GATE_LANES), lambda i: (0, 0)),
        ],
        out_specs=[pl.BlockSpec((tm, d), lambda i: (i, 0)),
                   pl.BlockSpec((tm, GATE_LANES), lambda i: (i, 0))],
        out_shape=[jax.ShapeDtypeStruct((m, d), jnp.bfloat16),
                   jax.ShapeDtypeStruct((m, GATE_LANES), jnp.float32)],
        compiler_params=_params(("parallel",)),
        name="cast_gate",
    )(x, w_in_t, gb)


def _in_proj_kernel(x_ref, w_ref, o_ref, w16_ref):
    @pl.when(pl.program_id(1) == 0)
    def _():
        w16_ref[...] = w_ref[...].astype(jnp.bfloat16)

    o_ref[...] = _dot_nt(x_ref[...], w16_ref[...]).astype(o_ref.dtype)


def _in_proj(xb, w_in_t, li, tm=2048, tn=PROJ_TILE):
    m, k = xb.shape
    tm = min(tm, m)
    n = OFF_G
    return pl.pallas_call(
        _in_proj_kernel,
        grid=(n // tn, m // tm),
        in_specs=[
            pl.BlockSpec((tm, k), lambda j, i: (i, 0)),
            pl.BlockSpec((None, tn, k), lambda j, i: (li, j, 0)),
        ],
        out_specs=pl.BlockSpec((tm, tn), lambda j, i: (i, j)),
        out_shape=jax.ShapeDtypeStruct((m, n), jnp.bfloat16),
        scratch_shapes=[pltpu.VMEM((tn, k), jnp.bfloat16)],
        compiler_params=_params(("arbitrary", "arbitrary")),
        name="in_proj",
    )(xb, w_in_t)


def _attn_a_kernel(q_ref, k_ref, v_ref, lam_ref, g_ref, slope_ref, wd_ref, o_ref, wd16_ref,
                   qt1_ref, qt2_ref, k1_ref, k2_ref, vt_ref, *, tq, lam_init, ahead):
    wd16_ref[...] = wd_ref[...].astype(jnp.bfloat16)
    s_len = q_ref.shape[0]
    nq = s_len // tq
    slope2 = slope_ref[0:1, 0:1] * LOG2E
    lf = lam_ref[...]
    lam_full = (jnp.exp(jnp.sum(lf[0:1] * lf[1:2], axis=1, keepdims=True))
                - jnp.exp(jnp.sum(lf[2:3] * lf[3:4], axis=1, keepdims=True)) + lam_init)
    feat = lax.broadcasted_iota(jnp.int32, (A_HEAD_DIM, tq), 0)
    lane = lax.broadcasted_iota(jnp.int32, (tq, A_HEAD_DIM), 1)
    key_bias = slope2 * lax.broadcasted_iota(jnp.int32, (tq, A_HEAD_DIM), 0).astype(jnp.float32)
    kb_hi = key_bias.astype(jnp.bfloat16).astype(jnp.float32)
    kb_lo = (key_bias - kb_hi).astype(jnp.bfloat16).astype(jnp.float32)
    ones_row = jnp.where(lax.broadcasted_iota(jnp.int32, (A_VT_ROWS - A_HEAD_DIM, tq), 0) == 0, 1.0, 0.0)
    for i in range(nq):
        rows = slice(i * tq, (i + 1) * tq)
        qt = (q_ref[rows, :].astype(jnp.float32) * (A_HALF ** -0.5 * LOG2E)).T
        qt1_ref[:, rows] = jnp.where(feat < A_HALF, qt, jnp.where(feat < A_HALF + 2, 1.0, 0.0)).astype(jnp.bfloat16)
        qt2_ref[:, rows] = jnp.where(feat >= A_HALF, qt, jnp.where(feat < 2, 1.0, 0.0)).astype(jnp.bfloat16)
        k = k_ref[rows, :].astype(jnp.float32)
        k1 = jnp.where(lane < A_HALF, k, jnp.where(lane == A_HALF, kb_hi, jnp.where(lane == A_HALF + 1, kb_lo, 0.0)))
        k2 = jnp.where(lane >= A_HALF, k, jnp.where(lane == 0, kb_hi, jnp.where(lane == 1, kb_lo, 0.0)))
        k1_ref[rows, :] = k1.astype(jnp.bfloat16)
        k2_ref[rows, :] = k2.astype(jnp.bfloat16)
        vt_ref[0:A_HEAD_DIM, rows] = v_ref[rows, :].astype(jnp.float32).T.astype(jnp.bfloat16)
        vt_ref[A_HEAD_DIM:, rows] = ones_row.astype(jnp.bfloat16)

    key = lax.broadcasted_iota(jnp.int32, (tq, tq), 0)
    qry = lax.broadcasted_iota(jnp.int32, (tq, tq), 1)
    rel = (qry - key).astype(jnp.float32)
    bias_diag = slope2 * (rel - jnp.abs(rel))
    diag_ok = (key // CHUNK) <= (qry // CHUNK)

    def update2(t1, t2, shift_c, vt, st1, st2):
        ts, sts = (t1, t2), (st1, st2)
        m_new = [jnp.maximum(st[0], jnp.max(t, axis=0, keepdims=True) + shift_c) for t, st in zip(ts, sts)]
        alpha = [jnp.exp2(st[0] - mn) for st, mn in zip(sts, m_new)]
        p = [jnp.exp2(t - (mn - shift_c)) for t, mn in zip(ts, m_new)]
        pv = [_dot(vt, pp.astype(jnp.bfloat16)) for pp in p]
        acc = [a * st[1] + x for a, st, x in zip(alpha, sts, pv)]
        return (m_new[0], acc[0]), (m_new[1], acc[1])

    def scores(i, j):
        t1 = _dot(k1_ref[j * tq:(j + 1) * tq, :], qt1_ref[:, i * tq:(i + 1) * tq])
        t2 = _dot(k2_ref[j * tq:(j + 1) * tq, :], qt2_ref[:, i * tq:(i + 1) * tq])
        if j < i:
            return t1, t2
        return jnp.where(diag_ok, t1 + bias_diag, NEG), jnp.where(diag_ok, t2 + bias_diag, NEG)

    pairs = [(i, j) for i in range(nq) for j in range(i + 1)]
    pending = [scores(*pr) for pr in pairs[:ahead]]
    fresh = (jnp.full((1, tq), NEG, jnp.float32), jnp.zeros((A_VT_ROWS, tq), jnp.float32))
    for n, (i, j) in enumerate(pairs):
        if j == 0:
            st1 = st2 = fresh
        t1, t2 = pending.pop(0)
        if n + ahead < len(pairs):
            pending.append(scores(*pairs[n + ahead]))
        if j < i:
            shift_c = -slope2 * float((i - j) * tq)
        else:
            shift_c = jnp.zeros((1, 1), jnp.float32)
        st1, st2 = update2(t1, t2, shift_c, vt_ref[:, j * tq:(j + 1) * tq], st1, st2)
        if j == i:
            (num1, l1), (num2, l2) = [(st[1][:A_HEAD_DIM], st[1][A_HEAD_DIM:A_HEAD_DIM + 1]) for st in (st1, st2)]
            ot = num1 * (1.0 / l1) - (lam_full / l2) * num2
            o = ot.T
            ms = jnp.mean(o * o, axis=1, keepdims=True)
            o = o * lax.rsqrt(ms + 1e-6) * g_ref[...] * (1.0 - lam_init)
            o_ref[i * tq:(i + 1) * tq, :] = o.astype(o_ref.dtype)


def _attn_a(proj, lam, subln, w_down, li, tq=256, ahead=2):
    b, s, _ = proj.shape
    tq = min(tq, s)
    f, d = w_down.shape[1:]
    slab = f // (b * A_HEADS)
    assert slab * b * A_HEADS == f and slab % 16 == 0
    lam_init = 0.8 - 0.6 * math.exp(-0.3 * li)
    slopes = np.asarray([2.0 ** (-8.0 * (h + 1) / A_HEADS) for h in range(A_HEADS)], np.float32)
    slopes = jnp.asarray(np.broadcast_to(slopes[:, None, None], (A_HEADS, 1, 128)))
    qi = OFF_A // A_HEAD_DIM
    ki = (OFF_A + A_WIDTH) // A_HEAD_DIM
    vi = (OFF_A + 2 * A_WIDTH) // A_HEAD_DIM

    def hspec(base):
        return pl.BlockSpec((None, s, A_HEAD_DIM), lambda bi, h, base=base: (bi, 0, base + h))

    return pl.pallas_call(
        functools.partial(_attn_a_kernel, tq=tq, lam_init=lam_init, ahead=ahead),
        grid=(b, A_HEADS),
        in_specs=[
            hspec(qi), hspec(ki), hspec(vi),
            pl.BlockSpec((None, 4, A_HALF), lambda bi, h: (li, 0, 0)),
            pl.BlockSpec((None, 1, A_HEAD_DIM), lambda bi, h: (li, 0, 0)),
            pl.BlockSpec((None, 1, 128), lambda bi, h: (h, 0, 0)),
            pl.BlockSpec((None, slab, d), lambda bi, h: (li, bi * A_HEADS + h, 0)),
        ],
        out_specs=[pl.BlockSpec((None, s, A_HEAD_DIM), lambda bi, h: (bi, 0, h)),
                   pl.BlockSpec((slab, d), lambda bi, h: (bi * A_HEADS + h, 0))],
        out_shape=[jax.ShapeDtypeStruct((b, s, A_WIDTH), jnp.bfloat16),
                   jax.ShapeDtypeStruct((f, d), jnp.bfloat16)],
        scratch_shapes=[
            pltpu.VMEM((A_HEAD_DIM, s), jnp.bfloat16),
            pltpu.VMEM((A_HEAD_DIM, s), jnp.bfloat16),
            pltpu.VMEM((s, A_HEAD_DIM), jnp.bfloat16),
            pltpu.VMEM((s, A_HEAD_DIM), jnp.bfloat16),
            pltpu.VMEM((A_VT_ROWS, s), jnp.bfloat16),
        ],
        compiler_params=_params(("parallel", "parallel")),
        name="attn_a",
    )(proj, proj, proj, lam, subln.reshape(-1, 1, A_HEAD_DIM), slopes, w_down)


def _attn_b_kernel(q_ref, k_ref, v_ref, u_ref, o_ref, kpad, vpad, table_ref, *, tq, pad):
    s_len = q_ref.shape[0]
    win = pad + tq
    width = u_ref.shape[1]

    @pl.when(pl.program_id(1) == 0)
    def _():
        rolled = pltpu.roll(jnp.broadcast_to(u_ref[...], (tq, width)), width - (tq - 1), 1,
                            stride=1, stride_axis=0)
        r = lax.broadcasted_iota(jnp.int32, (tq, win), 0)
        j = lax.broadcasted_iota(jnp.int32, (tq, win), 1)
        kc, qc = j // CHUNK, B_PAST_CHUNKS + r // CHUNK
        table_ref[...] = jnp.where((kc <= qc) & (kc >= qc - B_PAST_CHUNKS), rolled[:, :win] * LOG2E, NEG)

    kpad[0:pad, :] = jnp.zeros((pad, B_HEAD_DIM), kpad.dtype)
    vpad[0:pad, :] = jnp.zeros((pad, B_HEAD_DIM), vpad.dtype)
    kpad[pad:pad + s_len, :] = k_ref[...]
    vpad[pad:pad + s_len, :] = v_ref[...]
    table = table_ref[...]
    jcol = lax.broadcasted_iota(jnp.int32, (tq, win), 1)
    scale2 = B_HEAD_DIM ** -0.5 * LOG2E
    nq = s_len // tq

    def scores(i):
        qs = i * tq
        q = (q_ref[qs:qs + tq, :].astype(jnp.float32) * scale2).astype(jnp.bfloat16)
        s = _dot_nt(q, kpad[qs:qs + win, :]) + table
        if qs < pad:
            s = jnp.where(jcol >= pad - qs, s, NEG)
        return s

    nxt = scores(0)
    for i in range(nq):
        s = nxt
        if i + 1 < nq:
            nxt = scores(i + 1)
        qs = i * tq
        m = jnp.max(s, axis=1, keepdims=True)
        p = jnp.exp2(s - m)
        l = jnp.sum(p, axis=1, keepdims=True)
        o = _dot(p.astype(jnp.bfloat16), vpad[qs:qs + win, :]) / l
        o_ref[qs:qs + tq, :] = o.astype(o_ref.dtype)


def _band_bias_vector(rel_bias, tq, pad):
    n = pad + 2 * tq - 1
    n_hi = pad + tq - REL_CLIP
    lo = REL_CLIP - tq + 1
    assert n_hi >= 0 and lo >= 0
    rb = rel_bias.astype(jnp.float32)
    u = jnp.concatenate([jnp.repeat(rb[:, 2 * REL_CLIP:], n_hi, axis=1), rb[:, lo:2 * REL_CLIP][:, ::-1]], axis=1)
    assert u.shape[1] == n
    return jnp.pad(u, ((0, 0), (0, 1)))[:, None, :]


def _attn_b(proj, rel_bias, tq=256):
    b, s, _ = proj.shape
    pad = B_PAST_CHUNKS * CHUNK
    width = pad + 2 * tq
    assert width & (width - 1) == 0
    u = _band_bias_vector(rel_bias, tq, pad)
    qi = OFF_B // B_HEAD_DIM
    ki = (OFF_B + B_WIDTH) // B_HEAD_DIM
    vi = (OFF_B + 2 * B_WIDTH) // B_HEAD_DIM

    def hspec(base):
        return pl.BlockSpec((None, s, B_HEAD_DIM), lambda h, bi, base=base: (bi, 0, base + h))

    return pl.pallas_call(
        functools.partial(_attn_b_kernel, tq=tq, pad=pad),
        grid=(B_HEADS, b),
        in_specs=[
            hspec(qi), hspec(ki), hspec(vi),
            pl.BlockSpec((None, 1, width), lambda h, bi: (h, 0, 0)),
        ],
        out_specs=pl.BlockSpec((None, s, B_HEAD_DIM), lambda h, bi: (bi, 0, h)),
        out_shape=jax.ShapeDtypeStruct((b, s, B_WIDTH), jnp.bfloat16),
        scratch_shapes=[
            pltpu.VMEM((pad + s, B_HEAD_DIM), jnp.bfloat16),
            pltpu.VMEM((pad + s, B_HEAD_DIM), jnp.bfloat16),
            pltpu.VMEM((tq, pad + tq), jnp.float32),
        ],
        compiler_params=_params(("arbitrary", "arbitrary")),
        name="attn_b",
    )(proj, proj, proj, u)


def _cumsum_rows(x):
    n = x.shape[0]
    row = lax.broadcasted_iota(jnp.int32, x.shape, 0)
    sh = 1
    while sh < n:
        x = x + jnp.where(row >= sh, pltpu.roll(x, sh, 0), 0.0)
        sh *= 2
    return x


def _mlstm_kernel(q_ref, k_ref, v_ref, og_ref, gate_ref, cq_ref, ck_ref, ng_ref, wo_ref, o_ref, wo16_ref,
                  qs_ref, ks_ref, state_ref, gt_ref, *, chunk, halo):
    wo16_ref[...] = wo_ref[...].astype(jnp.bfloat16)
    s_len = q_ref.shape[0]
    n_chunks = s_len // chunk
    pair = pl.program_id(1)
    lane = lax.broadcasted_iota(jnp.int32, (chunk, C_WIN), 1)
    glane = lax.broadcasted_iota(jnp.int32, (chunk, GATE_LANES), 1)
    trow = lax.broadcasted_iota(jnp.int32, (chunk, chunk), 0)
    tcol = lax.broadcasted_iota(jnp.int32, (chunk, chunk), 1)
    causal = tcol <= trow
    windows = ((0, lane < C_HEAD_DIM, C_HEAD_DIM),
               (C_PAIR - C_WIN, lane >= C_WIN - C_HEAD_DIM, 0))

    def conv_chunk(c, carry):
        st = pl.multiple_of(c * chunk, chunk)
        prev_st = pl.multiple_of(jnp.maximum(st - halo, 0), halo)
        keep = jnp.where(c > 0, 1.0, 0.0)
        for src, w_ref, dst, post in ((q_ref, cq_ref, qs_ref, 1.0),
                                      (k_ref, ck_ref, ks_ref, C_HEAD_DIM ** -0.5)):
            cur = src[pl.ds(st, chunk), :].astype(jnp.float32)
            prev = src[pl.ds(prev_st, halo), :].astype(jnp.float32) * keep
            xc = jnp.concatenate([prev, cur], axis=0)
            w = w_ref[...]
            y = w[CONV_K - 1:CONV_K, :] * cur
            for back in range(1, CONV_K):
                y = y + w[CONV_K - 1 - back:CONV_K - back, :] * pltpu.roll(xc, back, 0)[halo:, :]
            y = y * _sigmoid(y) * post
            dst[pl.ds(st, chunk), :] = y.astype(dst.dtype)
        return carry

    lax.fori_loop(0, n_chunks, conv_chunk, 0)
    state_ref[...] = jnp.zeros(state_ref.shape, jnp.float32)

    def step(c, ms):
        st = pl.multiple_of(c * chunk, chunk)
        g = gate_ref[pl.ds(st, chunk), :]
        bcum = _cumsum_rows(_log_sigmoid(g))
        gb = jnp.where(glane < C_HEADS, g, bcum)
        gt_ref[...] = gb.T
        ng = ng_ref[...]
        zero = jnp.zeros((), jnp.bfloat16)
        ops, qk, qc = [], [], []
        for a, (w0, valid, spare) in enumerate(windows):
            wcols = slice(w0, w0 + C_WIN)
            q = jnp.where(valid, qs_ref[pl.ds(st, chunk), wcols], zero)
            k = jnp.where(valid, ks_ref[pl.ds(st, chunk), wcols], zero)
            v = jnp.where(valid, v_ref[pl.ds(st, chunk), wcols], zero)
            v_ext = jnp.where(lane == spare, jnp.ones((), v.dtype), v)
            ops.append((q, k, v_ext))
            qk.append(_dot_nt(q, k))
            qc.append(_dot(q, state_ref[a].astype(jnp.bfloat16)))

        gate, m_out = [], []
        for a in range(2):
            h = 2 * pair + a
            m = ms[a]
            i_row = gt_ref[pl.ds(h, 1), :]
            b_row = gt_ref[pl.ds(C_HEADS + h, 1), :]
            i_col = jnp.sum(jnp.where(glane == h, gb, 0.0), axis=1, keepdims=True)
            b_col = jnp.sum(jnp.where(glane == C_HEADS + h, gb, 0.0), axis=1, keepdims=True)
            d = jnp.where(causal, b_col + (i_row - b_row), NEG)
            inter = b_col + m
            m_t = jnp.maximum(inter, jnp.max(d, axis=1, keepdims=True))
            b_last = b_col[chunk - 1:chunk, :]
            g_col = b_last - b_col + i_col
            m_new = jnp.maximum(b_last + m, jnp.max(g_col, axis=0, keepdims=True))
            gate.append((d, inter, m_t, jnp.exp(g_col - m_new), jnp.exp(b_last + m - m_new)))
            m_out.append(m_new)

        nds = []
        for a in range(2):
            d, inter, m_t = gate[a][:3]
            sw = qk[a] * jnp.exp(d - m_t)
            nds.append(_dot(sw.astype(jnp.bfloat16), ops[a][2]) + jnp.exp(inter - m_t) * qc[a])

        for a in range(2):
            k, v_ext = ops[a][1:]
            wk, decay = gate[a][3:]
            kw_t = (k.astype(jnp.float32) * wk).T.astype(jnp.bfloat16)
            state_ref[a] = decay * state_ref[a] + _dot(kw_t, v_ext)

        hn = []
        for a, (w0, valid, spare) in enumerate(windows):
            nd, m_t = nds[a], gate[a][2]
            den = jnp.sum(jnp.where(lane == spare, nd, 0.0), axis=1, keepdims=True)
            num = jnp.where(valid, nd, 0.0)
            hh = num / jnp.maximum(jnp.abs(den), jnp.exp(-m_t))
            msq = jnp.sum(hh * hh, axis=1, keepdims=True) * (1.0 / C_HEAD_DIM)
            hn.append(hh * lax.rsqrt(msq + 1e-6) * ng[:, w0:w0 + C_WIN])

        mid = C_PAIR - C_WIN
        hn_pair = jnp.concatenate([hn[0][:, :mid], hn[0][:, mid:] + hn[1][:, :C_WIN - mid], hn[1][:, C_WIN - mid:]],
                                  axis=1)
        og = og_ref[pl.ds(st, chunk), :].astype(jnp.float32)
        o_ref[pl.ds(st, chunk), :] = (_sigmoid(og) * hn_pair).astype(o_ref.dtype)
        return tuple(m_out)

    m0 = jnp.zeros((1, 1), jnp.float32)
    lax.fori_loop(0, n_chunks, step, (m0, m0), unroll=4)


def _mlstm(proj, gates, conv_w, norm_g, w_out, li, chunk=256, halo=16):
    b, s, _ = proj.shape
    chunk = min(chunk, s)
    n_pairs = C_HEADS // 2
    base = OFF_C // C_PAIR
    d_in, d = w_out.shape[1:]
    slab = d_in // (b * n_pairs)
    assert slab * b * n_pairs == d_in and slab % 16 == 0

    def pspec(t):
        return pl.BlockSpec((None, s, C_PAIR), lambda bi, p, t=t: (bi, 0, base + n_pairs * t + p))

    return pl.pallas_call(
        functools.partial(_mlstm_kernel, chunk=chunk, halo=halo),
        grid=(b, n_pairs),
        in_specs=[
            pspec(0), pspec(1), pspec(2), pspec(3),
            pl.BlockSpec((None, s, GATE_LANES), lambda bi, p: (bi, 0, 0)),
            pl.BlockSpec((None, CONV_K, C_PAIR), lambda bi, p: (li, 0, p)),
            pl.BlockSpec((None, CONV_K, C_PAIR), lambda bi, p: (li, 0, n_pairs + p)),
            pl.BlockSpec((1, C_PAIR), lambda bi, p: (0, 0)),
            pl.BlockSpec((None, slab, d), lambda bi, p: (li, bi * n_pairs + p, 0)),
        ],
        out_specs=[pl.BlockSpec((None, s, C_PAIR), lambda bi, p: (bi, 0, p)),
                   pl.BlockSpec((slab, d), lambda bi, p: (bi * n_pairs + p, 0))],
        out_shape=[jax.ShapeDtypeStruct((b, s, C_WIDTH), jnp.bfloat16),
                   jax.ShapeDtypeStruct((d_in, d), jnp.bfloat16)],
        scratch_shapes=[
            pltpu.VMEM((s, C_PAIR), jnp.bfloat16),
            pltpu.VMEM((s, C_PAIR), jnp.bfloat16),
            pltpu.VMEM((2, C_WIN, C_WIN), jnp.float32),
            pltpu.VMEM((GATE_LANES, chunk), jnp.float32),
        ],
        compiler_params=_params(("parallel", "parallel")),
        name="mlstm",
    )(proj, proj, proj, proj, gates, conv_w, conv_w, norm_g, w_out)


def _out_proj_ln_kernel(ya_ref, yb_ref, yc_ref, w16_ref, x_ref, g_ref, b_ref, o32_ref, o16_ref, *, alpha):
    half = x_ref.shape[0] // 2

    def project(r):
        rows = slice(r * half, (r + 1) * half)
        mixed = jnp.concatenate([ya_ref[rows, :], yb_ref[rows, :], yc_ref[rows, :]], axis=1)
        return _dot(mixed, w16_ref[...])

    nxt = project(0)
    for r in range(2):
        mix = nxt
        if r == 0:
            nxt = project(1)
        rows = slice(r * half, (r + 1) * half)
        y = _layer_norm(alpha * x_ref[rows, :] + mix, g_ref[...], b_ref[...])
        o32_ref[rows, :] = y
        o16_ref[rows, :] = y.astype(o16_ref.dtype)


def _out_proj_ln(ya, yb, yc, wo16, x, g, b, li, alpha, tm=512):
    m, d = x.shape
    tm = min(tm, m)
    row = lambda w: pl.BlockSpec((tm, w), lambda i: (i, 0))
    vec = pl.BlockSpec((None, 1, d), lambda i: (li, 0, 0))
    return pl.pallas_call(
        functools.partial(_out_proj_ln_kernel, alpha=alpha),
        grid=(m // tm,),
        in_specs=[row(ya.shape[1]), row(yb.shape[1]), row(yc.shape[1]),
                  pl.BlockSpec(wo16.shape, lambda i: (0, 0), pipeline_mode=pl.Buffered(1)),
                  row(d), vec, vec],
        out_specs=[row(d), row(d)],
        out_shape=[jax.ShapeDtypeStruct((m, d), jnp.float32),
                   jax.ShapeDtypeStruct((m, d), jnp.bfloat16)],
        compiler_params=_params(("parallel",)),
        name="out_proj_ln",
    )(ya, yb, yc, wo16, x, g, b)


FFN_SLABS = 4


def _ffn_up_kernel(x_ref, wg_ref, wu_ref, o_ref, wg16_ref, wu16_ref):
    @pl.when(pl.program_id(1) == 0)
    def _():
        wg16_ref[...] = wg_ref[...].astype(jnp.bfloat16)
        wu16_ref[...] = wu_ref[...].astype(jnp.bfloat16)

    slab = x_ref.shape[0] // FFN_SLABS

    def products(r):
        x = x_ref[r * slab:(r + 1) * slab, :]
        return _dot(x, wg16_ref[...]), _dot(x, wu16_ref[...])

    nxt = products(0)
    for r in range(FFN_SLABS):
        a, u = nxt
        if r + 1 < FFN_SLABS:
            nxt = products(r + 1)
        o_ref[r * slab:(r + 1) * slab, :] = (a * _sigmoid(a) * u).astype(o_ref.dtype)


def _ffn_up(xb, w_gate, w_up, li, tm=2048, tn=512):
    m, k = xb.shape
    n = w_gate.shape[2]
    tm, tn = min(tm, m), min(tn, n)
    return pl.pallas_call(
        _ffn_up_kernel,
        grid=(n // tn, m // tm),
        in_specs=[
            pl.BlockSpec((tm, k), lambda j, i: (i, 0)),
            pl.BlockSpec((None, k, tn), lambda j, i: (li, 0, j)),
            pl.BlockSpec((None, k, tn), lambda j, i: (li, 0, j)),
        ],
        out_specs=pl.BlockSpec((tm, tn), lambda j, i: (i, j)),
        out_shape=jax.ShapeDtypeStruct((m, n), jnp.bfloat16),
        scratch_shapes=[pltpu.VMEM((k, tn), jnp.bfloat16), pltpu.VMEM((k, tn), jnp.bfloat16)],
        compiler_params=_params(("arbitrary", "arbitrary")),
        name="ffn_up",
    )(xb, w_gate, w_up)


def _ffn_down_ln_kernel(h_ref, w_ref, x_ref, g_ref, b_ref, *rest, alpha, next_gates, single_k):
    if next_gates:
        wg_ref, gb_ref, o32_ref, o16_ref, gate_ref = rest
    else:
        (o32_ref,) = rest

    def finish(ffn):
        y = _layer_norm(alpha * x_ref[...] + ffn, g_ref[...], b_ref[...])
        o32_ref[...] = y
        if next_gates:
            y16 = y.astype(jnp.bfloat16)
            o16_ref[...] = y16
            gate_ref[...] = _gate_proj(y16, wg_ref, gb_ref)

    if single_k:
        finish(_dot(h_ref[...], w_ref[...]))
        return

    kk = pl.program_id(1)

    @pl.when(kk == 0)
    def _():
        o32_ref[...] = jnp.zeros(o32_ref.shape, jnp.float32)

    o32_ref[...] += _dot(h_ref[...], w_ref[...])

    @pl.when(kk == pl.num_programs(1) - 1)
    def _():
        finish(o32_ref[...])


def _ffn_down_ln(hid, wd16, x, g, b, li, alpha, w_in=None, gb_next=None, tm=256, tk=5632):
    m, f = hid.shape
    d = wd16.shape[1]
    tm, tk = min(tm, m), min(tk, f)
    next_gates = w_in is not None
    vec = pl.BlockSpec((None, 1, d), lambda i, kk: (li, 0, 0))
    row = pl.BlockSpec((tm, d), lambda i, kk: (i, 0))
    in_specs = [pl.BlockSpec((tm, tk), lambda i, kk: (i, kk)),
                pl.BlockSpec((tk, d), lambda i, kk: (kk, 0),
                             **({"pipeline_mode": pl.Buffered(1)} if tk == f else {})),
                row, vec, vec]
    out_specs = [row]
    out_shape = [jax.ShapeDtypeStruct((m, d), jnp.float32)]
    args = [hid, wd16, x, g, b]
    if next_gates:
        in_specs += [_gate_weight_spec(li + 1, d), pl.BlockSpec((1, GATE_LANES), lambda i, kk: (0, 0))]
        out_specs += [row, pl.BlockSpec((tm, GATE_LANES), lambda i, kk: (i, 0))]
        out_shape += [jax.ShapeDtypeStruct((m, d), jnp.bfloat16),
                      jax.ShapeDtypeStruct((m, GATE_LANES), jnp.float32)]
        args += [w_in, gb_next]
    return pl.pallas_call(
        functools.partial(_ffn_down_ln_kernel, alpha=alpha, next_gates=next_gates, single_k=tk == f),
        grid=(m // tm, f // tk),
        in_specs=in_specs,
        out_specs=out_specs,
        out_shape=out_shape,
        compiler_params=_params(("parallel", "arbitrary")),
        name="ffn_down_ln",
    )(*args)


def _gate_bias_tile(gate_bias, li):
    return jnp.pad(gate_bias[li], (0, GATE_LANES - 2 * C_HEADS)).reshape(1, GATE_LANES)


def kernel(x, w_in, gate_bias, conv_w, lam, subln_a, norm_c, rel_bias, w_out, ln1_g, ln1_b,
           w_gate, w_up, w_down, ln2_g, ln2_b):
    bsz, s_len, d = x.shape
    depth = w_in.shape[0]
    alpha = (2.0 * depth) ** 0.25
    m = bsz * s_len
    x32 = x.reshape(m, d)
    ln1_g, ln1_b, ln2_g, ln2_b = (p.reshape(depth, 1, d) for p in (ln1_g, ln1_b, ln2_g, ln2_b))
    w_in_t = jnp.swapaxes(w_in, 1, 2)
    x16, gates = _cast_gate(x32, w_in_t, _gate_bias_tile(gate_bias, 0), 0)
    for li in range(depth):
        proj = _in_proj(x16, w_in_t, li).reshape(bsz, s_len, OFF_G)
        gates = gates.reshape(bsz, s_len, GATE_LANES)
        ya, wd16 = _attn_a(proj, lam, subln_a, w_down, li)
        yb = _attn_b(proj, rel_bias[li])
        norm_g = jnp.tile(norm_c[li], 2).reshape(1, C_PAIR)
        yc, wo16 = _mlstm(proj, gates, conv_w, norm_g, w_out, li)
        x32, x16 = _out_proj_ln(ya.reshape(m, A_WIDTH), yb.reshape(m, B_WIDTH), yc.reshape(m, C_WIDTH),
                                wo16, x32, ln1_g, ln1_b, li, alpha)
        hid = _ffn_up(x16, w_gate, w_up, li)
        if li + 1 < depth:
            x32, x16, gates = _ffn_down_ln(hid, wd16, x32, ln2_g, ln2_b, li, alpha,
                                           w_in_t, _gate_bias_tile(gate_bias, li + 1))
        else:
            (x32,) = _ffn_down_ln(hid, wd16, x32, ln2_g, ln2_b, li, alpha)
    return x32.reshape(bsz, s_len, d)
```

```python
import functools
import math

import numpy as np
import jax
import jax.numpy as jnp
from jax import lax
from jax.experimental import pallas as pl
from jax.experimental.pallas import tpu as pltpu

CHUNK = 64
A_HEADS = 4
A_HEAD_DIM = 128
A_HALF = 64
B_HEADS = 6
B_HEAD_DIM = 128
B_PAST_CHUNKS = 8
REL_CLIP = 256
C_HEADS = 4
C_HEAD_DIM = 192
C_PAIR = 2 * C_HEAD_DIM
C_WIN = 256
CONV_K = 4
NEG = -1e30
GATE_LANES = 128

A_WIDTH = A_HEADS * A_HEAD_DIM
B_WIDTH = B_HEADS * B_HEAD_DIM
C_WIDTH = C_HEADS * C_HEAD_DIM
OFF_A = 0
OFF_B = 3 * A_WIDTH
OFF_C = OFF_B + 3 * B_WIDTH
OFF_G = OFF_C + 4 * C_WIDTH
N_GATES = 2 * C_HEADS

BF16_ROW_TILE = 16
A_VT_ROWS = A_HEAD_DIM + BF16_ROW_TILE
LOG2E = math.log2(math.e)

V7X_VMEM_BYTES = 64 * 1024 * 1024
VMEM_LIMIT = V7X_VMEM_BYTES * 7 // 8
CAST_ROWS = 1024
PROJ_ROWS = 2048
PROJ_TILE = 768
OUT_PROJ_ROWS = 512
FFN_UP_ROWS = 2048
FFN_UP_COLS = 512
FFN_SLABS = 4
FFN_DOWN_ROWS = 256
ATTN_TILE = 256
ATTN_A_AHEAD = 2
MLSTM_CHUNK = 256
MLSTM_HALO = BF16_ROW_TILE

_NT = (((1,), (1,)), ((), ()))


def _params(sem):
    return pltpu.CompilerParams(dimension_semantics=sem, vmem_limit_bytes=VMEM_LIMIT)


def _dot(a, b):
    return jnp.dot(a, b, preferred_element_type=jnp.float32)


def _dot_nt(a, b):
    return lax.dot_general(a, b, _NT, preferred_element_type=jnp.float32)


def _log_sigmoid(x):
    return jnp.minimum(x, 0.0) - jnp.log(1.0 + jnp.exp(-jnp.abs(x)))


def _sigmoid(x):
    return 1.0 / (1.0 + jnp.exp(-x))


def _layer_norm(z, g, b):
    mu = jnp.mean(z, axis=1, keepdims=True)
    zc = z - mu
    var = jnp.mean(zc * zc, axis=1, keepdims=True)
    return zc * lax.rsqrt(var + 1e-5) * g + b


def _gate_weight_spec(li, d):
    assert OFF_G % N_GATES == 0
    return pl.BlockSpec((None, N_GATES, d), lambda *_: (li, OFF_G // N_GATES, 0))


def _gate_proj(x16, wg_ref, gb_ref):
    wg = wg_ref[...].astype(jnp.bfloat16)
    wg = jnp.concatenate([wg, jnp.zeros((GATE_LANES - N_GATES, wg.shape[1]), wg.dtype)], axis=0)
    return _dot_nt(x16, wg) + gb_ref[...]


def _cast_gate_kernel(x_ref, wg_ref, gb_ref, x16_ref, g_ref):
    x16 = x_ref[...].astype(jnp.bfloat16)
    x16_ref[...] = x16
    g_ref[...] = _gate_proj(x16, wg_ref, gb_ref)


def _cast_gate(x, w_in_t, gb, li, tm=CAST_ROWS):
    m, d = x.shape
    tm = min(tm, m)
    return pl.pallas_call(
        _cast_gate_kernel,
        grid=(m // tm,),
        in_specs=[
            pl.BlockSpec((tm, d), lambda i: (i, 0)),
            _gate_weight_spec(li, d),
            pl.BlockSpec((1, GATE_LANES), lambda i: (0, 0)),
        ],
        out_specs=[pl.BlockSpec((tm, d), lambda i: (i, 0)),
                   pl.BlockSpec((tm, GATE_LANES), lambda i: (i, 0))],
        out_shape=[jax.ShapeDtypeStruct((m, d), jnp.bfloat16),
                   jax.ShapeDtypeStruct((m, GATE_LANES), jnp.float32)],
        compiler_params=_params(("parallel",)),
        name="cast_gate",
    )(x, w_in_t, gb)


def _in_proj_kernel(x_ref, w_ref, o_ref, w16_ref):
    @pl.when(pl.program_id(1) == 0)
    def _():
        w16_ref[...] = w_ref[...].astype(jnp.bfloat16)

    o_ref[...] = _dot_nt(x_ref[...], w16_ref[...]).astype(o_ref.dtype)


def _in_proj(xb, w_in_t, li, tm=PROJ_ROWS, tn=PROJ_TILE):
    m, k = xb.shape
    tm = min(tm, m)
    n = OFF_G
    return pl.pallas_call(
        _in_proj_kernel,
        grid=(n // tn, m // tm),
        in_specs=[
            pl.BlockSpec((tm, k), lambda j, i: (i, 0)),
            pl.BlockSpec((None, tn, k), lambda j, i: (li, j, 0)),
        ],
        out_specs=pl.BlockSpec((tm, tn), lambda j, i: (i, j)),
        out_shape=jax.ShapeDtypeStruct((m, n), jnp.bfloat16),
        scratch_shapes=[pltpu.VMEM((tn, k), jnp.bfloat16)],
        compiler_params=_params(("arbitrary", "arbitrary")),
        name="in_proj",
    )(xb, w_in_t)


def _attn_a_kernel(q_ref, k_ref, v_ref, lam_ref, g_ref, slope_ref, wd_ref, o_ref, wd16_ref,
                   qt1_ref, qt2_ref, k1_ref, k2_ref, vt_ref, *, tq, lam_init, ahead):
    wd16_ref[...] = wd_ref[...].astype(jnp.bfloat16)
    s_len = q_ref.shape[0]
    nq = s_len // tq
    slope2 = slope_ref[0:1, 0:1] * LOG2E
    lf = lam_ref[...]
    lam_full = (jnp.exp(jnp.sum(lf[0:1] * lf[1:2], axis=1, keepdims=True))
                - jnp.exp(jnp.sum(lf[2:3] * lf[3:4], axis=1, keepdims=True)) + lam_init)
    feat = lax.broadcasted_iota(jnp.int32, (A_HEAD_DIM, tq), 0)
    lane = lax.broadcasted_iota(jnp.int32, (tq, A_HEAD_DIM), 1)
    key_bias = slope2 * lax.broadcasted_iota(jnp.int32, (tq, A_HEAD_DIM), 0).astype(jnp.float32)
    kb_hi = key_bias.astype(jnp.bfloat16).astype(jnp.float32)
    kb_lo = (key_bias - kb_hi).astype(jnp.bfloat16).astype(jnp.float32)
    ones_row = jnp.where(lax.broadcasted_iota(jnp.int32, (A_VT_ROWS - A_HEAD_DIM, tq), 0) == 0, 1.0, 0.0)
    for i in range(nq):
        rows = slice(i * tq, (i + 1) * tq)
        qt = (q_ref[rows, :].astype(jnp.float32) * (A_HALF ** -0.5 * LOG2E)).T
        qt1_ref[:, rows] = jnp.where(feat < A_HALF, qt, jnp.where(feat < A_HALF + 2, 1.0, 0.0)).astype(jnp.bfloat16)
        qt2_ref[:, rows] = jnp.where(feat >= A_HALF, qt, jnp.where(feat < 2, 1.0, 0.0)).astype(jnp.bfloat16)
        k = k_ref[rows, :].astype(jnp.float32)
        k1 = jnp.where(lane < A_HALF, k, jnp.where(lane == A_HALF, kb_hi, jnp.where(lane == A_HALF + 1, kb_lo, 0.0)))
        k2 = jnp.where(lane >= A_HALF, k, jnp.where(lane == 0, kb_hi, jnp.where(lane == 1, kb_lo, 0.0)))
        k1_ref[rows, :] = k1.astype(jnp.bfloat16)
        k2_ref[rows, :] = k2.astype(jnp.bfloat16)
        vt_ref[0:A_HEAD_DIM, rows] = v_ref[rows, :].astype(jnp.float32).T.astype(jnp.bfloat16)
        vt_ref[A_HEAD_DIM:, rows] = ones_row.astype(jnp.bfloat16)

    key = lax.broadcasted_iota(jnp.int32, (tq, tq), 0)
    qry = lax.broadcasted_iota(jnp.int32, (tq, tq), 1)
    rel = (qry - key).astype(jnp.float32)
    bias_diag = slope2 * (rel - jnp.abs(rel))
    diag_ok = (key // CHUNK) <= (qry // CHUNK)

    def update2(t1, t2, shift_c, vt, st1, st2):
        ts, sts = (t1, t2), (st1, st2)
        m_new = [jnp.maximum(st[0], jnp.max(t, axis=0, keepdims=True) + shift_c) for t, st in zip(ts, sts)]
        alpha = [jnp.exp2(st[0] - mn) for st, mn in zip(sts, m_new)]
        p = [jnp.exp2(t - (mn - shift_c)) for t, mn in zip(ts, m_new)]
        pv = [_dot(vt, pp.astype(jnp.bfloat16)) for pp in p]
        acc = [a * st[1] + x for a, st, x in zip(alpha, sts, pv)]
        return (m_new[0], acc[0]), (m_new[1], acc[1])

    def scores(i, j):
        t1 = _dot(k1_ref[j * tq:(j + 1) * tq, :], qt1_ref[:, i * tq:(i + 1) * tq])
        t2 = _dot(k2_ref[j * tq:(j + 1) * tq, :], qt2_ref[:, i * tq:(i + 1) * tq])
        if j < i:
            return t1, t2
        return jnp.where(diag_ok, t1 + bias_diag, NEG), jnp.where(diag_ok, t2 + bias_diag, NEG)

    pairs = [(i, j) for i in range(nq) for j in range(i + 1)]
    pending = [scores(*pr) for pr in pairs[:ahead]]
    fresh = (jnp.full((1, tq), NEG, jnp.float32), jnp.zeros((A_VT_ROWS, tq), jnp.float32))
    for n, (i, j) in enumerate(pairs):
        if j == 0:
            st1 = st2 = fresh
        t1, t2 = pending.pop(0)
        if n + ahead < len(pairs):
            pending.append(scores(*pairs[n + ahead]))
        if j < i:
            shift_c = -slope2 * float((i - j) * tq)
        else:
            shift_c = jnp.zeros((1, 1), jnp.float32)
        st1, st2 = update2(t1, t2, shift_c, vt_ref[:, j * tq:(j + 1) * tq], st1, st2)
        if j == i:
            (num1, l1), (num2, l2) = [(st[1][:A_HEAD_DIM], st[1][A_HEAD_DIM:A_HEAD_DIM + 1]) for st in (st1, st2)]
            ot = num1 * (1.0 / l1) - (lam_full / l2) * num2
            o = ot.T
            ms = jnp.mean(o * o, axis=1, keepdims=True)
            o = o * lax.rsqrt(ms + 1e-6) * g_ref[...] * (1.0 - lam_init)
            o_ref[i * tq:(i + 1) * tq, :] = o.astype(o_ref.dtype)


def _attn_a(proj, lam, subln, w_down, li, tq=ATTN_TILE, ahead=ATTN_A_AHEAD):
    b, s, _ = proj.shape
    tq = min(tq, s)
    f, d = w_down.shape[1:]
    slab = f // (b * A_HEADS)
    assert slab * b * A_HEADS == f and slab % BF16_ROW_TILE == 0
    lam_init = 0.8 - 0.6 * math.exp(-0.3 * li)
    slopes = np.asarray([2.0 ** (-8.0 * (h + 1) / A_HEADS) for h in range(A_HEADS)], np.float32)
    slopes = jnp.asarray(np.broadcast_to(slopes[:, None, None], (A_HEADS, 1, 128)))
    qi = OFF_A // A_HEAD_DIM
    ki = (OFF_A + A_WIDTH) // A_HEAD_DIM
    vi = (OFF_A + 2 * A_WIDTH) // A_HEAD_DIM

    def hspec(base):
        return pl.BlockSpec((None, s, A_HEAD_DIM), lambda bi, h, base=base: (bi, 0, base + h))

    return pl.pallas_call(
        functools.partial(_attn_a_kernel, tq=tq, lam_init=lam_init, ahead=ahead),
        grid=(b, A_HEADS),
        in_specs=[
            hspec(qi), hspec(ki), hspec(vi),
            pl.BlockSpec((None, 4, A_HALF), lambda bi, h: (li, 0, 0)),
            pl.BlockSpec((None, 1, A_HEAD_DIM), lambda bi, h: (li, 0, 0)),
            pl.BlockSpec((None, 1, 128), lambda bi, h: (h, 0, 0)),
            pl.BlockSpec((None, slab, d), lambda bi, h: (li, bi * A_HEADS + h, 0)),
        ],
        out_specs=[pl.BlockSpec((None, s, A_HEAD_DIM), lambda bi, h: (bi, 0, h)),
                   pl.BlockSpec((slab, d), lambda bi, h: (bi * A_HEADS + h, 0))],
        out_shape=[jax.ShapeDtypeStruct((b, s, A_WIDTH), jnp.bfloat16),
                   jax.ShapeDtypeStruct((f, d), jnp.bfloat16)],
        scratch_shapes=[
            pltpu.VMEM((A_HEAD_DIM, s), jnp.bfloat16),
            pltpu.VMEM((A_HEAD_DIM, s), jnp.bfloat16),
            pltpu.VMEM((s, A_HEAD_DIM), jnp.bfloat16),
            pltpu.VMEM((s, A_HEAD_DIM), jnp.bfloat16),
            pltpu.VMEM((A_VT_ROWS, s), jnp.bfloat16),
        ],
        compiler_params=_params(("parallel", "parallel")),
        name="attn_a",
    )(proj, proj, proj, lam, subln.reshape(-1, 1, A_HEAD_DIM), slopes, w_down)


def _attn_b_kernel(q_ref, k_ref, v_ref, u_ref, o_ref, kpad, vpad, table_ref, *, tq, pad):
    s_len = q_ref.shape[0]
    win = pad + tq
    width = u_ref.shape[1]

    @pl.when(pl.program_id(1) == 0)
    def _():
        rolled = pltpu.roll(jnp.broadcast_to(u_ref[...], (tq, width)), width - (tq - 1), 1,
                            stride=1, stride_axis=0)
        r = lax.broadcasted_iota(jnp.int32, (tq, win), 0)
        j = lax.broadcasted_iota(jnp.int32, (tq, win), 1)
        kc, qc = j // CHUNK, B_PAST_CHUNKS + r // CHUNK
        table_ref[...] = jnp.where((kc <= qc) & (kc >= qc - B_PAST_CHUNKS), rolled[:, :win] * LOG2E, NEG)

    kpad[0:pad, :] = jnp.zeros((pad, B_HEAD_DIM), kpad.dtype)
    vpad[0:pad, :] = jnp.zeros((pad, B_HEAD_DIM), vpad.dtype)
    kpad[pad:pad + s_len, :] = k_ref[...]
    vpad[pad:pad + s_len, :] = v_ref[...]
    table = table_ref[...]
    jcol = lax.broadcasted_iota(jnp.int32, (tq, win), 1)
    scale2 = B_HEAD_DIM ** -0.5 * LOG2E
    nq = s_len // tq

    def scores(i):
        qs = i * tq
        q = (q_ref[qs:qs + tq, :].astype(jnp.float32) * scale2).astype(jnp.bfloat16)
        s = _dot_nt(q, kpad[qs:qs + win, :]) + table
        if qs < pad:
            s = jnp.where(jcol >= pad - qs, s, NEG)
        return s

    nxt = scores(0)
    for i in range(nq):
        s = nxt
        if i + 1 < nq:
            nxt = scores(i + 1)
        qs = i * tq
        m = jnp.max(s, axis=1, keepdims=True)
        p = jnp.exp2(s - m)
        l = jnp.sum(p, axis=1, keepdims=True)
        o = _dot(p.astype(jnp.bfloat16), vpad[qs:qs + win, :]) / l
        o_ref[qs:qs + tq, :] = o.astype(o_ref.dtype)


def _band_bias_vector(rel_bias, tq, pad):
    n = pad + 2 * tq - 1
    n_hi = pad + tq - REL_CLIP
    lo = REL_CLIP - tq + 1
    assert n_hi >= 0 and lo >= 0
    rb = rel_bias.astype(jnp.float32)
    u = jnp.concatenate([jnp.repeat(rb[:, 2 * REL_CLIP:], n_hi, axis=1), rb[:, lo:2 * REL_CLIP][:, ::-1]], axis=1)
    assert u.shape[1] == n
    return jnp.pad(u, ((0, 0), (0, 1)))[:, None, :]


def _attn_b(proj, rel_bias, tq=ATTN_TILE):
    b, s, _ = proj.shape
    pad = B_PAST_CHUNKS * CHUNK
    width = pad + 2 * tq
    assert width & (width - 1) == 0
    u = _band_bias_vector(rel_bias, tq, pad)
    qi = OFF_B // B_HEAD_DIM
    ki = (OFF_B + B_WIDTH) // B_HEAD_DIM
    vi = (OFF_B + 2 * B_WIDTH) // B_HEAD_DIM

    def hspec(base):
        return pl.BlockSpec((None, s, B_HEAD_DIM), lambda h, bi, base=base: (bi, 0, base + h))

    return pl.pallas_call(
        functools.partial(_attn_b_kernel, tq=tq, pad=pad),
        grid=(B_HEADS, b),
        in_specs=[
            hspec(qi), hspec(ki), hspec(vi),
            pl.BlockSpec((None, 1, width), lambda h, bi: (h, 0, 0)),
        ],
        out_specs=pl.BlockSpec((None, s, B_HEAD_DIM), lambda h, bi: (bi, 0, h)),
        out_shape=jax.ShapeDtypeStruct((b, s, B_WIDTH), jnp.bfloat16),
        scratch_shapes=[
            pltpu.VMEM((pad + s, B_HEAD_DIM), jnp.bfloat16),
            pltpu.VMEM((pad + s, B_HEAD_DIM), jnp.bfloat16),
            pltpu.VMEM((tq, pad + tq), jnp.float32),
        ],
        compiler_params=_params(("arbitrary", "arbitrary")),
        name="attn_b",
    )(proj, proj, proj, u)


def _cumsum_rows(x):
    n = x.shape[0]
    row = lax.broadcasted_iota(jnp.int32, x.shape, 0)
    sh = 1
    while sh < n:
        x = x + jnp.where(row >= sh, pltpu.roll(x, sh, 0), 0.0)
        sh *= 2
    return x


def _mlstm_kernel(q_ref, k_ref, v_ref, og_ref, gate_ref, cq_ref, ck_ref, ng_ref, wo_ref, o_ref, wo16_ref,
                  qs_ref, ks_ref, state_ref, gt_ref, *, chunk, halo):
    wo16_ref[...] = wo_ref[...].astype(jnp.bfloat16)
    s_len = q_ref.shape[0]
    n_chunks = s_len // chunk
    pair = pl.program_id(1)
    lane = lax.broadcasted_iota(jnp.int32, (chunk, C_WIN), 1)
    glane = lax.broadcasted_iota(jnp.int32, (chunk, GATE_LANES), 1)
    trow = lax.broadcasted_iota(jnp.int32, (chunk, chunk), 0)
    tcol = lax.broadcasted_iota(jnp.int32, (chunk, chunk), 1)
    causal = tcol <= trow
    windows = ((0, lane < C_HEAD_DIM, C_HEAD_DIM),
               (C_PAIR - C_WIN, lane >= C_WIN - C_HEAD_DIM, 0))

    def conv_chunk(c, carry):
        st = pl.multiple_of(c * chunk, chunk)
        prev_st = pl.multiple_of(jnp.maximum(st - halo, 0), halo)
        keep = jnp.where(c > 0, 1.0, 0.0)
        for src, w_ref, dst, post in ((q_ref, cq_ref, qs_ref, 1.0),
                                      (k_ref, ck_ref, ks_ref, C_HEAD_DIM ** -0.5)):
            cur = src[pl.ds(st, chunk), :].astype(jnp.float32)
            prev = src[pl.ds(prev_st, halo), :].astype(jnp.float32) * keep
            xc = jnp.concatenate([prev, cur], axis=0)
            w = w_ref[...]
            y = w[CONV_K - 1:CONV_K, :] * cur
            for back in range(1, CONV_K):
                y = y + w[CONV_K - 1 - back:CONV_K - back, :] * pltpu.roll(xc, back, 0)[halo:, :]
            y = y * _sigmoid(y) * post
            dst[pl.ds(st, chunk), :] = y.astype(dst.dtype)
        return carry

    lax.fori_loop(0, n_chunks, conv_chunk, 0)
    state_ref[...] = jnp.zeros(state_ref.shape, jnp.float32)

    def step(c, ms):
        st = pl.multiple_of(c * chunk, chunk)
        g = gate_ref[pl.ds(st, chunk), :]
        bcum = _cumsum_rows(_log_sigmoid(g))
        gb = jnp.where(glane < C_HEADS, g, bcum)
        gt_ref[...] = gb.T
        ng = ng_ref[...]
        zero = jnp.zeros((), jnp.bfloat16)
        ops, qk, qc = [], [], []
        for a, (w0, valid, spare) in enumerate(windows):
            wcols = slice(w0, w0 + C_WIN)
            q = jnp.where(valid, qs_ref[pl.ds(st, chunk), wcols], zero)
            k = jnp.where(valid, ks_ref[pl.ds(st, chunk), wcols], zero)
            v = jnp.where(valid, v_ref[pl.ds(st, chunk), wcols], zero)
            v_ext = jnp.where(lane == spare, jnp.ones((), v.dtype), v)
            ops.append((q, k, v_ext))
            qk.append(_dot_nt(q, k))
            qc.append(_dot(q, state_ref[a].astype(jnp.bfloat16)))

        gate, m_out = [], []
        for a in range(2):
            h = 2 * pair + a
            m = ms[a]
            i_row = gt_ref[pl.ds(h, 1), :]
            b_row = gt_ref[pl.ds(C_HEADS + h, 1), :]
            i_col = jnp.sum(jnp.where(glane == h, gb, 0.0), axis=1, keepdims=True)
            b_col = jnp.sum(jnp.where(glane == C_HEADS + h, gb, 0.0), axis=1, keepdims=True)
            d = jnp.where(causal, b_col + (i_row - b_row), NEG)
            inter = b_col + m
            m_t = jnp.maximum(inter, jnp.max(d, axis=1, keepdims=True))
            b_last = b_col[chunk - 1:chunk, :]
            g_col = b_last - b_col + i_col
            m_new = jnp.maximum(b_last + m, jnp.max(g_col, axis=0, keepdims=True))
            gate.append((d, inter, m_t, jnp.exp(g_col - m_new), jnp.exp(b_last + m - m_new)))
            m_out.append(m_new)

        nds = []
        for a in range(2):
            d, inter, m_t = gate[a][:3]
            sw = qk[a] * jnp.exp(d - m_t)
            nds.append(_dot(sw.astype(jnp.bfloat16), ops[a][2]) + jnp.exp(inter - m_t) * qc[a])

        for a in range(2):
            k, v_ext = ops[a][1:]
            wk, decay = gate[a][3:]
            kw_t = (k.astype(jnp.float32) * wk).T.astype(jnp.bfloat16)
            state_ref[a] = decay * state_ref[a] + _dot(kw_t, v_ext)

        hn = []
        for a, (w0, valid, spare) in enumerate(windows):
            nd, m_t = nds[a], gate[a][2]
            den = jnp.sum(jnp.where(lane == spare, nd, 0.0), axis=1, keepdims=True)
            num = jnp.where(valid, nd, 0.0)
            hh = num / jnp.maximum(jnp.abs(den), jnp.exp(-m_t))
            msq = jnp.sum(hh * hh, axis=1, keepdims=True) * (1.0 / C_HEAD_DIM)
            hn.append(hh * lax.rsqrt(msq + 1e-6) * ng[:, w0:w0 + C_WIN])

        mid = C_PAIR - C_WIN
        hn_pair = jnp.concatenate([hn[0][:, :mid], hn[0][:, mid:] + hn[1][:, :C_WIN - mid], hn[1][:, C_WIN - mid:]],
                                  axis=1)
        og = og_ref[pl.ds(st, chunk), :].astype(jnp.float32)
        o_ref[pl.ds(st, chunk), :] = (_sigmoid(og) * hn_pair).astype(o_ref.dtype)
        return tuple(m_out)

    m0 = jnp.zeros((1, 1), jnp.float32)
    lax.fori_loop(0, n_chunks, step, (m0, m0), unroll=4)


def _mlstm(proj, gates, conv_w, norm_g, w_out, li, chunk=MLSTM_CHUNK, halo=MLSTM_HALO):
    b, s, _ = proj.shape
    chunk = min(chunk, s)
    n_pairs = C_HEADS // 2
    base = OFF_C // C_PAIR
    d_in, d = w_out.shape[1:]
    slab = d_in // (b * n_pairs)
    assert slab * b * n_pairs == d_in and slab % BF16_ROW_TILE == 0

    def pspec(t):
        return pl.BlockSpec((None, s, C_PAIR), lambda bi, p, t=t: (bi, 0, base + n_pairs * t + p))

    return pl.pallas_call(
        functools.partial(_mlstm_kernel, chunk=chunk, halo=halo),
        grid=(b, n_pairs),
        in_specs=[
            pspec(0), pspec(1), pspec(2), pspec(3),
            pl.BlockSpec((None, s, GATE_LANES), lambda bi, p: (bi, 0, 0)),
            pl.BlockSpec((None, CONV_K, C_PAIR), lambda bi, p: (li, 0, p)),
            pl.BlockSpec((None, CONV_K, C_PAIR), lambda bi, p: (li, 0, n_pairs + p)),
            pl.BlockSpec((1, C_PAIR), lambda bi, p: (0, 0)),
            pl.BlockSpec((None, slab, d), lambda bi, p: (li, bi * n_pairs + p, 0)),
        ],
        out_specs=[pl.BlockSpec((None, s, C_PAIR), lambda bi, p: (bi, 0, p)),
                   pl.BlockSpec((slab, d), lambda bi, p: (bi * n_pairs + p, 0))],
        out_shape=[jax.ShapeDtypeStruct((b, s, C_WIDTH), jnp.bfloat16),
                   jax.ShapeDtypeStruct((d_in, d), jnp.bfloat16)],
        scratch_shapes=[
            pltpu.VMEM((s, C_PAIR), jnp.bfloat16),
            pltpu.VMEM((s, C_PAIR), jnp.bfloat16),
            pltpu.VMEM((2, C_WIN, C_WIN), jnp.float32),
            pltpu.VMEM((GATE_LANES, chunk), jnp.float32),
        ],
        compiler_params=_params(("parallel", "parallel")),
        name="mlstm",
    )(proj, proj, proj, proj, gates, conv_w, conv_w, norm_g, w_out)


def _out_proj_ln_kernel(ya_ref, yb_ref, yc_ref, w16_ref, x_ref, g_ref, b_ref, o32_ref, o16_ref, *, alpha):
    half = x_ref.shape[0] // 2

    def project(r):
        rows = slice(r * half, (r + 1) * half)
        mixed = jnp.concatenate([ya_ref[rows, :], yb_ref[rows, :], yc_ref[rows, :]], axis=1)
        return _dot(mixed, w16_ref[...])

    nxt = project(0)
    for r in range(2):
        mix = nxt
        if r == 0:
            nxt = project(1)
        rows = slice(r * half, (r + 1) * half)
        y = _layer_norm(alpha * x_ref[rows, :] + mix, g_ref[...], b_ref[...])
        o32_ref[rows, :] = y
        o16_ref[rows, :] = y.astype(o16_ref.dtype)


def _out_proj_ln(ya, yb, yc, wo16, x, g, b, li, alpha, tm=OUT_PROJ_ROWS):
    m, d = x.shape
    tm = min(tm, m)
    row = lambda w: pl.BlockSpec((tm, w), lambda i: (i, 0))
    vec = pl.BlockSpec((None, 1, d), lambda i: (li, 0, 0))
    return pl.pallas_call(
        functools.partial(_out_proj_ln_kernel, alpha=alpha),
        grid=(m // tm,),
        in_specs=[row(ya.shape[1]), row(yb.shape[1]), row(yc.shape[1]),
                  pl.BlockSpec(wo16.shape, lambda i: (0, 0), pipeline_mode=pl.Buffered(1)),
                  row(d), vec, vec],
        out_specs=[row(d), row(d)],
        out_shape=[jax.ShapeDtypeStruct((m, d), jnp.float32),
                   jax.ShapeDtypeStruct((m, d), jnp.bfloat16)],
        compiler_params=_params(("parallel",)),
        name="out_proj_ln",
    )(ya, yb, yc, wo16, x, g, b)


def _ffn_up_kernel(x_ref, wg_ref, wu_ref, o_ref, wg16_ref, wu16_ref):
    @pl.when(pl.program_id(1) == 0)
    def _():
        wg16_ref[...] = wg_ref[...].astype(jnp.bfloat16)
        wu16_ref[...] = wu_ref[...].astype(jnp.bfloat16)

    slab = x_ref.shape[0] // FFN_SLABS

    def products(r):
        x = x_ref[r * slab:(r + 1) * slab, :]
        return _dot(x, wg16_ref[...]), _dot(x, wu16_ref[...])

    nxt = products(0)
    for r in range(FFN_SLABS):
        a, u = nxt
        if r + 1 < FFN_SLABS:
            nxt = products(r + 1)
        o_ref[r * slab:(r + 1) * slab, :] = (a * _sigmoid(a) * u).astype(o_ref.dtype)


def _ffn_up(xb, w_gate, w_up, li, tm=FFN_UP_ROWS, tn=FFN_UP_COLS):
    m, k = xb.shape
    n = w_gate.shape[2]
    tm, tn = min(tm, m), min(tn, n)
    return pl.pallas_call(
        _ffn_up_kernel,
        grid=(n // tn, m // tm),
        in_specs=[
            pl.BlockSpec((tm, k), lambda j, i: (i, 0)),
            pl.BlockSpec((None, k, tn), lambda j, i: (li, 0, j)),
            pl.BlockSpec((None, k, tn), lambda j, i: (li, 0, j)),
        ],
        out_specs=pl.BlockSpec((tm, tn), lambda j, i: (i, j)),
        out_shape=jax.ShapeDtypeStruct((m, n), jnp.bfloat16),
        scratch_shapes=[pltpu.VMEM((k, tn), jnp.bfloat16), pltpu.VMEM((k, tn), jnp.bfloat16)],
        compiler_params=_params(("arbitrary", "arbitrary")),
        name="ffn_up",
    )(xb, w_gate, w_up)


def _ffn_down_ln_kernel(h_ref, w_ref, x_ref, g_ref, b_ref, *rest, alpha, next_gates, single_k):
    if next_gates:
        wg_ref, gb_ref, o32_ref, o16_ref, gate_ref = rest
    else:
        (o32_ref,) = rest

    def finish(ffn):
        y = _layer_norm(alpha * x_ref[...] + ffn, g_ref[...], b_ref[...])
        o32_ref[...] = y
        if next_gates:
            y16 = y.astype(jnp.bfloat16)
            o16_ref[...] = y16
            gate_ref[...] = _gate_proj(y16, wg_ref, gb_ref)

    if single_k:
        finish(_dot(h_ref[...], w_ref[...]))
        return

    kk = pl.program_id(1)

    @pl.when(kk == 0)
    def _():
        o32_ref[...] = jnp.zeros(o32_ref.shape, jnp.float32)

    o32_ref[...] += _dot(h_ref[...], w_ref[...])

    @pl.when(kk == pl.num_programs(1) - 1)
    def _():
        finish(o32_ref[...])


def _ffn_down_ln(hid, wd16, x, g, b, li, alpha, w_in=None, gb_next=None, tm=FFN_DOWN_ROWS, tk=None):
    m, f = hid.shape
    d = wd16.shape[1]
    tm, tk = min(tm, m), f if tk is None else min(tk, f)
    next_gates = w_in is not None
    vec = pl.BlockSpec((None, 1, d), lambda i, kk: (li, 0, 0))
    row = pl.BlockSpec((tm, d), lambda i, kk: (i, 0))
    in_specs = [pl.BlockSpec((tm, tk), lambda i, kk: (i, kk)),
                pl.BlockSpec((tk, d), lambda i, kk: (kk, 0),
                             **({"pipeline_mode": pl.Buffered(1)} if tk == f else {})),
                row, vec, vec]
    out_specs = [row]
    out_shape = [jax.ShapeDtypeStruct((m, d), jnp.float32)]
    args = [hid, wd16, x, g, b]
    if next_gates:
        in_specs += [_gate_weight_spec(li + 1, d), pl.BlockSpec((1, GATE_LANES), lambda i, kk: (0, 0))]
        out_specs += [row, pl.BlockSpec((tm, GATE_LANES), lambda i, kk: (i, 0))]
        out_shape += [jax.ShapeDtypeStruct((m, d), jnp.bfloat16),
                      jax.ShapeDtypeStruct((m, GATE_LANES), jnp.float32)]
        args += [w_in, gb_next]
    return pl.pallas_call(
        functools.partial(_ffn_down_ln_kernel, alpha=alpha, next_gates=next_gates, single_k=tk == f),
        grid=(m // tm, f // tk),
        in_specs=in_specs,
        out_specs=out_specs,
        out_shape=out_shape,
        compiler_params=_params(("parallel", "arbitrary")),
        name="ffn_down_ln",
    )(*args)


def _gate_bias_tile(gate_bias, li):
    return jnp.pad(gate_bias[li], (0, GATE_LANES - 2 * C_HEADS)).reshape(1, GATE_LANES)


def kernel(x, w_in, gate_bias, conv_w, lam, subln_a, norm_c, rel_bias, w_out, ln1_g, ln1_b,
           w_gate, w_up, w_down, ln2_g, ln2_b):
    bsz, s_len, d = x.shape
    depth = w_in.shape[0]
    alpha = (2.0 * depth) ** 0.25
    m = bsz * s_len
    x32 = x.reshape(m, d)
    ln1_g, ln1_b, ln2_g, ln2_b = (p.reshape(depth, 1, d) for p in (ln1_g, ln1_b, ln2_g, ln2_b))
    w_in_t = jnp.swapaxes(w_in, 1, 2)
    x16, gates = _cast_gate(x32, w_in_t, _gate_bias_tile(gate_bias, 0), 0)
    for li in range(depth):
        proj = _in_proj(x16, w_in_t, li).reshape(bsz, s_len, OFF_G)
        gates = gates.reshape(bsz, s_len, GATE_LANES)
        ya, wd16 = _attn_a(proj, lam, subln_a, w_down, li)
        yb = _attn_b(proj, rel_bias[li])
        norm_g = jnp.tile(norm_c[li], 2).reshape(1, C_PAIR)
        yc, wo16 = _mlstm(proj, gates, conv_w, norm_g, w_out, li)
        x32, x16 = _out_proj_ln(ya.reshape(m, A_WIDTH), yb.reshape(m, B_WIDTH), yc.reshape(m, C_WIDTH),
                                wo16, x32, ln1_g, ln1_b, li, alpha)
        hid = _ffn_up(x16, w_gate, w_up, li)
        if li + 1 < depth:
            x32, x16, gates = _ffn_down_ln(hid, wd16, x32, ln2_g, ln2_b, li, alpha,
                                           w_in_t, _gate_bias_tile(gate_bias, li + 1))
        else:
            (x32,) = _ffn_down_ln(hid, wd16, x32, ln2_g, ln2_b, li, alpha)
    return x32.reshape(bsz, s_len, d)
```

```python
import functools
import math

import numpy as np
import jax
import jax.numpy as jnp
from jax import lax
from jax.experimental import pallas as pl
from jax.experimental.pallas import tpu as pltpu

CHUNK = 64
A_HEADS = 4
A_HEAD_DIM = 128
A_HALF = 64
B_HEADS = 6
B_HEAD_DIM = 128
B_PAST_CHUNKS = 8
REL_CLIP = 256
C_HEADS = 4
C_HEAD_DIM = 192
C_PAIR = 2 * C_HEAD_DIM
C_WIN = 256
CONV_K = 4
NEG = -1e30
GATE_LANES = 128

A_WIDTH = A_HEADS * A_HEAD_DIM
B_WIDTH = B_HEADS * B_HEAD_DIM
C_WIDTH = C_HEADS * C_HEAD_DIM
OFF_A = 0
OFF_B = 3 * A_WIDTH
OFF_C = OFF_B + 3 * B_WIDTH
OFF_G = OFF_C + 4 * C_WIDTH
N_GATES = 2 * C_HEADS

BF16_ROW_TILE = 16
A_VT_ROWS = A_HEAD_DIM + BF16_ROW_TILE
LOG2E = math.log2(math.e)

V7X_VMEM_BYTES = 64 * 1024 * 1024
VMEM_LIMIT = V7X_VMEM_BYTES * 7 // 8
CAST_ROWS = 1024
PROJ_ROWS = 2048
PROJ_TILE = 768
OUT_PROJ_ROWS = 512
FFN_UP_ROWS = 2048
FFN_UP_COLS = 512
FFN_SLABS = 4
FFN_DOWN_ROWS = 256
ATTN_TILE = 256
ATTN_A_AHEAD = 2
MLSTM_CHUNK = 256
MLSTM_HALO = BF16_ROW_TILE

_NT = (((1,), (1,)), ((), ()))


def _params(sem):
    return pltpu.CompilerParams(dimension_semantics=sem, vmem_limit_bytes=VMEM_LIMIT)


def _dot(a, b):
    return jnp.dot(a, b, preferred_element_type=jnp.float32)


def _dot_nt(a, b):
    return lax.dot_general(a, b, _NT, preferred_element_type=jnp.float32)


def _log_sigmoid(x):
    return jnp.minimum(x, 0.0) - jnp.log(1.0 + jnp.exp(-jnp.abs(x)))


def _sigmoid(x):
    return 1.0 / (1.0 + jnp.exp(-x))


def _layer_norm(z, g, b):
    mu = jnp.mean(z, axis=1, keepdims=True)
    zc = z - mu
    var = jnp.mean(zc * zc, axis=1, keepdims=True)
    return zc * lax.rsqrt(var + 1e-5) * g + b


def _gate_weight_spec(li, d):
    assert OFF_G % N_GATES == 0
    return pl.BlockSpec((None, N_GATES, d), lambda *_: (li, OFF_G // N_GATES, 0))


def _gate_proj(x16, wg_ref, gb_ref):
    wg = wg_ref[...].astype(jnp.bfloat16)
    wg = jnp.concatenate([wg, jnp.zeros((GATE_LANES - N_GATES, wg.shape[1]), wg.dtype)], axis=0)
    return _dot_nt(x16, wg) + gb_ref[...]


def _cast_gate_kernel(x_ref, wg_ref, gb_ref, x16_ref, g_ref):
    x16 = x_ref[...].astype(jnp.bfloat16)
    x16_ref[...] = x16
    g_ref[...] = _gate_proj(x16, wg_ref, gb_ref)


def _cast_gate(x, w_in_t, gb, li, tm=CAST_ROWS):
    m, d = x.shape
    tm = min(tm, m)
    return pl.pallas_call(
        _cast_gate_kernel,
        grid=(m // tm,),
        in_specs=[
            pl.BlockSpec((tm, d), lambda i: (i, 0)),
            _gate_weight_spec(li, d),
            pl.BlockSpec((1, GATE_LANES), lambda i: (0, 0)),
        ],
        out_specs=[pl.BlockSpec((tm, d), lambda i: (i, 0)),
                   pl.BlockSpec((tm, GATE_LANES), lambda i: (i, 0))],
        out_shape=[jax.ShapeDtypeStruct((m, d), jnp.bfloat16),
                   jax.ShapeDtypeStruct((m, GATE_LANES), jnp.float32)],
        compiler_params=_params(("parallel",)),
        name="cast_gate",
    )(x, w_in_t, gb)


def _in_proj_kernel(x_ref, w_ref, o_ref, w16_ref):
    @pl.when(pl.program_id(1) == 0)
    def _():
        w16_ref[...] = w_ref[...].astype(jnp.bfloat16)

    o_ref[...] = _dot_nt(x_ref[...], w16_ref[...]).astype(o_ref.dtype)


def _in_proj(xb, w_in_t, li, tm=PROJ_ROWS, tn=PROJ_TILE):
    m, k = xb.shape
    tm = min(tm, m)
    n = OFF_G
    return pl.pallas_call(
        _in_proj_kernel,
        grid=(n // tn, m // tm),
        in_specs=[
            pl.BlockSpec((tm, k), lambda j, i: (i, 0)),
            pl.BlockSpec((None, tn, k), lambda j, i: (li, j, 0)),
        ],
        out_specs=pl.BlockSpec((tm, tn), lambda j, i: (i, j)),
        out_shape=jax.ShapeDtypeStruct((m, n), jnp.bfloat16),
        scratch_shapes=[pltpu.VMEM((tn, k), jnp.bfloat16)],
        compiler_params=_params(("arbitrary", "arbitrary")),
        name="in_proj",
    )(xb, w_in_t)


def _attn_a_kernel(q_ref, k_ref, v_ref, lam_ref, g_ref, slope_ref, wd_ref, o_ref, wd16_ref,
                   qt1_ref, qt2_ref, k1_ref, k2_ref, vt_ref, *, tq, lam_init, ahead):
    wd16_ref[...] = wd_ref[...].astype(jnp.bfloat16)
    s_len = q_ref.shape[0]
    nq = s_len // tq
    slope2 = slope_ref[0:1, 0:1] * LOG2E
    lf = lam_ref[...]
    lam_full = (jnp.exp(jnp.sum(lf[0:1] * lf[1:2], axis=1, keepdims=True))
                - jnp.exp(jnp.sum(lf[2:3] * lf[3:4], axis=1, keepdims=True)) + lam_init)
    feat = lax.broadcasted_iota(jnp.int32, (A_HEAD_DIM, tq), 0)
    lane = lax.broadcasted_iota(jnp.int32, (tq, A_HEAD_DIM), 1)
    key_bias = slope2 * lax.broadcasted_iota(jnp.int32, (tq, A_HEAD_DIM), 0).astype(jnp.float32)
    kb_hi = key_bias.astype(jnp.bfloat16).astype(jnp.float32)
    kb_lo = (key_bias - kb_hi).astype(jnp.bfloat16).astype(jnp.float32)
    ones_row = jnp.where(lax.broadcasted_iota(jnp.int32, (A_VT_ROWS - A_HEAD_DIM, tq), 0) == 0, 1.0, 0.0)
    for i in range(nq):
        rows = slice(i * tq, (i + 1) * tq)
        qt = (q_ref[rows, :].astype(jnp.float32) * (A_HALF ** -0.5 * LOG2E)).T
        qt1_ref[:, rows] = jnp.where(feat < A_HALF, qt, jnp.where(feat < A_HALF + 2, 1.0, 0.0)).astype(jnp.bfloat16)
        qt2_ref[:, rows] = jnp.where(feat >= A_HALF, qt, jnp.where(feat < 2, 1.0, 0.0)).astype(jnp.bfloat16)
        k = k_ref[rows, :].astype(jnp.float32)
        k1 = jnp.where(lane < A_HALF, k, jnp.where(lane == A_HALF, kb_hi, jnp.where(lane == A_HALF + 1, kb_lo, 0.0)))
        k2 = jnp.where(lane >= A_HALF, k, jnp.where(lane == 0, kb_hi, jnp.where(lane == 1, kb_lo, 0.0)))
        k1_ref[rows, :] = k1.astype(jnp.bfloat16)
        k2_ref[rows, :] = k2.astype(jnp.bfloat16)
        vt_ref[0:A_HEAD_DIM, rows] = v_ref[rows, :].astype(jnp.float32).T.astype(jnp.bfloat16)
        vt_ref[A_HEAD_DIM:, rows] = ones_row.astype(jnp.bfloat16)

    key = lax.broadcasted_iota(jnp.int32, (tq, tq), 0)
    qry = lax.broadcasted_iota(jnp.int32, (tq, tq), 1)
    rel = (qry - key).astype(jnp.float32)
    bias_diag = slope2 * (rel - jnp.abs(rel))
    diag_ok = (key // CHUNK) <= (qry // CHUNK)

    def update2(t1, t2, shift_c, vt, st1, st2):
        ts, sts = (t1, t2), (st1, st2)
        m_new = [jnp.maximum(st[0], jnp.max(t, axis=0, keepdims=True) + shift_c) for t, st in zip(ts, sts)]
        alpha = [jnp.exp2(st[0] - mn) for st, mn in zip(sts, m_new)]
        p = [jnp.exp2(t - (mn - shift_c)) for t, mn in zip(ts, m_new)]
        pv = [_dot(vt, pp.astype(jnp.bfloat16)) for pp in p]
        acc = [a * st[1] + x for a, st, x in zip(alpha, sts, pv)]
        return (m_new[0], acc[0]), (m_new[1], acc[1])

    def scores(i, j):
        t1 = _dot(k1_ref[j * tq:(j + 1) * tq, :], qt1_ref[:, i * tq:(i + 1) * tq])
        t2 = _dot(k2_ref[j * tq:(j + 1) * tq, :], qt2_ref[:, i * tq:(i + 1) * tq])
        if j < i:
            return t1, t2
        return jnp.where(diag_ok, t1 + bias_diag, NEG), jnp.where(diag_ok, t2 + bias_diag, NEG)

    pairs = [(i, j) for i in range(nq) for j in range(i + 1)]
    pending = [scores(*pr) for pr in pairs[:ahead]]
    fresh = (jnp.full((1, tq), NEG, jnp.float32), jnp.zeros((A_VT_ROWS, tq), jnp.float32))
    for n, (i, j) in enumerate(pairs):
        if j == 0:
            st1 = st2 = fresh
        t1, t2 = pending.pop(0)
        if n + ahead < len(pairs):
            pending.append(scores(*pairs[n + ahead]))
        if j < i:
            shift_c = -slope2 * float((i - j) * tq)
        else:
            shift_c = jnp.zeros((1, 1), jnp.float32)
        st1, st2 = update2(t1, t2, shift_c, vt_ref[:, j * tq:(j + 1) * tq], st1, st2)
        if j == i:
            (num1, l1), (num2, l2) = [(st[1][:A_HEAD_DIM], st[1][A_HEAD_DIM:A_HEAD_DIM + 1]) for st in (st1, st2)]
            ot = num1 * (1.0 / l1) - (lam_full / l2) * num2
            o = ot.T
            ms = jnp.mean(o * o, axis=1, keepdims=True)
            o = o * lax.rsqrt(ms + 1e-6) * g_ref[...] * (1.0 - lam_init)
            o_ref[i * tq:(i + 1) * tq, :] = o.astype(o_ref.dtype)


def _attn_a(proj, lam, subln, w_down, li, tq=ATTN_TILE, ahead=ATTN_A_AHEAD):
    b, s, _ = proj.shape
    tq = min(tq, s)
    f, d = w_down.shape[1:]
    slab = f // (b * A_HEADS)
    assert slab * b * A_HEADS == f and slab % BF16_ROW_TILE == 0
    lam_init = 0.8 - 0.6 * math.exp(-0.3 * li)
    slopes = np.asarray([2.0 ** (-8.0 * (h + 1) / A_HEADS) for h in range(A_HEADS)], np.float32)
    slopes = jnp.asarray(np.broadcast_to(slopes[:, None, None], (A_HEADS, 1, 128)))
    qi = OFF_A // A_HEAD_DIM
    ki = (OFF_A + A_WIDTH) // A_HEAD_DIM
    vi = (OFF_A + 2 * A_WIDTH) // A_HEAD_DIM

    def hspec(base):
        return pl.BlockSpec((None, s, A_HEAD_DIM), lambda bi, h, base=base: (bi, 0, base + h))

    return pl.pallas_call(
        functools.partial(_attn_a_kernel, tq=tq, lam_init=lam_init, ahead=ahead),
        grid=(b, A_HEADS),
        in_specs=[
            hspec(qi), hspec(ki), hspec(vi),
            pl.BlockSpec((None, 4, A_HALF), lambda bi, h: (li, 0, 0)),
            pl.BlockSpec((None, 1, A_HEAD_DIM), lambda bi, h: (li, 0, 0)),
            pl.BlockSpec((None, 1, 128), lambda bi, h: (h, 0, 0)),
            pl.BlockSpec((None, slab, d), lambda bi, h: (li, bi * A_HEADS + h, 0)),
        ],
        out_specs=[pl.BlockSpec((None, s, A_HEAD_DIM), lambda bi, h: (bi, 0, h)),
                   pl.BlockSpec((slab, d), lambda bi, h: (bi * A_HEADS + h, 0))],
        out_shape=[jax.ShapeDtypeStruct((b, s, A_WIDTH), jnp.bfloat16),
                   jax.ShapeDtypeStruct((f, d), jnp.bfloat16)],
        scratch_shapes=[
            pltpu.VMEM((A_HEAD_DIM, s), jnp.bfloat16),
            pltpu.VMEM((A_HEAD_DIM, s), jnp.bfloat16),
            pltpu.VMEM((s, A_HEAD_DIM), jnp.bfloat16),
            pltpu.VMEM((s, A_HEAD_DIM), jnp.bfloat16),
            pltpu.VMEM((A_VT_ROWS, s), jnp.bfloat16),
        ],
        compiler_params=_params(("parallel", "parallel")),
        name="attn_a",
    )(proj, proj, proj, lam, subln.reshape(-1, 1, A_HEAD_DIM), slopes, w_down)


def _attn_b_kernel(q_ref, k_ref, v_ref, u_ref, o_ref, table_ref, *, tq, pad):
    s_len = q_ref.shape[0]
    win = pad + tq
    width = u_ref.shape[1]

    @pl.when(pl.program_id(1) == 0)
    def _():
        rolled = pltpu.roll(jnp.broadcast_to(u_ref[...], (tq, width)), width - (tq - 1), 1,
                            stride=1, stride_axis=0)
        r = lax.broadcasted_iota(jnp.int32, (tq, win), 0)
        j = lax.broadcasted_iota(jnp.int32, (tq, win), 1)
        kc, qc = j // CHUNK, B_PAST_CHUNKS + r // CHUNK
        table_ref[...] = jnp.where((kc <= qc) & (kc >= qc - B_PAST_CHUNKS), rolled[:, :win] * LOG2E, NEG)

    scale2 = B_HEAD_DIM ** -0.5 * LOG2E
    nq = s_len // tq

    def key_start(i):
        return max(i * tq - pad, 0)

    def scores(i):
        qs, k0 = i * tq, key_start(i)
        q = (q_ref[qs:qs + tq, :].astype(jnp.float32) * scale2).astype(jnp.bfloat16)
        return _dot_nt(q, k_ref[k0:qs + tq, :]) + table_ref[:, k0 - (qs - pad):]

    nxt = scores(0)
    for i in range(nq):
        s = nxt
        if i + 1 < nq:
            nxt = scores(i + 1)
        qs = i * tq
        m = jnp.max(s, axis=1, keepdims=True)
        p = jnp.exp2(s - m)
        l = jnp.sum(p, axis=1, keepdims=True)
        o = _dot(p.astype(jnp.bfloat16), v_ref[key_start(i):qs + tq, :]) / l
        o_ref[qs:qs + tq, :] = o.astype(o_ref.dtype)


def _band_bias_vector(rel_bias, tq, pad):
    n = pad + 2 * tq - 1
    n_hi = pad + tq - REL_CLIP
    lo = REL_CLIP - tq + 1
    assert n_hi >= 0 and lo >= 0
    rb = rel_bias.astype(jnp.float32)
    u = jnp.concatenate([jnp.repeat(rb[:, 2 * REL_CLIP:], n_hi, axis=1), rb[:, lo:2 * REL_CLIP][:, ::-1]], axis=1)
    assert u.shape[1] == n
    return jnp.pad(u, ((0, 0), (0, 1)))[:, None, :]


def _attn_b(proj, rel_bias, tq=ATTN_TILE):
    b, s, _ = proj.shape
    pad = B_PAST_CHUNKS * CHUNK
    width = pad + 2 * tq
    assert width & (width - 1) == 0
    u = _band_bias_vector(rel_bias, tq, pad)
    qi = OFF_B // B_HEAD_DIM
    ki = (OFF_B + B_WIDTH) // B_HEAD_DIM
    vi = (OFF_B + 2 * B_WIDTH) // B_HEAD_DIM

    def hspec(base):
        return pl.BlockSpec((None, s, B_HEAD_DIM), lambda h, bi, base=base: (bi, 0, base + h))

    return pl.pallas_call(
        functools.partial(_attn_b_kernel, tq=tq, pad=pad),
        grid=(B_HEADS, b),
        in_specs=[
            hspec(qi), hspec(ki), hspec(vi),
            pl.BlockSpec((None, 1, width), lambda h, bi: (h, 0, 0)),
        ],
        out_specs=pl.BlockSpec((None, s, B_HEAD_DIM), lambda h, bi: (bi, 0, h)),
        out_shape=jax.ShapeDtypeStruct((b, s, B_WIDTH), jnp.bfloat16),
        scratch_shapes=[
            pltpu.VMEM((tq, pad + tq), jnp.float32),
        ],
        compiler_params=_params(("arbitrary", "arbitrary")),
        name="attn_b",
    )(proj, proj, proj, u)


def _cumsum_rows(x):
    n = x.shape[0]
    row = lax.broadcasted_iota(jnp.int32, x.shape, 0)
    sh = 1
    while sh < n:
        x = x + jnp.where(row >= sh, pltpu.roll(x, sh, 0), 0.0)
        sh *= 2
    return x


def _mlstm_kernel(q_ref, k_ref, v_ref, og_ref, gate_ref, cq_ref, ck_ref, ng_ref, wo_ref, o_ref, wo16_ref,
                  qs_ref, ks_ref, state_ref, gt_ref, *, chunk, halo):
    wo16_ref[...] = wo_ref[...].astype(jnp.bfloat16)
    s_len = q_ref.shape[0]
    n_chunks = s_len // chunk
    pair = pl.program_id(1)
    lane = lax.broadcasted_iota(jnp.int32, (chunk, C_WIN), 1)
    glane = lax.broadcasted_iota(jnp.int32, (chunk, GATE_LANES), 1)
    trow = lax.broadcasted_iota(jnp.int32, (chunk, chunk), 0)
    tcol = lax.broadcasted_iota(jnp.int32, (chunk, chunk), 1)
    causal = tcol <= trow
    windows = ((0, lane < C_HEAD_DIM, C_HEAD_DIM),
               (C_PAIR - C_WIN, lane >= C_WIN - C_HEAD_DIM, 0))

    def conv_chunk(c, carry):
        st = pl.multiple_of(c * chunk, chunk)
        prev_st = pl.multiple_of(jnp.maximum(st - halo, 0), halo)
        keep = jnp.where(c > 0, 1.0, 0.0)
        for src, w_ref, dst, post in ((q_ref, cq_ref, qs_ref, 1.0),
                                      (k_ref, ck_ref, ks_ref, C_HEAD_DIM ** -0.5)):
            cur = src[pl.ds(st, chunk), :].astype(jnp.float32)
            prev = src[pl.ds(prev_st, halo), :].astype(jnp.float32) * keep
            xc = jnp.concatenate([prev, cur], axis=0)
            w = w_ref[...]
            y = w[CONV_K - 1:CONV_K, :] * cur
            for back in range(1, CONV_K):
                y = y + w[CONV_K - 1 - back:CONV_K - back, :] * pltpu.roll(xc, back, 0)[halo:, :]
            y = y * _sigmoid(y) * post
            dst[pl.ds(st, chunk), :] = y.astype(dst.dtype)
        return carry

    lax.fori_loop(0, n_chunks, conv_chunk, 0)
    state_ref[...] = jnp.zeros(state_ref.shape, jnp.float32)

    def step(c, ms):
        st = pl.multiple_of(c * chunk, chunk)
        g = gate_ref[pl.ds(st, chunk), :]
        bcum = _cumsum_rows(_log_sigmoid(g))
        gb = jnp.where(glane < C_HEADS, g, bcum)
        gt_ref[...] = gb.T
        ng = ng_ref[...]
        zero = jnp.zeros((), jnp.bfloat16)
        ops, qk, qc = [], [], []
        for a, (w0, valid, spare) in enumerate(windows):
            wcols = slice(w0, w0 + C_WIN)
            q = jnp.where(valid, qs_ref[pl.ds(st, chunk), wcols], zero)
            k = jnp.where(valid, ks_ref[pl.ds(st, chunk), wcols], zero)
            v = jnp.where(valid, v_ref[pl.ds(st, chunk), wcols], zero)
            v_ext = jnp.where(lane == spare, jnp.ones((), v.dtype), v)
            ops.append((q, k, v_ext))
            qk.append(_dot_nt(q, k))
            qc.append(_dot(q, state_ref[a].astype(jnp.bfloat16)))

        gate, m_out = [], []
        for a in range(2):
            h = 2 * pair + a
            m = ms[a]
            i_row = gt_ref[pl.ds(h, 1), :]
            b_row = gt_ref[pl.ds(C_HEADS + h, 1), :]
            i_col = jnp.sum(jnp.where(glane == h, gb, 0.0), axis=1, keepdims=True)
            b_col = jnp.sum(jnp.where(glane == C_HEADS + h, gb, 0.0), axis=1, keepdims=True)
            d = jnp.where(causal, b_col + (i_row - b_row), NEG)
            inter = b_col + m
            m_t = jnp.maximum(inter, jnp.max(d, axis=1, keepdims=True))
            b_last = b_col[chunk - 1:chunk, :]
            g_col = b_last - b_col + i_col
            m_new = jnp.maximum(b_last + m, jnp.max(g_col, axis=0, keepdims=True))
            gate.append((d, inter, m_t, jnp.exp(g_col - m_new), jnp.exp(b_last + m - m_new)))
            m_out.append(m_new)

        nds = []
        for a in range(2):
            d, inter, m_t = gate[a][:3]
            sw = qk[a] * jnp.exp(d - m_t)
            nds.append(_dot(sw.astype(jnp.bfloat16), ops[a][2]) + jnp.exp(inter - m_t) * qc[a])

        for a in range(2):
            k, v_ext = ops[a][1:]
            wk, decay = gate[a][3:]
            kw_t = (k.astype(jnp.float32) * wk).T.astype(jnp.bfloat16)
            state_ref[a] = decay * state_ref[a] + _dot(kw_t, v_ext)

        hn = []
        for a, (w0, valid, spare) in enumerate(windows):
            nd, m_t = nds[a], gate[a][2]
            den = jnp.sum(jnp.where(lane == spare, nd, 0.0), axis=1, keepdims=True)
            num = jnp.where(valid, nd, 0.0)
            hh = num / jnp.maximum(jnp.abs(den), jnp.exp(-m_t))
            msq = jnp.sum(hh * hh, axis=1, keepdims=True) * (1.0 / C_HEAD_DIM)
            hn.append(hh * lax.rsqrt(msq + 1e-6) * ng[:, w0:w0 + C_WIN])

        mid = C_PAIR - C_WIN
        hn_pair = jnp.concatenate([hn[0][:, :mid], hn[0][:, mid:] + hn[1][:, :C_WIN - mid], hn[1][:, C_WIN - mid:]],
                                  axis=1)
        og = og_ref[pl.ds(st, chunk), :].astype(jnp.float32)
        o_ref[pl.ds(st, chunk), :] = (_sigmoid(og) * hn_pair).astype(o_ref.dtype)
        return tuple(m_out)

    m0 = jnp.zeros((1, 1), jnp.float32)
    lax.fori_loop(0, n_chunks, step, (m0, m0), unroll=4)


def _mlstm(proj, gates, conv_w, norm_g, w_out, li, chunk=MLSTM_CHUNK, halo=MLSTM_HALO):
    b, s, _ = proj.shape
    chunk = min(chunk, s)
    n_pairs = C_HEADS // 2
    base = OFF_C // C_PAIR
    d_in, d = w_out.shape[1:]
    slab = d_in // (b * n_pairs)
    assert slab * b * n_pairs == d_in and slab % BF16_ROW_TILE == 0

    def pspec(t):
        return pl.BlockSpec((None, s, C_PAIR), lambda bi, p, t=t: (bi, 0, base + n_pairs * t + p))

    return pl.pallas_call(
        functools.partial(_mlstm_kernel, chunk=chunk, halo=halo),
        grid=(b, n_pairs),
        in_specs=[
            pspec(0), pspec(1), pspec(2), pspec(3),
            pl.BlockSpec((None, s, GATE_LANES), lambda bi, p: (bi, 0, 0)),
            pl.BlockSpec((None, CONV_K, C_PAIR), lambda bi, p: (li, 0, p)),
            pl.BlockSpec((None, CONV_K, C_PAIR), lambda bi, p: (li, 0, n_pairs + p)),
            pl.BlockSpec((1, C_PAIR), lambda bi, p: (0, 0)),
            pl.BlockSpec((None, slab, d), lambda bi, p: (li, bi * n_pairs + p, 0)),
        ],
        out_specs=[pl.BlockSpec((None, s, C_PAIR), lambda bi, p: (bi, 0, p)),
                   pl.BlockSpec((slab, d), lambda bi, p: (bi * n_pairs + p, 0))],
        out_shape=[jax.ShapeDtypeStruct((b, s, C_WIDTH), jnp.bfloat16),
                   jax.ShapeDtypeStruct((d_in, d), jnp.bfloat16)],
        scratch_shapes=[
            pltpu.VMEM((s, C_PAIR), jnp.bfloat16),
            pltpu.VMEM((s, C_PAIR), jnp.bfloat16),
            pltpu.VMEM((2, C_WIN, C_WIN), jnp.float32),
            pltpu.VMEM((GATE_LANES, chunk), jnp.float32),
        ],
        compiler_params=_params(("parallel", "parallel")),
        name="mlstm",
    )(proj, proj, proj, proj, gates, conv_w, conv_w, norm_g, w_out)


def _out_proj_ln_kernel(ya_ref, yb_ref, yc_ref, w16_ref, x_ref, g_ref, b_ref, o32_ref, o16_ref, *, alpha):
    half = x_ref.shape[0] // 2

    def project(r):
        rows = slice(r * half, (r + 1) * half)
        mixed = jnp.concatenate([ya_ref[rows, :], yb_ref[rows, :], yc_ref[rows, :]], axis=1)
        return _dot(mixed, w16_ref[...])

    nxt = project(0)
    for r in range(2):
        mix = nxt
        if r == 0:
            nxt = project(1)
        rows = slice(r * half, (r + 1) * half)
        y = _layer_norm(alpha * x_ref[rows, :] + mix, g_ref[...], b_ref[...])
        o32_ref[rows, :] = y
        o16_ref[rows, :] = y.astype(o16_ref.dtype)


def _out_proj_ln(ya, yb, yc, wo16, x, g, b, li, alpha, tm=OUT_PROJ_ROWS):
    m, d = x.shape
    tm = min(tm, m)
    row = lambda w: pl.BlockSpec((tm, w), lambda i: (i, 0))
    vec = pl.BlockSpec((None, 1, d), lambda i: (li, 0, 0))
    return pl.pallas_call(
        functools.partial(_out_proj_ln_kernel, alpha=alpha),
        grid=(m // tm,),
        in_specs=[row(ya.shape[1]), row(yb.shape[1]), row(yc.shape[1]),
                  pl.BlockSpec(wo16.shape, lambda i: (0, 0), pipeline_mode=pl.Buffered(1)),
                  row(d), vec, vec],
        out_specs=[row(d), row(d)],
        out_shape=[jax.ShapeDtypeStruct((m, d), jnp.float32),
                   jax.ShapeDtypeStruct((m, d), jnp.bfloat16)],
        compiler_params=_params(("parallel",)),
        name="out_proj_ln",
    )(ya, yb, yc, wo16, x, g, b)


def _ffn_up_kernel(x_ref, wg_ref, wu_ref, o_ref, wg16_ref, wu16_ref):
    @pl.when(pl.program_id(1) == 0)
    def _():
        wg16_ref[...] = wg_ref[...].astype(jnp.bfloat16)
        wu16_ref[...] = wu_ref[...].astype(jnp.bfloat16)

    slab = x_ref.shape[0] // FFN_SLABS

    def products(r):
        x = x_ref[r * slab:(r + 1) * slab, :]
        return _dot(x, wg16_ref[...]), _dot(x, wu16_ref[...])

    nxt = products(0)
    for r in range(FFN_SLABS):
        a, u = nxt
        if r + 1 < FFN_SLABS:
            nxt = products(r + 1)
        o_ref[r * slab:(r + 1) * slab, :] = (a * _sigmoid(a) * u).astype(o_ref.dtype)


def _ffn_up(xb, w_gate, w_up, li, tm=FFN_UP_ROWS, tn=FFN_UP_COLS):
    m, k = xb.shape
    n = w_gate.shape[2]
    tm, tn = min(tm, m), min(tn, n)
    return pl.pallas_call(
        _ffn_up_kernel,
        grid=(n // tn, m // tm),
        in_specs=[
            pl.BlockSpec((tm, k), lambda j, i: (i, 0)),
            pl.BlockSpec((None, k, tn), lambda j, i: (li, 0, j)),
            pl.BlockSpec((None, k, tn), lambda j, i: (li, 0, j)),
        ],
        out_specs=pl.BlockSpec((tm, tn), lambda j, i: (i, j)),
        out_shape=jax.ShapeDtypeStruct((m, n), jnp.bfloat16),
        scratch_shapes=[pltpu.VMEM((k, tn), jnp.bfloat16), pltpu.VMEM((k, tn), jnp.bfloat16)],
        compiler_params=_params(("arbitrary", "arbitrary")),
        name="ffn_up",
    )(xb, w_gate, w_up)


def _ffn_down_ln_kernel(h_ref, w_ref, x_ref, g_ref, b_ref, *rest, alpha, next_gates, single_k):
    if next_gates:
        wg_ref, gb_ref, o32_ref, o16_ref, gate_ref = rest
    else:
        (o32_ref,) = rest

    def finish(ffn):
        y = _layer_norm(alpha * x_ref[...] + ffn, g_ref[...], b_ref[...])
        o32_ref[...] = y
        if next_gates:
            y16 = y.astype(jnp.bfloat16)
            o16_ref[...] = y16
            gate_ref[...] = _gate_proj(y16, wg_ref, gb_ref)

    if single_k:
        finish(_dot(h_ref[...], w_ref[...]))
        return

    kk = pl.program_id(1)

    @pl.when(kk == 0)
    def _():
        o32_ref[...] = jnp.zeros(o32_ref.shape, jnp.float32)

    o32_ref[...] += _dot(h_ref[...], w_ref[...])

    @pl.when(kk == pl.num_programs(1) - 1)
    def _():
        finish(o32_ref[...])


def _ffn_down_ln(hid, wd16, x, g, b, li, alpha, w_in=None, gb_next=None, tm=FFN_DOWN_ROWS, tk=None):
    m, f = hid.shape
    d = wd16.shape[1]
    tm, tk = min(tm, m), f if tk is None else min(tk, f)
    next_gates = w_in is not None
    vec = pl.BlockSpec((None, 1, d), lambda i, kk: (li, 0, 0))
    row = pl.BlockSpec((tm, d), lambda i, kk: (i, 0))
    in_specs = [pl.BlockSpec((tm, tk), lambda i, kk: (i, kk)),
                pl.BlockSpec((tk, d), lambda i, kk: (kk, 0),
                             **({"pipeline_mode": pl.Buffered(1)} if tk == f else {})),
                row, vec, vec]
    out_specs = [row]
    out_shape = [jax.ShapeDtypeStruct((m, d), jnp.float32)]
    args = [hid, wd16, x, g, b]
    if next_gates:
        in_specs += [_gate_weight_spec(li + 1, d), pl.BlockSpec((1, GATE_LANES), lambda i, kk: (0, 0))]
        out_specs += [row, pl.BlockSpec((tm, GATE_LANES), lambda i, kk: (i, 0))]
        out_shape += [jax.ShapeDtypeStruct((m, d), jnp.bfloat16),
                      jax.ShapeDtypeStruct((m, GATE_LANES), jnp.float32)]
        args += [w_in, gb_next]
    return pl.pallas_call(
        functools.partial(_ffn_down_ln_kernel, alpha=alpha, next_gates=next_gates, single_k=tk == f),
        grid=(m // tm, f // tk),
        in_specs=in_specs,
        out_specs=out_specs,
        out_shape=out_shape,
        compiler_params=_params(("parallel", "arbitrary")),
        name="ffn_down_ln",
    )(*args)


def _gate_bias_tile(gate_bias, li):
    return jnp.pad(gate_bias[li], (0, GATE_LANES - 2 * C_HEADS)).reshape(1, GATE_LANES)


def kernel(x, w_in, gate_bias, conv_w, lam, subln_a, norm_c, rel_bias, w_out, ln1_g, ln1_b,
           w_gate, w_up, w_down, ln2_g, ln2_b):
    bsz, s_len, d = x.shape
    depth = w_in.shape[0]
    alpha = (2.0 * depth) ** 0.25
    m = bsz * s_len
    x32 = x.reshape(m, d)
    ln1_g, ln1_b, ln2_g, ln2_b = (p.reshape(depth, 1, d) for p in (ln1_g, ln1_b, ln2_g, ln2_b))
    w_in_t = jnp.swapaxes(w_in, 1, 2)
    x16, gates = _cast_gate(x32, w_in_t, _gate_bias_tile(gate_bias, 0), 0)
    for li in range(depth):
        proj = _in_proj(x16, w_in_t, li).reshape(bsz, s_len, OFF_G)
        gates = gates.reshape(bsz, s_len, GATE_LANES)
        ya, wd16 = _attn_a(proj, lam, subln_a, w_down, li)
        yb = _attn_b(proj, rel_bias[li])
        norm_g = jnp.tile(norm_c[li], 2).reshape(1, C_PAIR)
        yc, wo16 = _mlstm(proj, gates, conv_w, norm_g, w_out, li)
        x32, x16 = _out_proj_ln(ya.reshape(m, A_WIDTH), yb.reshape(m, B_WIDTH), yc.reshape(m, C_WIDTH),
                                wo16, x32, ln1_g, ln1_b, li, alpha)
        hid = _ffn_up(x16, w_gate, w_up, li)
        if li + 1 < depth:
            x32, x16, gates = _ffn_down_ln(hid, wd16, x32, ln2_g, ln2_b, li, alpha,
                                           w_in_t, _gate_bias_tile(gate_bias, li + 1))
        else:
            (x32,) = _ffn_down_ln(hid, wd16, x32, ln2_g, ln2_b, li, alpha)
    return x32.reshape(bsz, s_len, d)
```
